```python
import jax
import jax.numpy as jnp
from jax import lax

D_MODEL = 2048
BATCH = 2
SEQ = 4096
DEPTH = 2

CTX_LEN = 256
GRID_W = 64
RMS_EPS = 1e-6

GLA_HEADS = 4
GLA_DK = D_MODEL // 2 // GLA_HEADS
GLA_DV = D_MODEL // GLA_HEADS
GLA_RANK = 16
GLA_TAU = 16.0
GLA_CHUNK = 64

ATT_HD = 64
ATT_HQ = D_MODEL // ATT_HD
ATT_HKV = ATT_HQ // 8
ATT_WINDOW = 128
ATT_BLOCK = 128
ROPE_BASE = 10000.0

D_FF = 5632
N_EXPERTS = 8
TOP_K = 2
N_DENSE = (DEPTH + 1) // 2
N_MOE = DEPTH // 2

IN_WIDTHS = (
    GLA_HEADS * GLA_DK,
    GLA_HEADS * GLA_DK,
    GLA_HEADS * GLA_DV,
    GLA_HEADS * GLA_DV,
    GLA_RANK,
    GLA_RANK,
    ATT_HQ * ATT_HD,
    ATT_HKV * ATT_HD,
    ATT_HKV * ATT_HD,
    D_MODEL,
    D_MODEL,
)
D_IN = sum(IN_WIDTHS)

kernel_name = 'hybrid_gla_swa_moe_diffusion_block'


def _in_split_points():
    pts, acc = [], 0
    for w in IN_WIDTHS[:-1]:
        acc += w
        pts.append(acc)
    return pts


def _rmsnorm(x, g):
    xf = x.astype(jnp.float32)
    y = xf * lax.rsqrt(jnp.mean(xf * xf, axis=-1, keepdims=True) + RMS_EPS)
    return (y * g.astype(jnp.float32)).astype(x.dtype)


def _heads(t, n):
    return t.reshape(t.shape[:-1] + (n, t.shape[-1] // n))


def _rope_tables(rows):
    row = jnp.repeat(jnp.arange(rows), GRID_W)
    col = jnp.tile(jnp.arange(GRID_W), rows)
    half = ATT_HD // 2
    inv = ROPE_BASE ** (-jnp.arange(0, half, 2, dtype=jnp.float32) / half)

    def angles(p):
        a = p.astype(jnp.float32)[:, None] * inv[None, :]
        return jnp.concatenate([a, a], axis=-1)

    ang = jnp.concatenate([angles(row), angles(col)], axis=-1)
    return jnp.cos(ang), jnp.sin(ang)


def _rot_half(x):
    a, b = jnp.split(x, 2, axis=-1)
    return jnp.concatenate([-b, a], axis=-1)


def _apply_rope2d(x, cos, sin):
    half = ATT_HD // 2
    rot = jnp.concatenate([_rot_half(x[..., :half]), _rot_half(x[..., half:])], axis=-1)
    c = cos[None, :, None, :]
    s = sin[None, :, None, :]
    return (x.astype(jnp.float32) * c + rot.astype(jnp.float32) * s).astype(x.dtype)


def _gla_scan(q, k, v, log_a, s0):
    B, L, H, _ = q.shape
    DV = v.shape[-1]
    C = GLA_CHUNK
    N = L // C

    def chunks(t):
        return jnp.moveaxis(t.astype(jnp.float32).reshape(B, N, C, H, t.shape[-1]), 1, 0)

    q, k, v, log_a = chunks(q), chunks(k), chunks(v), chunks(log_a)
    b = jnp.cumsum(log_a, axis=2)
    b_last = b[:, :, -1:]
    q_in = q * jnp.exp(b)
    k_in = k * jnp.exp(-b)
    k_out = k * jnp.exp(b_last - b)
    mask = jnp.tril(jnp.ones((C, C), dtype=bool))
    a_intra = jnp.where(mask, jnp.einsum('nbihd,nbjhd->nbhij', q_in, k_in), 0.0)
    o_intra = jnp.einsum('nbhij,nbjhv->nbihv', a_intra, v)
    decay = jnp.exp(b_last[:, :, 0])

    def step(s, inp):
        q_n, k_n, v_n, d_n = inp
        o_n = jnp.einsum('bihd,bhdv->bihv', q_n, s)
        s = d_n[..., None] * s + jnp.einsum('bjhd,bjhv->bhdv', k_n, v_n)
        return s, o_n

    s_fin, o_inter = lax.scan(step, s0, (q_in, k_out, v, decay))
    o = jnp.moveaxis(o_intra + o_inter, 0, 1).reshape(B, L, H, DV)
    return o, s_fin


def _gla_readout(o, r, g_gla):
    on = _rmsnorm(o, g_gla)
    B, L = r.shape[:2]
    return on.reshape(B, L, GLA_HEADS * GLA_DV).astype(r.dtype) * jax.nn.silu(r)


def _window_attention(q, k, v, kc, vc, sink):
    B, L = q.shape[:2]
    G = ATT_HQ // ATT_HKV
    nb = L // ATT_BLOCK
    span = ATT_BLOCK + 2 * ATT_WINDOW
    n_ctx = kc.shape[1]
    pad = ((0, 0), (ATT_WINDOW, ATT_WINDOW), (0, 0), (0, 0))
    kp = jnp.pad(k, pad)
    vp = jnp.pad(v, pad)
    qb = jnp.moveaxis(q.reshape(B, nb, ATT_BLOCK, ATT_HKV, G, ATT_HD), 1, 0)
    sink_l = sink.astype(jnp.float32).reshape(1, ATT_HKV, G, 1, 1)

    def one_block(args):
        i, q_i = args
        start = i * ATT_BLOCK
        k_i = lax.dynamic_slice_in_dim(kp, start, span, axis=1)
        v_i = lax.dynamic_slice_in_dim(vp, start, span, axis=1)
        tq = start + jnp.arange(ATT_BLOCK)
        tk = start - ATT_WINDOW + jnp.arange(span)
        ok = (jnp.abs(tq[:, None] - tk[None, :]) <= ATT_WINDOW) & ((tk >= 0) & (tk < L))[None, :]
        s_loc = jnp.einsum('bqhgd,bkhd->bhgqk', q_i, k_i).astype(jnp.float32)
        s_loc = jnp.where(ok, s_loc, -jnp.inf)
        s_ctx = jnp.einsum('bqhgd,bkhd->bhgqk', q_i, kc).astype(jnp.float32)
        s_snk = jnp.broadcast_to(sink_l, s_loc.shape[:-1] + (1,))
        p = jax.nn.softmax(jnp.concatenate([s_loc, s_ctx, s_snk], axis=-1), axis=-1).astype(v.dtype)
        o = jnp.einsum('bhgqk,bkhd->bqhgd', p[..., :span], v_i)
        o = o + jnp.einsum('bhgqk,bkhd->bqhgd', p[..., span:span + n_ctx], vc)
        return o

    ob = lax.map(one_block, (jnp.arange(nb), qb))
    return jnp.moveaxis(ob, 0, 1).reshape(B, L, ATT_HQ * ATT_HD)


def _ctx_attention(qc, kc, vc, sink):
    B, Lc = qc.shape[:2]
    G = ATT_HQ // ATT_HKV
    q = qc.reshape(B, Lc, ATT_HKV, G, ATT_HD)
    s = jnp.einsum('bqhgd,bkhd->bhgqk', q, kc).astype(jnp.float32)
    snk = jnp.broadcast_to(sink.astype(jnp.float32).reshape(1, ATT_HKV, G, 1, 1), s.shape[:-1] + (1,))
    p = jax.nn.softmax(jnp.concatenate([s, snk], axis=-1), axis=-1)[..., :-1].astype(vc.dtype)
    o = jnp.einsum('bhgqk,bkhd->bqhgd', p, vc)
    return o.reshape(B, Lc, ATT_HQ * ATT_HD)


def _token_mixer(hx, hc, w_in, w_lr_f, b_lr_f, w_lr_b, b_lr_b, g_gla, att_sink,
                 w_br_gla, w_br_att, w_out, cos, sin, with_ctx_out):
    pts = _in_split_points()
    px = jnp.split(hx @ w_in, pts, axis=-1)
    pc = jnp.split(hc @ w_in, pts, axis=-1)

    def gla_in(p):
        q = _heads(p[0], GLA_HEADS) * GLA_DK ** -0.5
        k = _heads(p[1], GLA_HEADS)
        v = _heads(p[2], GLA_HEADS)
        la_f = _heads(jax.nn.log_sigmoid((p[4] @ w_lr_f + b_lr_f).astype(jnp.float32)) / GLA_TAU, GLA_HEADS)
        la_b = _heads(jax.nn.log_sigmoid((p[5] @ w_lr_b + b_lr_b).astype(jnp.float32)) / GLA_TAU, GLA_HEADS)
        return q, k, v, la_f, la_b

    def flip(t):
        return jnp.flip(t, axis=1)

    qx, kx, vx, lfx, lbx = gla_in(px)
    qc, kc, vc, lfc, lbc = gla_in(pc)
    s0 = jnp.zeros((hx.shape[0], GLA_HEADS, GLA_DK, GLA_DV), jnp.float32)
    oc_f, sc_f = _gla_scan(qc, kc, vc, lfc, s0)
    oc_b, sc_b = _gla_scan(flip(qc), flip(kc), flip(vc), flip(lbc), s0)
    ox_f, _ = _gla_scan(qx, kx, vx, lfx, sc_f)
    ox_b, _ = _gla_scan(flip(qx), flip(kx), flip(vx), flip(lbx), sc_b)
    gla_x = _gla_readout(ox_f + flip(ox_b), px[3], g_gla)

    qa = _apply_rope2d(_heads(px[6], ATT_HQ), cos, sin) * ATT_HD ** -0.5
    ka = _apply_rope2d(_heads(px[7], ATT_HKV), cos, sin)
    va = _heads(px[8], ATT_HKV)
    qca = _heads(pc[6], ATT_HQ) * ATT_HD ** -0.5
    kca = _heads(pc[7], ATT_HKV)
    vca = _heads(pc[8], ATT_HKV)
    att_x = _window_attention(qa, ka, va, kca, vca, att_sink)

    def merge(p, gla_o, att_o):
        return (jax.nn.sigmoid(p[9]) * (gla_o @ w_br_gla)
                + jax.nn.sigmoid(p[10]) * (att_o @ w_br_att)) @ w_out

    y_x = merge(px, gla_x, att_x)
    if not with_ctx_out:
        return y_x, None
    gla_c = _gla_readout(oc_f + flip(oc_b), pc[3], g_gla)
    att_c = _ctx_attention(qca, kca, vca, att_sink)
    return y_x, merge(pc, gla_c, att_c)


def _swiglu(h, wg, wu, wd):
    return (jax.nn.silu(h @ wg) * (h @ wu)) @ wd


def _moe(h, w_router, b_router, wg, wu, wd):
    logits = (h @ w_router).astype(jnp.float32) + b_router.astype(jnp.float32)
    top_v, top_i = lax.top_k(logits, TOP_K)
    top_w = jax.nn.softmax(top_v, axis=-1)
    comb = jnp.sum(jax.nn.one_hot(top_i, N_EXPERTS, dtype=jnp.float32) * top_w[..., None], axis=-2)
    comb = comb.astype(h.dtype)
    out = jnp.zeros_like(h)
    for e in range(N_EXPERTS):
        out = out + comb[..., e:e + 1] * _swiglu(h, wg[e], wu[e], wd[e])
    return out


def _channel_mixer(layer, h, w_ffn_gate, w_ffn_up, w_ffn_down, w_router, b_router,
                   w_exp_gate, w_exp_up, w_exp_down):
    i = layer // 2
    if layer % 2 == 0:
        return _swiglu(h, w_ffn_gate[i], w_ffn_up[i], w_ffn_down[i])
    return _moe(h, w_router[i], b_router[i], w_exp_gate[i], w_exp_up[i], w_exp_down[i])


def setup_inputs(seed: int = 0) -> dict:
    key = jax.random.key(seed)
    ks = iter(jax.random.split(key, 32))
    f32 = jnp.float32

    def nrm(shape, scale):
        return jax.random.normal(next(ks), shape, f32) * scale

    D = D_MODEL
    GV = GLA_HEADS * GLA_DV
    AW = ATT_HQ * ATT_HD
    GK = GLA_HEADS * GLA_DK
    return {
        'x': nrm((BATCH, SEQ, D), 1.0),
        'c': nrm((BATCH, D), 1.0),
        'ctx': nrm((BATCH, CTX_LEN, D), 1.0),
        'c_ctx': nrm((D,), 1.0),
        'w_mod': nrm((DEPTH, D, 6 * D), D ** -0.5),
        'b_mod': nrm((DEPTH, 6 * D), 0.02),
        'g_pre_mix': 1.0 + nrm((DEPTH, D), 0.05),
        'g_post_mix': 1.0 + nrm((DEPTH, D), 0.05),
        'g_pre_ffn': 1.0 + nrm((DEPTH, D), 0.05),
        'g_post_ffn': 1.0 + nrm((DEPTH, D), 0.05),
        'w_in': nrm((DEPTH, D, D_IN), D ** -0.5),
        'w_lr_f': nrm((DEPTH, GLA_RANK, GK), GLA_RANK ** -0.5),
        'b_lr_f': nrm((DEPTH, GK), 0.1),
        'w_lr_b': nrm((DEPTH, GLA_RANK, GK), GLA_RANK ** -0.5),
        'b_lr_b': nrm((DEPTH, GK), 0.1),
        'g_gla': 1.0 + nrm((DEPTH, GLA_DV), 0.05),
        'att_sink': nrm((DEPTH, ATT_HQ), 0.5),
        'w_br_gla': nrm((DEPTH, GV, D), GV ** -0.5),
        'w_br_att': nrm((DEPTH, AW, D), AW ** -0.5),
        'w_out': nrm((DEPTH, D, D), D ** -0.5),
        'w_ffn_gate': nrm((N_DENSE, D, D_FF), D ** -0.5),
        'w_ffn_up': nrm((N_DENSE, D, D_FF), D ** -0.5),
        'w_ffn_down': nrm((N_DENSE, D_FF, D), D_FF ** -0.5),
        'w_router': nrm((N_MOE, D, N_EXPERTS), D ** -0.5),
        'b_router': nrm((N_MOE, N_EXPERTS), 0.01),
        'w_exp_gate': nrm((N_MOE, N_EXPERTS, D, D_FF), D ** -0.5),
        'w_exp_up': nrm((N_MOE, N_EXPERTS, D, D_FF), D ** -0.5),
        'w_exp_down': nrm((N_MOE, N_EXPERTS, D_FF, D), D_FF ** -0.5),
    }


def reference(x, c, ctx, c_ctx, w_mod, b_mod, g_pre_mix, g_post_mix, g_pre_ffn, g_post_ffn,
              w_in, w_lr_f, b_lr_f, w_lr_b, b_lr_b, g_gla, att_sink, w_br_gla, w_br_att, w_out,
              w_ffn_gate, w_ffn_up, w_ffn_down, w_router, b_router, w_exp_gate, w_exp_up, w_exp_down):
    L = x.shape[1]
    ROWS = L // GRID_W
    cos, sin = _rope_tables(ROWS)
    for l in range(DEPTH):
        last = l == DEPTH - 1
        mod_x = jax.nn.silu(c) @ w_mod[l] + b_mod[l]
        mod_c = jax.nn.silu(c_ctx) @ w_mod[l] + b_mod[l]
        sh1, sc1, gt1, sh2, sc2, gt2 = jnp.split(mod_x[:, None, :], 6, axis=-1)
        ch1, cs1, cg1, ch2, cs2, cg2 = jnp.split(mod_c, 6, axis=-1)

        hx = _rmsnorm(x, g_pre_mix[l]) * (1.0 + sc1) + sh1
        hc = _rmsnorm(ctx, g_pre_mix[l]) * (1.0 + cs1) + ch1
        yx, yc = _token_mixer(hx, hc, w_in[l], w_lr_f[l], b_lr_f[l], w_lr_b[l], b_lr_b[l],
                              g_gla[l], att_sink[l], w_br_gla[l], w_br_att[l], w_out[l],
                              cos, sin, not last)
        x = x + gt1 * _rmsnorm(yx, g_post_mix[l])
        if not last:
            ctx = ctx + cg1 * _rmsnorm(yc, g_post_mix[l])

        hx = _rmsnorm(x, g_pre_ffn[l]) * (1.0 + sc2) + sh2
        fx = _channel_mixer(l, hx, w_ffn_gate, w_ffn_up, w_ffn_down, w_router, b_router,
                            w_exp_gate, w_exp_up, w_exp_down)
        x = x + gt2 * _rmsnorm(fx, g_post_ffn[l])
        if not last:
            hc = _rmsnorm(ctx, g_pre_ffn[l]) * (1.0 + cs2) + ch2
            fc = _channel_mixer(l, hc, w_ffn_gate, w_ffn_up, w_ffn_down, w_router, b_router,
                                w_exp_gate, w_exp_up, w_exp_down)
            ctx = ctx + cg2 * _rmsnorm(fc, g_post_ffn[l])
    return x
```

```python
import functools

import jax
import jax.numpy as jnp
from jax import lax
from jax.experimental import pallas as pl
from jax.experimental.pallas import tpu as pltpu

BF = jnp.bfloat16
F32 = jnp.float32

D = 2048
B = 2
L = 4096
LC = 256
T = B * L
TC = B * LC
R = TC + T
DEPTH = 2
GRID_W = 64
EPS = 1e-6

GH = 4
GDK = 256
GDV = 512
GRANK = 16
GTAU = 16.0
GC = 64

HD = 64
HQ = 32
HKV = 4
WIN = 128
ROPE_BASE = 10000.0

DFF = 5632
NE = 8

TM = 512
NT_ALL = R // TM
TPB = L // TM
VMEM_LIMIT = 56 * 1024 * 1024

C_GQ, C_GK, C_GV, C_GR, C_LF, C_LB, C_AQ, C_AK, C_AV, C_G1, C_G2, C_END = (
    0, 1024, 2048, 4096, 6144, 6160, 6176, 8224, 8480, 8736, 10784, 12832)


def _cparams(sem):
    return pltpu.CompilerParams(dimension_semantics=sem, vmem_limit_bytes=VMEM_LIMIT)


def _mrow(gi):
    return jnp.where(gi == 0, 2, (gi - 1) // TPB)


def _sigmoid(z):
    return 1.0 / (1.0 + jnp.exp(-z))


def _silu(z):
    return z * _sigmoid(z)


def _split(a):
    hi = a.astype(BF)
    lo = (a - hi.astype(F32)).astype(BF)
    return hi, lo


def _mod_kernel(c_ref, w_ref, b_ref, o_ref):
    a = _silu(c_ref[...]).astype(BF)
    o_ref[...] = jnp.dot(a, w_ref[...].astype(BF), preferred_element_type=F32) + b_ref[...]


def _modulation(cvec, w_mod, b_mod):
    tn = 1024
    return pl.pallas_call(
        _mod_kernel,
        grid=(DEPTH, 6 * D // tn),
        in_specs=[
            pl.BlockSpec((8, D), lambda l, j: (0, 0)),
            pl.BlockSpec((None, D, tn), lambda l, j: (l, 0, j)),
            pl.BlockSpec((None, 1, tn), lambda l, j: (l, 0, j)),
        ],
        out_specs=pl.BlockSpec((None, 8, tn), lambda l, j: (l, 0, j)),
        out_shape=jax.ShapeDtypeStruct((DEPTH, 8, 6 * D), F32),
        compiler_params=_cparams(("arbitrary", "arbitrary")),
        name="modulation",
    )(cvec, w_mod, b_mod.reshape(DEPTH, 1, 6 * D))


def _norm_mod(x, g, sc, sh):
    ms = jnp.mean(x * x, axis=-1, keepdims=True)
    return (x * lax.rsqrt(ms + EPS) * g) * (1.0 + sc) + sh


def _prenorm_kernel(x_ref, g_ref, sc_ref, sh_ref, o_ref):
    o_ref[...] = _norm_mod(x_ref[...], g_ref[...], sc_ref[...], sh_ref[...]).astype(BF)


def _prenorm(xa, g, modl, k_sc, k_sh):
    return pl.pallas_call(
        _prenorm_kernel,
        grid=(NT_ALL,),
        in_specs=[
            pl.BlockSpec((TM, D), lambda i: (i, 0)),
            pl.BlockSpec((1, D), lambda i: (0, 0)),
            pl.BlockSpec((None, 1, D), lambda i: (_mrow(i) * 6 + k_sc, 0, 0)),
            pl.BlockSpec((None, 1, D), lambda i: (_mrow(i) * 6 + k_sh, 0, 0)),
        ],
        out_specs=pl.BlockSpec((TM, D), lambda i: (i, 0)),
        out_shape=jax.ShapeDtypeStruct((R, D), BF),
        compiler_params=_cparams(("arbitrary",)),
        name="prenorm",
    )(xa, g.reshape(1, D), modl, modl)


def _proj_kernel(a_ref, w_ref, o_ref, wbf_ref):
    @pl.when(pl.program_id(1) == 0)
    def _():
        wbf_ref[...] = w_ref[...].astype(BF)

    o_ref[...] = jnp.dot(a_ref[...], wbf_ref[...], preferred_element_type=F32).astype(o_ref.dtype)


def _proj(h, w, w_idx, n_cols, tn, out_dtype, name):
    nlead = len(w_idx)
    return pl.pallas_call(
        _proj_kernel,
        grid=(n_cols // tn, NT_ALL),
        in_specs=[
            pl.BlockSpec((TM, D), lambda j, i: (i, 0)),
            pl.BlockSpec((None,) * nlead + (D, tn), lambda j, i: tuple(w_idx) + (0, j)),
        ],
        out_specs=pl.BlockSpec((TM, tn), lambda j, i: (i, j)),
        out_shape=jax.ShapeDtypeStruct((R, n_cols), out_dtype),
        scratch_shapes=[pltpu.VMEM((D, tn), BF)],
        compiler_params=_cparams(("arbitrary", "arbitrary")),
        name=name,
    )(h, w)


PB_TN = 512
PB_COLS = 2048 + 4096 + 512
PB_QT = 2048 // PB_TN
PB_KVT = PB_COLS // PB_TN - 1


def _rope(x, cos, sin_a, sin_b):
    return x * cos + pltpu.roll(x, 112, axis=1) * sin_a + pltpu.roll(x, 16, axis=1) * sin_b


def _proj_att_kernel(a_ref, w_ref, cos_ref, sa_ref, sb_ref, o_ref, wbf_ref):
    j = pl.program_id(0)
    i = pl.program_id(1)

    @pl.when(i == 0)
    def _():
        wbf_ref[...] = w_ref[...].astype(BF)

    acc = jnp.dot(a_ref[...], wbf_ref[...], preferred_element_type=F32)
    latent = i > 0
    is_q = j < PB_QT
    is_kv = j == PB_KVT

    @pl.when(jnp.logical_and(is_q, latent))
    def _():
        for s in range(PB_TN // 128):
            x = acc[:, s * 128:(s + 1) * 128]
            y = _rope(x, cos_ref[...], sa_ref[...], sb_ref[...]) * (HD ** -0.5)
            o_ref[:, s * 128:(s + 1) * 128] = y.astype(BF)

    @pl.when(jnp.logical_and(is_q, jnp.logical_not(latent)))
    def _():
        o_ref[...] = (acc * (HD ** -0.5)).astype(BF)

    @pl.when(jnp.logical_and(is_kv, latent))
    def _():
        for s in range(HKV * HD // 128):
            x = acc[:, s * 128:(s + 1) * 128]
            y = _rope(x, cos_ref[...], sa_ref[...], sb_ref[...])
            o_ref[:, s * 128:(s + 1) * 128] = y.astype(BF)
        o_ref[:, HKV * HD:] = acc[:, HKV * HD:].astype(BF)

    @pl.when(jnp.logical_not(jnp.logical_or(is_q, jnp.logical_and(is_kv, latent))))
    def _():
        o_ref[...] = acc.astype(BF)


def _proj_att(h, wb, cos, sin_a, sin_b):
    def tab(j, i):
        return (jnp.where(i == 0, 0, (i - 1) % TPB), 0)

    return pl.pallas_call(
        _proj_att_kernel,
        grid=(PB_COLS // PB_TN, NT_ALL),
        in_specs=[
            pl.BlockSpec((TM, D), lambda j, i: (i, 0)),
            pl.BlockSpec((D, PB_TN), lambda j, i: (0, j)),
            pl.BlockSpec((TM, 128), tab),
            pl.BlockSpec((TM, 128), tab),
            pl.BlockSpec((TM, 128), tab),
        ],
        out_specs=pl.BlockSpec((TM, PB_TN), lambda j, i: (i, j)),
        out_shape=jax.ShapeDtypeStruct((R, PB_COLS), BF),
        scratch_shapes=[pltpu.VMEM((D, PB_TN), BF)],
        compiler_params=_cparams(("arbitrary", "arbitrary")),
        name="proj_att",
    )(h, wb, cos, sin_a, sin_b)


def _rope_tables():
    rows = L // GRID_W
    row = jnp.repeat(jnp.arange(rows), GRID_W)
    col = jnp.tile(jnp.arange(GRID_W), rows)
    half = HD // 2
    inv = ROPE_BASE ** (-jnp.arange(0, half, 2, dtype=F32) / half)

    def angles(p):
        a = p.astype(F32)[:, None] * inv[None, :]
        return jnp.concatenate([a, a], axis=-1)

    ang = jnp.concatenate([angles(row), angles(col)], axis=-1)
    ang = jnp.concatenate([ang, ang], axis=-1)
    cos, sin = jnp.cos(ang), jnp.sin(ang)
    first = (jnp.arange(128) % 32) < 16
    return cos, jnp.where(first, -sin, 0.0), jnp.where(first, 0.0, sin)


GG = 256
NG = 1 + L // GG


def _gla_chunk(q, k, v, lr, wlr, blr, s_ref, fwd):
    lh, ll = _split(lr)
    wh, wl = _split(wlr)
    z = (jnp.dot(lh, wh, preferred_element_type=F32) + jnp.dot(ll, wh, preferred_element_type=F32)
         + jnp.dot(lh, wl, preferred_element_type=F32) + blr)
    la = (jnp.minimum(z, 0.0) - jnp.log1p(jnp.exp(-jnp.abs(z)))) * (1.0 / GTAU)
    ri = lax.broadcasted_iota(jnp.int32, (GC, GC), 0)
    ci = lax.broadcasted_iota(jnp.int32, (GC, GC), 1)
    keep = (ri >= ci) if fwd else (ri <= ci)
    tri = jnp.where(keep, 1.0, 0.0).astype(BF)
    ah, al = _split(la)
    bcum = jnp.dot(tri, ah, preferred_element_type=F32) + jnp.dot(tri, al, preferred_element_type=F32)
    tot = jnp.sum(la, axis=0, keepdims=True)
    kf = k.astype(F32)
    q_in = (q.astype(F32) * jnp.exp(bcum) * (GDK ** -0.5)).astype(BF)
    k_in = (kf * jnp.exp(-bcum)).astype(BF)
    k_out = (kf * jnp.exp(tot - bcum)).astype(BF)
    a = lax.dot_general(q_in, k_in, (((1,), (1,)), ((), ())), preferred_element_type=F32)
    a = jnp.where(keep, a, 0.0).astype(BF)
    st = s_ref[...]
    o = jnp.dot(a, v, preferred_element_type=F32)
    o = o + lax.dot_general(q_in, st.astype(BF), (((1,), (1,)), ((), ())), preferred_element_type=F32)
    upd = lax.dot_general(v, k_out, (((0,), (0,)), ((), ())), preferred_element_type=F32)
    s_ref[...] = st * jnp.exp(tot) + upd
    return o


def _gla_kernel(q_ref, k_ref, v_ref, r_ref, lr_ref, wlr_ref, blr_ref, gg_ref, o_ref, s_ref, of_ref):
    d = pl.program_id(2)
    g = pl.program_id(3)

    @pl.when(g == 0)
    def _():
        s_ref[...] = jnp.zeros_like(s_ref)

    def chunk(c, fwd):
        rows = slice(c * GC, (c + 1) * GC)
        return _gla_chunk(q_ref[rows, :], k_ref[rows, :], v_ref[rows, :], lr_ref[rows, :],
                          wlr_ref[...], blr_ref[...], s_ref, fwd)

    @pl.when(d == 0)
    def _():
        for c in range(GG // GC):
            of_ref[g, c * GC:(c + 1) * GC, :] = chunk(c, True)

    @pl.when(d == 1)
    def _():
        slot = jnp.where(g == 0, 0, NG - g)
        for c in reversed(range(GG // GC)):
            rows = slice(c * GC, (c + 1) * GC)
            o = chunk(c, False) + of_ref[slot, rows, :]
            on = o * lax.rsqrt(jnp.mean(o * o, axis=-1, keepdims=True) + EPS) * gg_ref[...]
            o_ref[rows, :] = (on * _silu(r_ref[rows, :].astype(F32))).astype(BF)


def _gla(pa, pc, wlr, blr, g_gla):
    lat0 = TC // GG

    def rb(b, h, d, g):
        return jnp.where(g == 0, b, lat0 + b * (L // GG) + jnp.where(d == 0, g - 1, NG - 1 - g))

    def rb_out(b, h, d, g):
        return jnp.where(d == 0, b, rb(b, h, d, g))

    return pl.pallas_call(
        _gla_kernel,
        grid=(B, GH, 2, NG),
        in_specs=[
            pl.BlockSpec((GG, GDK), lambda b, h, d, g: (rb(b, h, d, g), C_GQ // GDK + h)),
            pl.BlockSpec((GG, GDK), lambda b, h, d, g: (rb(b, h, d, g), C_GK // GDK + h)),
            pl.BlockSpec((GG, GDV), lambda b, h, d, g: (rb(b, h, d, g), C_GV // GDV + h)),
            pl.BlockSpec((GG, GDV), lambda b, h, d, g: (rb(b, h, d, g), C_GR // GDV + h)),
            pl.BlockSpec((GG, 128), lambda b, h, d, g: (rb(b, h, d, g), 0)),
            pl.BlockSpec((None, 128, GDK), lambda b, h, d, g: (d, 0, h)),
            pl.BlockSpec((None, 1, GDK), lambda b, h, d, g: (d, 0, h)),
            pl.BlockSpec((1, GDV), lambda b, h, d, g: (0, 0)),
        ],
        out_specs=pl.BlockSpec((GG, GDV), lambda b, h, d, g: (rb_out(b, h, d, g), h)),
        out_shape=jax.ShapeDtypeStruct((R, GH * GDV), BF),
        scratch_shapes=[pltpu.VMEM((GDV, GDK), F32), pltpu.VMEM((NG, GG, GDV), F32)],
        compiler_params=_cparams(("arbitrary", "arbitrary", "arbitrary", "arbitrary")),
        name="gla",
    )(pa, pa, pa, pa, pc, wlr, blr, g_gla.reshape(1, GDV))


AB = 128
NB = L // AB
NCB = LC // AB
NEG = float("-inf")


def _attn_kernel(q_ref, kvp_ref, kvc_ref, kvn_ref, kvx_ref, sink_ref, o_ref):
    j = pl.program_id(1)
    latent = j >= NCB
    ri = lax.broadcasted_iota(jnp.int32, (AB, AB), 0)
    ci = lax.broadcasted_iota(jnp.int32, (AB, AB), 1)
    ok_p = jnp.logical_and(ci >= ri, j > NCB)
    ok_c = jnp.logical_and(ci >= 0, latent)
    ok_n = jnp.logical_and(ci <= ri, jnp.logical_and(latent, j < NCB + NB - 1))
    bias = jnp.concatenate(
        [jnp.where(ok_p, 0.0, NEG), jnp.where(ok_c, 0.0, NEG), jnp.where(ok_n, 0.0, NEG),
         jnp.zeros((AB, LC), F32)], axis=1)
    for hk in range(HKV):
        ks = slice(hk * HD, (hk + 1) * HD)
        vs = slice(HKV * HD + hk * HD, HKV * HD + (hk + 1) * HD)
        k_all = jnp.concatenate([kvp_ref[:, ks], kvc_ref[:, ks], kvn_ref[:, ks], kvx_ref[:, ks]], axis=0)
        v_all = jnp.concatenate([kvp_ref[:, vs], kvc_ref[:, vs], kvn_ref[:, vs], kvx_ref[:, vs]], axis=0)
        for gq in range(HQ // HKV):
            h = hk * (HQ // HKV) + gq
            q = q_ref[:, h * HD:(h + 1) * HD]
            s = lax.dot_general(q, k_all, (((1,), (1,)), ((), ())), preferred_element_type=F32) + bias
            snk = sink_ref[h]
            m = jnp.maximum(jnp.max(s, axis=-1, keepdims=True), snk)
            e = jnp.exp(s - m)
            den = jnp.sum(e, axis=-1, keepdims=True) + jnp.exp(snk - m)
            o = jnp.dot(e.astype(BF), v_all, preferred_element_type=F32) / den
            o_ref[:, h * HD:(h + 1) * HD] = o.astype(BF)


def _attention(pb, sink):
    lat0 = TC // AB

    def qrow(b, j):
        return jnp.where(j < NCB, b * NCB + j, lat0 + b * NB + j - NCB)

    def krow(off):
        def f(b, j):
            i = jnp.clip(j - NCB + off, 0, NB - 1)
            return (lat0 + b * NB + i, PB_KVT)
        return f

    return pl.pallas_call(
        _attn_kernel,
        grid=(B, NCB + NB),
        in_specs=[
            pl.BlockSpec((AB, HQ * HD), lambda b, j: (qrow(b, j), 0)),
            pl.BlockSpec((AB, PB_TN), krow(-1)),
            pl.BlockSpec((AB, PB_TN), krow(0)),
            pl.BlockSpec((AB, PB_TN), krow(1)),
            pl.BlockSpec((LC, PB_TN), lambda b, j: (b, PB_KVT)),
            pl.BlockSpec(memory_space=pltpu.SMEM),
        ],
        out_specs=pl.BlockSpec((AB, HQ * HD), lambda b, j: (qrow(b, j), 0)),
        out_shape=jax.ShapeDtypeStruct((R, HQ * HD), BF),
        compiler_params=_cparams(("arbitrary", "arbitrary")),
        name="attention",
    )(pb, pb, pb, pb, pb, sink)


MG_TN = 512


def _merge_kernel(gla_ref, att_ref, g1_ref, g2_ref, w1_ref, w2_ref, o_ref, w1b_ref, w2b_ref):
    @pl.when(pl.program_id(1) == 0)
    def _():
        w1b_ref[...] = w1_ref[...].astype(BF)
        w2b_ref[...] = w2_ref[...].astype(BF)

    y1 = jnp.dot(gla_ref[...], w1b_ref[...], preferred_element_type=F32)
    y2 = jnp.dot(att_ref[...], w2b_ref[...], preferred_element_type=F32)
    y = _sigmoid(g1_ref[...].astype(F32)) * y1 + _sigmoid(g2_ref[...].astype(F32)) * y2
    o_ref[...] = y.astype(BF)


def _merge(gla, att, pb, w1, w2, l, t0):
    nt = NT_ALL - t0
    g1c = 2048 // MG_TN
    g2c = 4096 // MG_TN
    return pl.pallas_call(
        _merge_kernel,
        grid=(D // MG_TN, nt),
        in_specs=[
            pl.BlockSpec((TM, GH * GDV), lambda j, i: (i + t0, 0)),
            pl.BlockSpec((TM, HQ * HD), lambda j, i: (i + t0, 0)),
            pl.BlockSpec((TM, MG_TN), lambda j, i: (i + t0, g1c + j)),
            pl.BlockSpec((TM, MG_TN), lambda j, i: (i + t0, g2c + j)),
            pl.BlockSpec((None, GH * GDV, MG_TN), lambda j, i: (l, 0, j)),
            pl.BlockSpec((None, HQ * HD, MG_TN), lambda j, i: (l, 0, j)),
        ],
        out_specs=pl.BlockSpec((TM, MG_TN), lambda j, i: (i, j)),
        out_shape=jax.ShapeDtypeStruct((nt * TM, D), BF),
        scratch_shapes=[pltpu.VMEM((GH * GDV, MG_TN), BF), pltpu.VMEM((HQ * HD, MG_TN), BF)],
        compiler_params=_cparams(("arbitrary", "arbitrary")),
        name="merge",
    )(gla, att, pb, pb, w1, w2)


RN_TK = 512


def _rownorm_kernel(*refs, has_prev, has_scale, final, has_next):
    it = iter(refs)
    a_ref, w_ref = next(it), next(it)
    prev_ref = next(it) if has_prev else None
    scale_ref = next(it) if has_scale else None
    if final:
        x_ref, gt_ref, gp_ref = next(it), next(it), next(it)
    if has_next:
        gn_ref, sc_ref, sh_ref = next(it), next(it), next(it)
    o_ref = next(it)
    h_ref = next(it) if has_next else None
    acc_ref = next(it)
    k = pl.program_id(1)

    @pl.when(k == 0)
    def _():
        acc_ref[...] = jnp.zeros_like(acc_ref)

    acc_ref[...] += jnp.dot(a_ref[...], w_ref[...].astype(BF), preferred_element_type=F32)

    @pl.when(k == pl.num_programs(1) - 1)
    def _():
        y = acc_ref[...]
        if has_scale:
            y = y * scale_ref[...]
        if has_prev:
            y = prev_ref[...] + y
        if not final:
            o_ref[...] = y
        else:
            ms = jnp.mean(y * y, axis=-1, keepdims=True)
            xn = x_ref[...] + gt_ref[...] * (y * lax.rsqrt(ms + EPS) * gp_ref[...])
            o_ref[...] = xn
            if has_next:
                h_ref[...] = _norm_mod(xn, gn_ref[...], sc_ref[...], sh_ref[...]).astype(BF)


def _rownorm(a, w_full, w_idx, kdim, *, a_ctx, t0, prev=None, scale=None, final=None, nxt=None):
    nt = NT_ALL - t0
    nk = kdim // RN_TK
    a_off = t0 if a_ctx else 0
    nlead = len(w_idx)
    args = [a, w_full]
    in_specs = [
        pl.BlockSpec((TM, RN_TK), lambda i, k: (i + a_off, k)),
        pl.BlockSpec((None,) * nlead + (RN_TK, D), lambda i, k: tuple(w_idx) + (k, 0)),
    ]
    if prev is not None:
        args.append(prev)
        in_specs.append(pl.BlockSpec((TM, D), lambda i, k: (i, 0)))
    if scale is not None:
        args.append(scale)
        in_specs.append(pl.BlockSpec((TM, 1), lambda i, k: (i, 0)))
    if final is not None:
        x, x_ctx, modl, k_gate, g_post = final
        x_off = t0 if x_ctx else 0
        args += [x, modl, g_post.reshape(1, D)]
        in_specs += [
            pl.BlockSpec((TM, D), lambda i, k: (i + x_off, 0)),
            pl.BlockSpec((None, 1, D), lambda i, k: (_mrow(i + t0) * 6 + k_gate, 0, 0)),
            pl.BlockSpec((1, D), lambda i, k: (0, 0)),
        ]
    out_shape = [jax.ShapeDtypeStruct((nt * TM, D), F32)]
    out_specs = [pl.BlockSpec((TM, D), lambda i, k: (i, 0))]
    if nxt is not None:
        g_next, modn, k_sc, k_sh = nxt
        args += [g_next.reshape(1, D), modn, modn]
        in_specs += [
            pl.BlockSpec((1, D), lambda i, k: (0, 0)),
            pl.BlockSpec((None, 1, D), lambda i, k: (_mrow(i + t0) * 6 + k_sc, 0, 0)),
            pl.BlockSpec((None, 1, D), lambda i, k: (_mrow(i + t0) * 6 + k_sh, 0, 0)),
        ]
        out_shape.append(jax.ShapeDtypeStruct((nt * TM, D), BF))
        out_specs.append(pl.BlockSpec((TM, D), lambda i, k: (i, 0)))
    kern = functools.partial(_rownorm_kernel, has_prev=prev is not None, has_scale=scale is not None,
                             final=final is not None, has_next=nxt is not None)
    out = pl.pallas_call(
        kern,
        grid=(nt, nk),
        in_specs=in_specs,
        out_specs=out_specs,
        out_shape=out_shape,
        scratch_shapes=[pltpu.VMEM((TM, D), F32)],
        compiler_params=_cparams(("arbitrary", "arbitrary")),
        name="rownorm",
    )(*args)
    return out if nxt is not None else out[0]


FF_TN = 512


def _ffn1_kernel(a_ref, wg_ref, wu_ref, o_ref, wgb_ref, wub_ref):
    @pl.when(pl.program_id(1) == 0)
    def _():
        wgb_ref[...] = wg_ref[...].astype(BF)
        wub_ref[...] = wu_ref[...].astype(BF)

    a = a_ref[...]
    yg = jnp.dot(a, wgb_ref[...], preferred_element_type=F32)
    yu = jnp.dot(a, wub_ref[...], preferred_element_type=F32)
    o_ref[...] = (_silu(yg) * yu).astype(BF)


def _ffn1(h, wg_full, wu_full, w_idx, nt):
    nlead = len(w_idx)
    wspec = pl.BlockSpec((None,) * nlead + (D, FF_TN), lambda j, i: tuple(w_idx) + (0, j))
    return pl.pallas_call(
        _ffn1_kernel,
        grid=(DFF // FF_TN, nt),
        in_specs=[pl.BlockSpec((TM, D), lambda j, i: (i, 0)), wspec, wspec],
        out_specs=pl.BlockSpec((TM, FF_TN), lambda j, i: (i, j)),
        out_shape=jax.ShapeDtypeStruct((nt * TM, DFF), BF),
        scratch_shapes=[pltpu.VMEM((D, FF_TN), BF), pltpu.VMEM((D, FF_TN), BF)],
        compiler_params=_cparams(("arbitrary", "arbitrary")),
        name="ffn_up",
    )(h, wg_full, wu_full)


def _router_kernel(x_ref, g_ref, sc_ref, sh_ref, wr_ref, br_ref, o_ref):
    h = _norm_mod(x_ref[...], g_ref[...], sc_ref[...], sh_ref[...])
    hh, hl = _split(h)
    wh, wl = _split(wr_ref[...])
    nt = (((1,), (1,)), ((), ()))
    lg = (lax.dot_general(wh, hh, nt, preferred_element_type=F32)
          + lax.dot_general(wh, hl, nt, preferred_element_type=F32)
          + lax.dot_general(wl, hh, nt, preferred_element_type=F32) + br_ref[...])
    idx = lax.broadcasted_iota(jnp.int32, lg.shape, 0)
    m1 = jnp.max(lg, axis=0, keepdims=True)
    i1 = jnp.min(jnp.where(lg == m1, idx, NE), axis=0, keepdims=True)
    l2 = jnp.where(idx == i1, NEG, lg)
    m2 = jnp.max(l2, axis=0, keepdims=True)
    i2 = jnp.min(jnp.where(l2 == m2, idx, NE), axis=0, keepdims=True)
    e2 = jnp.exp(m2 - m1)
    w1 = 1.0 / (1.0 + e2)
    w2 = e2 / (1.0 + e2)
    o_ref[...] = jnp.where(idx == i1, w1, 0.0) + jnp.where(idx == i2, w2, 0.0)


def _router(x_lat, g, modl, k_sc, k_sh, w_router, b_router):
    return pl.pallas_call(
        _router_kernel,
        grid=(T // TM,),
        in_specs=[
            pl.BlockSpec((TM, D), lambda i: (i, 0)),
            pl.BlockSpec((1, D), lambda i: (0, 0)),
            pl.BlockSpec((None, 1, D), lambda i: (_mrow(i + 1) * 6 + k_sc, 0, 0)),
            pl.BlockSpec((None, 1, D), lambda i: (_mrow(i + 1) * 6 + k_sh, 0, 0)),
            pl.BlockSpec((NE, D), lambda i: (0, 0)),
            pl.BlockSpec((NE, 1), lambda i: (0, 0)),
        ],
        out_specs=pl.BlockSpec((NE, TM), lambda i: (0, i)),
        out_shape=jax.ShapeDtypeStruct((NE, T), F32),
        compiler_params=_cparams(("arbitrary",)),
        name="router",
    )(x_lat, g.reshape(1, D), modl, modl, w_router.T, b_router.reshape(NE, 1))


def kernel(x, c, ctx, c_ctx, w_mod, b_mod, g_pre_mix, g_post_mix, g_pre_ffn, g_post_ffn,
           w_in, w_lr_f, b_lr_f, w_lr_b, b_lr_b, g_gla, att_sink, w_br_gla, w_br_att, w_out,
           w_ffn_gate, w_ffn_up, w_ffn_down, w_router, b_router, w_exp_gate, w_exp_up, w_exp_down):
    cvec = jnp.zeros((8, D), F32).at[0:B].set(c).at[B].set(c_ctx)
    mod = _modulation(cvec, w_mod, b_mod).reshape(DEPTH, 8 * 6, 1, D)
    xa = jnp.concatenate([ctx.reshape(TC, D), x.reshape(T, D)], axis=0)
    cos, sin_a, sin_b = _rope_tables()

    h = _prenorm(xa, g_pre_mix[0], mod[0], 1, 0)
    for l in range(DEPTH):
        last = l == DEPTH - 1
        t0 = 1 if last else 0
        modl = mod[l]
        wl = w_in[l]
        pa = _proj(h, w_in, (l,), C_LF, 512, BF, "proj_gla")
        wb = jnp.concatenate([wl[:, C_AQ:C_AK], wl[:, C_G1:C_END], wl[:, C_AK:C_G1]], axis=1).astype(BF)
        pb = _proj_att(h, wb, cos, sin_a, sin_b)
        wc = jnp.pad(wl[:, C_LF:C_AQ], ((0, 0), (0, 128 - 2 * GRANK)))
        pc = _proj(h, wc, (), 128, 128, F32, "proj_decay")

        wlr = jnp.zeros((2, 128, GH * GDK), F32)
        wlr = wlr.at[0, 0:GRANK].set(w_lr_f[l]).at[1, GRANK:2 * GRANK].set(w_lr_b[l])
        blr = jnp.stack([b_lr_f[l], b_lr_b[l]]).reshape(2, 1, GH * GDK)
        gla = _gla(pa, pc, wlr, blr, g_gla[l])
        att = _attention(pb, att_sink[l])
        mm = _merge(gla, att, pb, w_br_gla, w_br_att, l, t0)

        xa, h2 = _rownorm(mm, w_out, (l,), D, a_ctx=False, t0=t0,
                          final=(xa, True, modl, 2, g_post_mix[l]),
                          nxt=(g_pre_ffn[l], modl, 4, 3))
        if not last:
            u = _ffn1(h2, w_ffn_gate, w_ffn_up, (l // 2,), NT_ALL)
            xa, h = _rownorm(u, w_ffn_down, (l // 2,), DFF, a_ctx=True, t0=0,
                             final=(xa, True, modl, 5, g_post_ffn[l]),
                             nxt=(g_pre_mix[l + 1], mod[l + 1], 1, 0))
        else:
            comb = _router(xa, g_pre_ffn[l], modl, 4, 3, w_router[l // 2], b_router[l // 2]).T
            acc = None
            for e in range(NE):
                u = _ffn1(h2, w_exp_gate, w_exp_up, (l // 2, e), T // TM)
                fin = dict(final=(xa, False, modl, 5, g_post_ffn[l])) if e == NE - 1 else {}
                acc = _rownorm(u, w_exp_down, (l // 2, e), DFF, a_ctx=False, t0=1,
                               prev=acc, scale=comb[:, e:e + 1], **fin)
            xa = acc
    return xa.reshape(B, L, D)
```

```python
import functools

import jax
import jax.numpy as jnp
from jax import lax
from jax.experimental import pallas as pl
from jax.experimental.pallas import tpu as pltpu

BF = jnp.bfloat16
F32 = jnp.float32

D = 2048
B = 2
L = 4096
LC = 256
T = B * L
TC = B * LC
R = TC + T
DEPTH = 2
GRID_W = 64
EPS = 1e-6

GH = 4
GDK = 256
GDV = 512
GRANK = 16
GTAU = 16.0
GC = 64

HD = 64
HQ = 32
HKV = 4
WIN = 128
ROPE_BASE = 10000.0

DFF = 5632
NE = 8

TM = 512
NT_ALL = R // TM
TPB = L // TM
VMEM_LIMIT = 56 * 1024 * 1024

C_GQ, C_GK, C_GV, C_GR, C_LF, C_LB, C_AQ, C_AK, C_AV, C_G1, C_G2, C_END = (
    0, 1024, 2048, 4096, 6144, 6160, 6176, 8224, 8480, 8736, 10784, 12832)


def _cparams(sem):
    return pltpu.CompilerParams(dimension_semantics=sem, vmem_limit_bytes=VMEM_LIMIT)


def _mrow(gi):
    return jnp.where(gi == 0, 2, (gi - 1) // TPB)


def _sigmoid(z):
    return 1.0 / (1.0 + jnp.exp(-z))


def _silu(z):
    return z * _sigmoid(z)


def _split(a):
    hi = a.astype(BF)
    lo = (a - hi.astype(F32)).astype(BF)
    return hi, lo


def _mod_kernel(c_ref, w_ref, b_ref, o_ref):
    a = _silu(c_ref[...]).astype(BF)
    o_ref[...] = jnp.dot(a, w_ref[...].astype(BF), preferred_element_type=F32) + b_ref[...]


def _modulation(cvec, w_mod, b_mod):
    tn = 1024
    return pl.pallas_call(
        _mod_kernel,
        grid=(DEPTH, 6 * D // tn),
        in_specs=[
            pl.BlockSpec((8, D), lambda l, j: (0, 0)),
            pl.BlockSpec((None, D, tn), lambda l, j: (l, 0, j)),
            pl.BlockSpec((None, 1, tn), lambda l, j: (l, 0, j)),
        ],
        out_specs=pl.BlockSpec((None, 8, tn), lambda l, j: (l, 0, j)),
        out_shape=jax.ShapeDtypeStruct((DEPTH, 8, 6 * D), F32),
        compiler_params=_cparams(("arbitrary", "arbitrary")),
        name="modulation",
    )(cvec, w_mod, b_mod.reshape(DEPTH, 1, 6 * D))


def _norm_mod(x, g, sc, sh):
    ms = jnp.mean(x * x, axis=-1, keepdims=True)
    return (x * lax.rsqrt(ms + EPS) * g) * (1.0 + sc) + sh


def _prenorm_kernel(x_ref, g_ref, sc_ref, sh_ref, o_ref):
    o_ref[...] = _norm_mod(x_ref[...], g_ref[...], sc_ref[...], sh_ref[...]).astype(BF)


def _prenorm(xa, g, modl, k_sc, k_sh):
    return pl.pallas_call(
        _prenorm_kernel,
        grid=(NT_ALL,),
        in_specs=[
            pl.BlockSpec((TM, D), lambda i: (i, 0)),
            pl.BlockSpec((1, D), lambda i: (0, 0)),
            pl.BlockSpec((None, 1, D), lambda i: (_mrow(i) * 6 + k_sc, 0, 0)),
            pl.BlockSpec((None, 1, D), lambda i: (_mrow(i) * 6 + k_sh, 0, 0)),
        ],
        out_specs=pl.BlockSpec((TM, D), lambda i: (i, 0)),
        out_shape=jax.ShapeDtypeStruct((R, D), BF),
        compiler_params=_cparams(("arbitrary",)),
        name="prenorm",
    )(xa, g.reshape(1, D), modl, modl)


def _proj_kernel(a_ref, w_ref, o_ref, wbf_ref):
    @pl.when(pl.program_id(1) == 0)
    def _():
        wbf_ref[...] = w_ref[...].astype(BF)

    o_ref[...] = jnp.dot(a_ref[...], wbf_ref[...], preferred_element_type=F32).astype(o_ref.dtype)


def _proj(h, w, w_idx, n_cols, tn, out_dtype, name):
    nlead = len(w_idx)
    return pl.pallas_call(
        _proj_kernel,
        grid=(n_cols // tn, NT_ALL),
        in_specs=[
            pl.BlockSpec((TM, D), lambda j, i: (i, 0)),
            pl.BlockSpec((None,) * nlead + (D, tn), lambda j, i: tuple(w_idx) + (0, j)),
        ],
        out_specs=pl.BlockSpec((TM, tn), lambda j, i: (i, j)),
        out_shape=jax.ShapeDtypeStruct((R, n_cols), out_dtype),
        scratch_shapes=[pltpu.VMEM((D, tn), BF)],
        compiler_params=_cparams(("arbitrary", "arbitrary")),
        name=name,
    )(h, w)


PB_TN = 512
PB_COLS = 2048 + 4096 + 512
PB_QT = 2048 // PB_TN
PB_KVT = PB_COLS // PB_TN - 1


def _rope(x, cos, sin_a, sin_b):
    return x * cos + pltpu.roll(x, 112, axis=1) * sin_a + pltpu.roll(x, 16, axis=1) * sin_b


def _proj_att_kernel(a_ref, w_ref, cos_ref, sa_ref, sb_ref, o_ref, wbf_ref):
    j = pl.program_id(0)
    i = pl.program_id(1)

    @pl.when(i == 0)
    def _():
        wbf_ref[...] = w_ref[...].astype(BF)

    acc = jnp.dot(a_ref[...], wbf_ref[...], preferred_element_type=F32)
    latent = i > 0
    is_q = j < PB_QT
    is_kv = j == PB_KVT

    @pl.when(jnp.logical_and(is_q, latent))
    def _():
        for s in range(PB_TN // 128):
            x = acc[:, s * 128:(s + 1) * 128]
            y = _rope(x, cos_ref[...], sa_ref[...], sb_ref[...]) * (HD ** -0.5)
            o_ref[:, s * 128:(s + 1) * 128] = y.astype(BF)

    @pl.when(jnp.logical_and(is_q, jnp.logical_not(latent)))
    def _():
        o_ref[...] = (acc * (HD ** -0.5)).astype(BF)

    @pl.when(jnp.logical_and(is_kv, latent))
    def _():
        for s in range(HKV * HD // 128):
            x = acc[:, s * 128:(s + 1) * 128]
            y = _rope(x, cos_ref[...], sa_ref[...], sb_ref[...])
            o_ref[:, s * 128:(s + 1) * 128] = y.astype(BF)
        o_ref[:, HKV * HD:] = acc[:, HKV * HD:].astype(BF)

    @pl.when(jnp.logical_not(jnp.logical_or(is_q, jnp.logical_and(is_kv, latent))))
    def _():
        o_ref[...] = acc.astype(BF)


def _proj_att(h, wb, cos, sin_a, sin_b):
    def tab(j, i):
        return (jnp.where(i == 0, 0, (i - 1) % TPB), 0)

    return pl.pallas_call(
        _proj_att_kernel,
        grid=(PB_COLS // PB_TN, NT_ALL),
        in_specs=[
            pl.BlockSpec((TM, D), lambda j, i: (i, 0)),
            pl.BlockSpec((D, PB_TN), lambda j, i: (0, j)),
            pl.BlockSpec((TM, 128), tab),
            pl.BlockSpec((TM, 128), tab),
            pl.BlockSpec((TM, 128), tab),
        ],
        out_specs=pl.BlockSpec((TM, PB_TN), lambda j, i: (i, j)),
        out_shape=jax.ShapeDtypeStruct((R, PB_COLS), BF),
        scratch_shapes=[pltpu.VMEM((D, PB_TN), BF)],
        compiler_params=_cparams(("arbitrary", "arbitrary")),
        name="proj_att",
    )(h, wb, cos, sin_a, sin_b)


def _rope_tables():
    rows = L // GRID_W
    row = jnp.repeat(jnp.arange(rows), GRID_W)
    col = jnp.tile(jnp.arange(GRID_W), rows)
    half = HD // 2
    inv = ROPE_BASE ** (-jnp.arange(0, half, 2, dtype=F32) / half)

    def angles(p):
        a = p.astype(F32)[:, None] * inv[None, :]
        return jnp.concatenate([a, a], axis=-1)

    ang = jnp.concatenate([angles(row), angles(col)], axis=-1)
    ang = jnp.concatenate([ang, ang], axis=-1)
    cos, sin = jnp.cos(ang), jnp.sin(ang)
    first = (jnp.arange(128) % 32) < 16
    return cos, jnp.where(first, -sin, 0.0), jnp.where(first, 0.0, sin)


GG = 256
NG = 1 + L // GG


def _gla_chunk(q, k, v, lr, wlr, blr, s_ref, fwd):
    lh, ll = _split(lr)
    wh, wl = _split(wlr)
    z = (jnp.dot(lh, wh, preferred_element_type=F32) + jnp.dot(ll, wh, preferred_element_type=F32)
         + jnp.dot(lh, wl, preferred_element_type=F32) + blr)
    la = (jnp.minimum(z, 0.0) - jnp.log1p(jnp.exp(-jnp.abs(z)))) * (1.0 / GTAU)
    ri = lax.broadcasted_iota(jnp.int32, (GC, GC), 0)
    ci = lax.broadcasted_iota(jnp.int32, (GC, GC), 1)
    keep = (ri >= ci) if fwd else (ri <= ci)
    tri = jnp.where(keep, 1.0, 0.0).astype(BF)
    ah, al = _split(la)
    bcum = jnp.dot(tri, ah, preferred_element_type=F32) + jnp.dot(tri, al, preferred_element_type=F32)
    tot = jnp.sum(la, axis=0, keepdims=True)
    kf = k.astype(F32)
    q_in = (q.astype(F32) * jnp.exp(bcum) * (GDK ** -0.5)).astype(BF)
    k_in = (kf * jnp.exp(-bcum)).astype(BF)
    k_out = (kf * jnp.exp(tot - bcum)).astype(BF)
    a = lax.dot_general(q_in, k_in, (((1,), (1,)), ((), ())), preferred_element_type=F32)
    a = jnp.where(keep, a, 0.0).astype(BF)
    st = s_ref[...]
    o = jnp.dot(a, v, preferred_element_type=F32)
    o = o + lax.dot_general(q_in, st.astype(BF), (((1,), (1,)), ((), ())), preferred_element_type=F32)
    upd = lax.dot_general(v, k_out, (((0,), (0,)), ((), ())), preferred_element_type=F32)
    s_ref[...] = st * jnp.exp(tot) + upd
    return o


def _gla_kernel(q_ref, k_ref, v_ref, r_ref, lr_ref, wlr_ref, blr_ref, gg_ref, o_ref, s_ref, of_ref):
    d = pl.program_id(2)
    g = pl.program_id(3)

    @pl.when(g == 0)
    def _():
        s_ref[...] = jnp.zeros_like(s_ref)

    def chunk(c, fwd):
        rows = slice(c * GC, (c + 1) * GC)
        return _gla_chunk(q_ref[rows, :], k_ref[rows, :], v_ref[rows, :], lr_ref[rows, :],
                          wlr_ref[...], blr_ref[...], s_ref, fwd)

    @pl.when(d == 0)
    def _():
        for c in range(GG // GC):
            of_ref[g, c * GC:(c + 1) * GC, :] = chunk(c, True)

    @pl.when(d == 1)
    def _():
        slot = jnp.where(g == 0, 0, NG - g)
        for c in reversed(range(GG // GC)):
            rows = slice(c * GC, (c + 1) * GC)
            o = chunk(c, False) + of_ref[slot, rows, :]
            on = o * lax.rsqrt(jnp.mean(o * o, axis=-1, keepdims=True) + EPS) * gg_ref[...]
            o_ref[rows, :] = (on * _silu(r_ref[rows, :].astype(F32))).astype(BF)


def _gla(pa, pc, wlr, blr, g_gla):
    lat0 = TC // GG

    def rb(b, h, d, g):
        return jnp.where(g == 0, b, lat0 + b * (L // GG) + jnp.where(d == 0, g - 1, NG - 1 - g))

    def rb_out(b, h, d, g):
        return jnp.where(d == 0, b, rb(b, h, d, g))

    return pl.pallas_call(
        _gla_kernel,
        grid=(B, GH, 2, NG),
        in_specs=[
            pl.BlockSpec((GG, GDK), lambda b, h, d, g: (rb(b, h, d, g), C_GQ // GDK + h)),
            pl.BlockSpec((GG, GDK), lambda b, h, d, g: (rb(b, h, d, g), C_GK // GDK + h)),
            pl.BlockSpec((GG, GDV), lambda b, h, d, g: (rb(b, h, d, g), C_GV // GDV + h)),
            pl.BlockSpec((GG, GDV), lambda b, h, d, g: (rb(b, h, d, g), C_GR // GDV + h)),
            pl.BlockSpec((GG, 128), lambda b, h, d, g: (rb(b, h, d, g), 0)),
            pl.BlockSpec((None, 128, GDK), lambda b, h, d, g: (d, 0, h)),
            pl.BlockSpec((None, 1, GDK), lambda b, h, d, g: (d, 0, h)),
            pl.BlockSpec((1, GDV), lambda b, h, d, g: (0, 0)),
        ],
        out_specs=pl.BlockSpec((GG, GDV), lambda b, h, d, g: (rb_out(b, h, d, g), h)),
        out_shape=jax.ShapeDtypeStruct((R, GH * GDV), BF),
        scratch_shapes=[pltpu.VMEM((GDV, GDK), F32), pltpu.VMEM((NG, GG, GDV), F32)],
        compiler_params=_cparams(("arbitrary", "arbitrary", "arbitrary", "arbitrary")),
        name="gla",
    )(pa, pa, pa, pa, pc, wlr, blr, g_gla.reshape(1, GDV))


AB = 128
NB = L // AB
NCB = LC // AB
NEG = float("-inf")


def _attn_kernel(q_ref, kvp_ref, kvc_ref, kvn_ref, kvx_ref, sink_ref, o_ref):
    j = pl.program_id(1)
    latent = j >= NCB
    ri = lax.broadcasted_iota(jnp.int32, (AB, AB), 0)
    ci = lax.broadcasted_iota(jnp.int32, (AB, AB), 1)
    ok_p = jnp.logical_and(ci >= ri, j > NCB)
    ok_c = jnp.logical_and(ci >= 0, latent)
    ok_n = jnp.logical_and(ci <= ri, jnp.logical_and(latent, j < NCB + NB - 1))
    bias = jnp.concatenate(
        [jnp.where(ok_p, 0.0, NEG), jnp.where(ok_c, 0.0, NEG), jnp.where(ok_n, 0.0, NEG),
         jnp.zeros((AB, LC), F32)], axis=1)
    for hk in range(HKV):
        ks = slice(hk * HD, (hk + 1) * HD)
        vs = slice(HKV * HD + hk * HD, HKV * HD + (hk + 1) * HD)
        k_all = jnp.concatenate([kvp_ref[:, ks], kvc_ref[:, ks], kvn_ref[:, ks], kvx_ref[:, ks]], axis=0)
        v_all = jnp.concatenate([kvp_ref[:, vs], kvc_ref[:, vs], kvn_ref[:, vs], kvx_ref[:, vs]], axis=0)
        for gq in range(HQ // HKV):
            h = hk * (HQ // HKV) + gq
            q = q_ref[:, h * HD:(h + 1) * HD]
            s = lax.dot_general(q, k_all, (((1,), (1,)), ((), ())), preferred_element_type=F32) + bias
            snk = sink_ref[h]
            m = jnp.maximum(jnp.max(s, axis=-1, keepdims=True), snk)
            e = jnp.exp(s - m)
            den = jnp.sum(e, axis=-1, keepdims=True) + jnp.exp(snk - m)
            o = jnp.dot(e.astype(BF), v_all, preferred_element_type=F32) / den
            o_ref[:, h * HD:(h + 1) * HD] = o.astype(BF)


def _attention(pb, sink):
    lat0 = TC // AB

    def qrow(b, j):
        return jnp.where(j < NCB, b * NCB + j, lat0 + b * NB + j - NCB)

    def krow(off):
        def f(b, j):
            i = jnp.clip(j - NCB + off, 0, NB - 1)
            return (lat0 + b * NB + i, PB_KVT)
        return f

    return pl.pallas_call(
        _attn_kernel,
        grid=(B, NCB + NB),
        in_specs=[
            pl.BlockSpec((AB, HQ * HD), lambda b, j: (qrow(b, j), 0)),
            pl.BlockSpec((AB, PB_TN), krow(-1)),
            pl.BlockSpec((AB, PB_TN), krow(0)),
            pl.BlockSpec((AB, PB_TN), krow(1)),
            pl.BlockSpec((LC, PB_TN), lambda b, j: (b, PB_KVT)),
            pl.BlockSpec(memory_space=pltpu.SMEM),
        ],
        out_specs=pl.BlockSpec((AB, HQ * HD), lambda b, j: (qrow(b, j), 0)),
        out_shape=jax.ShapeDtypeStruct((R, HQ * HD), BF),
        compiler_params=_cparams(("arbitrary", "arbitrary")),
        name="attention",
    )(pb, pb, pb, pb, pb, sink)


MG_TN = 512


def _merge_kernel(gla_ref, att_ref, g1_ref, g2_ref, w1_ref, w2_ref, o_ref, w1b_ref, w2b_ref):
    @pl.when(pl.program_id(1) == 0)
    def _():
        w1b_ref[...] = w1_ref[...].astype(BF)
        w2b_ref[...] = w2_ref[...].astype(BF)

    y1 = jnp.dot(gla_ref[...], w1b_ref[...], preferred_element_type=F32)
    y2 = jnp.dot(att_ref[...], w2b_ref[...], preferred_element_type=F32)
    y = _sigmoid(g1_ref[...].astype(F32)) * y1 + _sigmoid(g2_ref[...].astype(F32)) * y2
    o_ref[...] = y.astype(BF)


def _merge(gla, att, pb, w1, w2, l, t0):
    nt = NT_ALL - t0
    g1c = 2048 // MG_TN
    g2c = 4096 // MG_TN
    return pl.pallas_call(
        _merge_kernel,
        grid=(D // MG_TN, nt),
        in_specs=[
            pl.BlockSpec((TM, GH * GDV), lambda j, i: (i + t0, 0)),
            pl.BlockSpec((TM, HQ * HD), lambda j, i: (i + t0, 0)),
            pl.BlockSpec((TM, MG_TN), lambda j, i: (i + t0, g1c + j)),
            pl.BlockSpec((TM, MG_TN), lambda j, i: (i + t0, g2c + j)),
            pl.BlockSpec((None, GH * GDV, MG_TN), lambda j, i: (l, 0, j)),
            pl.BlockSpec((None, HQ * HD, MG_TN), lambda j, i: (l, 0, j)),
        ],
        out_specs=pl.BlockSpec((TM, MG_TN), lambda j, i: (i, j)),
        out_shape=jax.ShapeDtypeStruct((nt * TM, D), BF),
        scratch_shapes=[pltpu.VMEM((GH * GDV, MG_TN), BF), pltpu.VMEM((HQ * HD, MG_TN), BF)],
        compiler_params=_cparams(("arbitrary", "arbitrary")),
        name="merge",
    )(gla, att, pb, pb, w1, w2)


RN_TK = 512


def _residual_norm(y, x, gate, g_post):
    ms = jnp.mean(y * y, axis=-1, keepdims=True)
    return x + gate * (y * lax.rsqrt(ms + EPS) * g_post)


def _rownorm_kernel(a_ref, w_ref, x_ref, gt_ref, gp_ref, gn_ref, sc_ref, sh_ref, o_ref, h_ref, acc_ref):
    k = pl.program_id(1)

    @pl.when(k == 0)
    def _():
        acc_ref[...] = jnp.zeros_like(acc_ref)

    acc_ref[...] += jnp.dot(a_ref[...], w_ref[...].astype(BF), preferred_element_type=F32)

    @pl.when(k == pl.num_programs(1) - 1)
    def _():
        xn = _residual_norm(acc_ref[...], x_ref[...], gt_ref[...], gp_ref[...])
        o_ref[...] = xn
        h_ref[...] = _norm_mod(xn, gn_ref[...], sc_ref[...], sh_ref[...]).astype(BF)


def _rownorm(a, w_full, l, kdim, *, a_ctx, t0, x, modl, k_gate, g_post, g_next, modn, k_sc, k_sh):
    nt = NT_ALL - t0
    a_off = t0 if a_ctx else 0

    def mspec(k_chunk):
        return pl.BlockSpec((None, 1, D), lambda i, k: (_mrow(i + t0) * 6 + k_chunk, 0, 0))

    vec = pl.BlockSpec((1, D), lambda i, k: (0, 0))
    return pl.pallas_call(
        _rownorm_kernel,
        grid=(nt, kdim // RN_TK),
        in_specs=[
            pl.BlockSpec((TM, RN_TK), lambda i, k: (i + a_off, k)),
            pl.BlockSpec((None, RN_TK, D), lambda i, k: (l, k, 0)),
            pl.BlockSpec((TM, D), lambda i, k: (i + t0, 0)),
            mspec(k_gate), vec, vec, mspec(k_sc), mspec(k_sh),
        ],
        out_specs=[pl.BlockSpec((TM, D), lambda i, k: (i, 0)), pl.BlockSpec((TM, D), lambda i, k: (i, 0))],
        out_shape=[jax.ShapeDtypeStruct((nt * TM, D), F32), jax.ShapeDtypeStruct((nt * TM, D), BF)],
        scratch_shapes=[pltpu.VMEM((TM, D), F32)],
        compiler_params=_cparams(("arbitrary", "arbitrary")),
        name="rownorm",
    )(a, w_full, x, modl, g_post.reshape(1, D), g_next.reshape(1, D), modn, modn)


FF_TN = 512


def _new_expert(te_ref, i):
    return jnp.logical_or(i == 0, te_ref[i] != te_ref[jnp.maximum(i - 1, 0)])


def _ffn_up_kernel(te_ref, act_ref, a_ref, wg_ref, wu_ref, o_ref, wgb_ref, wub_ref):
    i = pl.program_id(1)

    @pl.when(_new_expert(te_ref, i))
    def _():
        wgb_ref[...] = wg_ref[...].astype(BF)
        wub_ref[...] = wu_ref[...].astype(BF)

    @pl.when(act_ref[i] == 1)
    def _():
        a = a_ref[...]
        yg = jnp.dot(a, wgb_ref[...], preferred_element_type=F32)
        yu = jnp.dot(a, wub_ref[...], preferred_element_type=F32)
        o_ref[...] = (_silu(yg) * yu).astype(BF)

    @pl.when(act_ref[i] == 0)
    def _():
        o_ref[...] = jnp.zeros_like(o_ref)


def _ffn_up(a, wg, wu, lead, te, act):
    nt = a.shape[0] // TM
    wspec = pl.BlockSpec((None, None, D, FF_TN), lambda j, i, te, act: (lead, te[i], 0, j))
    return pl.pallas_call(
        _ffn_up_kernel,
        grid_spec=pltpu.PrefetchScalarGridSpec(
            num_scalar_prefetch=2,
            grid=(DFF // FF_TN, nt),
            in_specs=[pl.BlockSpec((TM, D), lambda j, i, te, act: (i, 0)), wspec, wspec],
            out_specs=pl.BlockSpec((TM, FF_TN), lambda j, i, te, act: (i, j)),
            scratch_shapes=[pltpu.VMEM((D, FF_TN), BF), pltpu.VMEM((D, FF_TN), BF)],
        ),
        out_shape=jax.ShapeDtypeStruct((nt * TM, DFF), BF),
        compiler_params=_cparams(("arbitrary", "arbitrary")),
        name="ffn_up",
    )(te, act, a, wg, wu)


DN_TN = 512


def _moe_down_kernel(te_ref, act_ref, a_ref, w_ref, s_ref, o_ref, wb_ref):
    i = pl.program_id(1)

    @pl.when(_new_expert(te_ref, i))
    def _():
        wb_ref[...] = w_ref[...].astype(BF)

    @pl.when(act_ref[i] == 1)
    def _():
        y = jnp.dot(a_ref[...], wb_ref[...], preferred_element_type=F32)
        o_ref[...] = (y * s_ref[...]).astype(BF)

    @pl.when(act_ref[i] == 0)
    def _():
        o_ref[...] = jnp.zeros_like(o_ref)


def _moe_down(u, wd, lead, wrow, te, act):
    nt = u.shape[0] // TM
    return pl.pallas_call(
        _moe_down_kernel,
        grid_spec=pltpu.PrefetchScalarGridSpec(
            num_scalar_prefetch=2,
            grid=(D // DN_TN, nt),
            in_specs=[
                pl.BlockSpec((TM, DFF), lambda j, i, te, act: (i, 0)),
                pl.BlockSpec((None, None, DFF, DN_TN), lambda j, i, te, act: (lead, te[i], 0, j)),
                pl.BlockSpec((TM, 1), lambda j, i, te, act: (i, 0)),
            ],
            out_specs=pl.BlockSpec((TM, DN_TN), lambda j, i, te, act: (i, j)),
            scratch_shapes=[pltpu.VMEM((DFF, DN_TN), BF)],
        ),
        out_shape=jax.ShapeDtypeStruct((nt * TM, D), BF),
        compiler_params=_cparams(("arbitrary", "arbitrary")),
        name="moe_down",
    )(te, act, u, wd, wrow)


def _router_kernel(x_ref, g_ref, sc_ref, sh_ref, wr_ref, br_ref, o_ref, pos_ref, sel_ref, carry_ref):
    @pl.when(pl.program_id(0) == 0)
    def _():
        carry_ref[...] = jnp.zeros_like(carry_ref)

    h = _norm_mod(x_ref[...], g_ref[...], sc_ref[...], sh_ref[...])
    hh, hl = _split(h)
    wh, wl = _split(wr_ref[...])
    nt = (((1,), (1,)), ((), ()))
    lg = (lax.dot_general(wh, hh, nt, preferred_element_type=F32)
          + lax.dot_general(wh, hl, nt, preferred_element_type=F32)
          + lax.dot_general(wl, hh, nt, preferred_element_type=F32) + br_ref[...])
    idx = lax.broadcasted_iota(jnp.int32, lg.shape, 0)
    m1 = jnp.max(lg, axis=0, keepdims=True)
    i1 = jnp.min(jnp.where(lg == m1, idx, NE), axis=0, keepdims=True)
    l2 = jnp.where(idx == i1, NEG, lg)
    m2 = jnp.max(l2, axis=0, keepdims=True)
    i2 = jnp.min(jnp.where(l2 == m2, idx, NE), axis=0, keepdims=True)
    e2 = jnp.exp(m2 - m1)
    w1 = 1.0 / (1.0 + e2)
    w2 = e2 / (1.0 + e2)
    o_ref[...] = jnp.where(idx == i1, w1, 0.0) + jnp.where(idx == i2, w2, 0.0)
    sel = jnp.where(idx == i1, 1.0, jnp.where(idx == i2, 1.0, 0.0))
    si = lax.broadcasted_iota(jnp.int32, (TM, TM), 0)
    ti = lax.broadcasted_iota(jnp.int32, (TM, TM), 1)
    before = jnp.where(si < ti, 1.0, 0.0).astype(BF)
    excl = jnp.dot(sel.astype(BF), before, preferred_element_type=F32)
    pos_ref[...] = (excl + carry_ref[...]).astype(jnp.int32)
    sel_ref[...] = sel.astype(jnp.int32)
    carry_ref[...] += jnp.sum(sel, axis=1, keepdims=True)


def _router(x_lat, g, modl, k_sc, k_sh, w_router, b_router):
    ospec = pl.BlockSpec((NE, TM), lambda i: (0, i))
    return pl.pallas_call(
        _router_kernel,
        grid=(T // TM,),
        in_specs=[
            pl.BlockSpec((TM, D), lambda i: (i, 0)),
            pl.BlockSpec((1, D), lambda i: (0, 0)),
            pl.BlockSpec((None, 1, D), lambda i: (_mrow(i + 1) * 6 + k_sc, 0, 0)),
            pl.BlockSpec((None, 1, D), lambda i: (_mrow(i + 1) * 6 + k_sh, 0, 0)),
            pl.BlockSpec((NE, D), lambda i: (0, 0)),
            pl.BlockSpec((NE, 1), lambda i: (0, 0)),
        ],
        out_specs=[ospec, ospec, ospec],
        out_shape=[jax.ShapeDtypeStruct((NE, T), F32), jax.ShapeDtypeStruct((NE, T), jnp.int32),
                   jax.ShapeDtypeStruct((NE, T), jnp.int32)],
        scratch_shapes=[pltpu.VMEM((NE, 1), F32)],
        compiler_params=_cparams(("arbitrary",)),
        name="router",
    )(x_lat, g.reshape(1, D), modl, modl, w_router.T, b_router.reshape(NE, 1))


NSB = T // TM
NP = 2 * T // TM + NE
NSTEP = NE * NSB * 2


def _route_tables(pos, sel):
    i32 = jnp.int32
    counts = pos[:, -1] + sel[:, -1]
    ntile = (counts + TM - 1) // TM
    tend = jnp.cumsum(ntile)
    seg = (tend - ntile) * TM
    dest = jnp.where(sel > 0, seg[:, None] + pos, -1)
    d_hi = jnp.max(dest, axis=0)
    d_lo = jnp.sum(dest, axis=0) + (NE - 2) - d_hi
    tiles = jnp.arange(NP, dtype=i32)
    total = tend[-1]
    te = jnp.sum((tiles[:, None] >= tend[None, :]).astype(i32), axis=1)
    te_last = jnp.sum(((total - 1) >= tend).astype(i32))
    act = (tiles < total).astype(i32)
    te = jnp.where(act > 0, te, te_last)
    pb = pos[:, ::TM]
    pe = jnp.concatenate([pb[:, 1:], counts[:, None]], axis=1)
    first_row = seg[:, None] + pb
    cnt = pe - pb
    t0 = first_row // TM
    nblk = jnp.where(cnt > 0, (first_row + cnt - 1) // TM - t0 + 1, 0)
    kk = jnp.arange(2, dtype=i32)
    valid = kk[None, None, :] < nblk[:, :, None]
    blk = jnp.minimum(t0[:, :, None] + kk[None, None, :], NP - 1)
    v_d = valid.reshape(NSTEP)
    b_d = blk.reshape(NSTEP)
    cm = lax.cummax(jnp.where(v_d, b_d, -1), axis=0)
    prev = jnp.concatenate([jnp.full((1,), -1, i32), cm[:-1]])
    disp = dict(blk=jnp.maximum(cm, 0), valid=v_d.astype(i32),
                first=jnp.logical_and(v_d, cm != prev).astype(i32))
    v_c = jnp.transpose(valid, (1, 0, 2)).reshape(NSTEP)
    b_c = jnp.transpose(blk, (1, 0, 2)).reshape(NSTEP)
    steps = jnp.arange(NSTEP, dtype=i32)
    src = jnp.maximum(lax.cummax(jnp.where(v_c, steps, -1), axis=0), 0)
    comb = dict(blk=b_c[src], valid=v_c.astype(i32))
    return dict(dest=dest, d_hi=d_hi.reshape(T, 1), d_lo=d_lo.reshape(T, 1), te=te, act=act,
                disp=disp, comb=comb)


def _dispatch_kernel(blk_ref, val_ref, first_ref, h_ref, dest_ref, cw_ref, xs_ref, wr_ref):
    e = pl.program_id(0)
    s = (e * NSB + pl.program_id(1)) * 2 + pl.program_id(2)

    @pl.when(first_ref[s] == 1)
    def _():
        xs_ref[...] = jnp.zeros_like(xs_ref)
        wr_ref[...] = jnp.zeros_like(wr_ref)

    @pl.when(val_ref[s] == 1)
    def _():
        drow = dest_ref[pl.ds(e, 1), :]
        crow = cw_ref[pl.ds(e, 1), :]
        r = lax.broadcasted_iota(jnp.int32, (TM, TM), 0) + blk_ref[s] * TM
        hit = drow == r
        onehot = jnp.where(hit, 1.0, 0.0).astype(BF)
        g = jnp.dot(onehot, h_ref[...], preferred_element_type=F32)
        xs_ref[...] = (xs_ref[...].astype(F32) + g).astype(BF)
        wr_ref[...] += jnp.sum(jnp.where(hit, crow, 0.0), axis=1, keepdims=True)


def _dispatch(h_lat, dest, comb_w, tb):
    def omap(e, sb, k, blk, val, first):
        return (blk[(e * NSB + sb) * 2 + k], 0)

    return pl.pallas_call(
        _dispatch_kernel,
        grid_spec=pltpu.PrefetchScalarGridSpec(
            num_scalar_prefetch=3,
            grid=(NE, NSB, 2),
            in_specs=[
                pl.BlockSpec((TM, D), lambda e, sb, k, *_: (sb, 0)),
                pl.BlockSpec((NE, TM), lambda e, sb, k, *_: (0, sb)),
                pl.BlockSpec((NE, TM), lambda e, sb, k, *_: (0, sb)),
            ],
            out_specs=[pl.BlockSpec((TM, D), omap), pl.BlockSpec((TM, 1), omap)],
        ),
        out_shape=[jax.ShapeDtypeStruct((NP * TM, D), BF), jax.ShapeDtypeStruct((NP * TM, 1), F32)],
        compiler_params=_cparams(("arbitrary", "arbitrary", "arbitrary")),
        name="moe_dispatch",
    )(tb["blk"], tb["valid"], tb["first"], h_lat, dest, comb_w)


def _combine_kernel(blk_ref, val_ref, y_ref, dhi_ref, dlo_ref, x_ref, gt_ref, gp_ref, o_ref, acc_ref):
    e = pl.program_id(1)
    k = pl.program_id(2)
    s = (pl.program_id(0) * NE + e) * 2 + k

    @pl.when(jnp.logical_and(e == 0, k == 0))
    def _():
        acc_ref[...] = jnp.zeros_like(acc_ref)

    @pl.when(val_ref[s] == 1)
    def _():
        c = lax.broadcasted_iota(jnp.int32, (TM, TM), 1) + blk_ref[s] * TM
        onehot = jnp.where(dhi_ref[...] == c, 1.0, jnp.where(dlo_ref[...] == c, 1.0, 0.0)).astype(BF)
        acc_ref[...] += jnp.dot(onehot, y_ref[...], preferred_element_type=F32)

    @pl.when(jnp.logical_and(e == NE - 1, k == 1))
    def _():
        o_ref[...] = _residual_norm(acc_ref[...], x_ref[...], gt_ref[...], gp_ref[...])


def _combine(y, d_hi, d_lo, x_lat, modl, k_gate, g_post, tb):
    return pl.pallas_call(
        _combine_kernel,
        grid_spec=pltpu.PrefetchScalarGridSpec(
            num_scalar_prefetch=2,
            grid=(NSB, NE, 2),
            in_specs=[
                pl.BlockSpec((TM, D), lambda t, e, k, blk, val: (blk[(t * NE + e) * 2 + k], 0)),
                pl.BlockSpec((TM, 1), lambda t, e, k, *_: (t, 0)),
                pl.BlockSpec((TM, 1), lambda t, e, k, *_: (t, 0)),
                pl.BlockSpec((TM, D), lambda t, e, k, *_: (t, 0)),
                pl.BlockSpec((None, 1, D), lambda t, e, k, *_: (_mrow(t + 1) * 6 + k_gate, 0, 0)),
                pl.BlockSpec((1, D), lambda t, e, k, *_: (0, 0)),
            ],
            out_specs=pl.BlockSpec((TM, D), lambda t, e, k, *_: (t, 0)),
            scratch_shapes=[pltpu.VMEM((TM, D), F32)],
        ),
        out_shape=jax.ShapeDtypeStruct((T, D), F32),
        compiler_params=_cparams(("arbitrary", "arbitrary", "arbitrary")),
        name="moe_combine",
    )(tb["blk"], tb["valid"], y, d_hi, d_lo, x_lat, modl, g_post.reshape(1, D))


def kernel(x, c, ctx, c_ctx, w_mod, b_mod, g_pre_mix, g_post_mix, g_pre_ffn, g_post_ffn,
           w_in, w_lr_f, b_lr_f, w_lr_b, b_lr_b, g_gla, att_sink, w_br_gla, w_br_att, w_out,
           w_ffn_gate, w_ffn_up, w_ffn_down, w_router, b_router, w_exp_gate, w_exp_up, w_exp_down):
    cvec = jnp.zeros((8, D), F32).at[0:B].set(c).at[B].set(c_ctx)
    mod = _modulation(cvec, w_mod, b_mod).reshape(DEPTH, 8 * 6, 1, D)
    xa = jnp.concatenate([ctx.reshape(TC, D), x.reshape(T, D)], axis=0)
    cos, sin_a, sin_b = _rope_tables()

    h = _prenorm(xa, g_pre_mix[0], mod[0], 1, 0)
    for l in range(DEPTH):
        last = l == DEPTH - 1
        t0 = 1 if last else 0
        modl = mod[l]
        wl = w_in[l]
        pa = _proj(h, w_in, (l,), C_LF, 512, BF, "proj_gla")
        wb = jnp.concatenate([wl[:, C_AQ:C_AK], wl[:, C_G1:C_END], wl[:, C_AK:C_G1]], axis=1).astype(BF)
        pb = _proj_att(h, wb, cos, sin_a, sin_b)
        wc = jnp.pad(wl[:, C_LF:C_AQ], ((0, 0), (0, 128 - 2 * GRANK)))
        pc = _proj(h, wc, (), 128, 128, F32, "proj_decay")

        wlr = jnp.zeros((2, 128, GH * GDK), F32)
        wlr = wlr.at[0, 0:GRANK].set(w_lr_f[l]).at[1, GRANK:2 * GRANK].set(w_lr_b[l])
        blr = jnp.stack([b_lr_f[l], b_lr_b[l]]).reshape(2, 1, GH * GDK)
        gla = _gla(pa, pc, wlr, blr, g_gla[l])
        att = _attention(pb, att_sink[l])
        mm = _merge(gla, att, pb, w_br_gla, w_br_att, l, t0)

        xa, h2 = _rownorm(mm, w_out, l, D, a_ctx=False, t0=t0, x=xa, modl=modl, k_gate=2,
                          g_post=g_post_mix[l], g_next=g_pre_ffn[l], modn=modl, k_sc=4, k_sh=3)
        if not last:
            ones = jnp.ones((NT_ALL,), jnp.int32)
            u = _ffn_up(h2, w_ffn_gate[:, None], w_ffn_up[:, None], l // 2, 0 * ones, ones)
            xa, h = _rownorm(u, w_ffn_down, l // 2, DFF, a_ctx=True, t0=0, x=xa, modl=modl, k_gate=5,
                             g_post=g_post_ffn[l], g_next=g_pre_mix[l + 1], modn=mod[l + 1], k_sc=1, k_sh=0)
        else:
            comb_w, pos, sel = _router(xa, g_pre_ffn[l], modl, 4, 3, w_router[l // 2], b_router[l // 2])
            rt = _route_tables(pos, sel)
            xs, wrow = _dispatch(h2, rt["dest"], comb_w, rt["disp"])
            u = _ffn_up(xs, w_exp_gate, w_exp_up, l // 2, rt["te"], rt["act"])
            y = _moe_down(u, w_exp_down, l // 2, wrow, rt["te"], rt["act"])
            xa = _combine(y, rt["d_hi"], rt["d_lo"], xa, modl, 5, g_post_ffn[l], rt["comb"])
    return xa.reshape(B, L, D)
```

```python
import functools

import jax
import jax.numpy as jnp
from jax import lax
from jax.experimental import pallas as pl
from jax.experimental.pallas import tpu as pltpu

BF = jnp.bfloat16
F32 = jnp.float32

D = 2048
B = 2
L = 4096
LC = 256
T = B * L
TC = B * LC
R = TC + T
DEPTH = 2
GRID_W = 64
EPS = 1e-6

GH = 4
GDK = 256
GDV = 512
GRANK = 16
GTAU = 16.0
GC = 64

HD = 64
HQ = 32
HKV = 4
WIN = 128
ROPE_BASE = 10000.0

DFF = 5632
NE = 8

TM = 512
NT_ALL = R // TM
TPB = L // TM
VMEM_LIMIT = 56 * 1024 * 1024

C_GQ, C_GK, C_GV, C_GR, C_LF, C_LB, C_AQ, C_AK, C_AV, C_G1, C_G2, C_END = (
    0, 1024, 2048, 4096, 6144, 6160, 6176, 8224, 8480, 8736, 10784, 12832)


def _cparams(sem):
    return pltpu.CompilerParams(dimension_semantics=sem, vmem_limit_bytes=VMEM_LIMIT)


def _mrow(gi):
    return jnp.where(gi == 0, 2, (gi - 1) // TPB)


def _sigmoid(z):
    return 1.0 / (1.0 + jnp.exp(-z))


def _silu(z):
    return z * _sigmoid(z)


def _split(a):
    hi = a.astype(BF)
    lo = (a - hi.astype(F32)).astype(BF)
    return hi, lo


def _mod_kernel(c_ref, w_ref, b_ref, o_ref):
    a = _silu(c_ref[...]).astype(BF)
    o_ref[...] = jnp.dot(a, w_ref[...].astype(BF), preferred_element_type=F32) + b_ref[...]


def _modulation(cvec, w_mod, b_mod):
    tn = 1024
    return pl.pallas_call(
        _mod_kernel,
        grid=(DEPTH, 6 * D // tn),
        in_specs=[
            pl.BlockSpec((8, D), lambda l, j: (0, 0)),
            pl.BlockSpec((None, D, tn), lambda l, j: (l, 0, j)),
            pl.BlockSpec((None, 1, tn), lambda l, j: (l, 0, j)),
        ],
        out_specs=pl.BlockSpec((None, 8, tn), lambda l, j: (l, 0, j)),
        out_shape=jax.ShapeDtypeStruct((DEPTH, 8, 6 * D), F32),
        compiler_params=_cparams(("arbitrary", "arbitrary")),
        name="modulation",
    )(cvec, w_mod, b_mod.reshape(DEPTH, 1, 6 * D))


def _norm_mod(x, g, sc, sh):
    ms = jnp.mean(x * x, axis=-1, keepdims=True)
    return (x * lax.rsqrt(ms + EPS) * g) * (1.0 + sc) + sh


def _prenorm_kernel(x_ref, g_ref, sc_ref, sh_ref, o_ref):
    o_ref[...] = _norm_mod(x_ref[...], g_ref[...], sc_ref[...], sh_ref[...]).astype(BF)


def _prenorm(xa, g, modl, k_sc, k_sh):
    return pl.pallas_call(
        _prenorm_kernel,
        grid=(NT_ALL,),
        in_specs=[
            pl.BlockSpec((TM, D), lambda i: (i, 0)),
            pl.BlockSpec((1, D), lambda i: (0, 0)),
            pl.BlockSpec((None, 1, D), lambda i: (_mrow(i) * 6 + k_sc, 0, 0)),
            pl.BlockSpec((None, 1, D), lambda i: (_mrow(i) * 6 + k_sh, 0, 0)),
        ],
        out_specs=pl.BlockSpec((TM, D), lambda i: (i, 0)),
        out_shape=jax.ShapeDtypeStruct((R, D), BF),
        compiler_params=_cparams(("arbitrary",)),
        name="prenorm",
    )(xa, g.reshape(1, D), modl, modl)


NT_DIMS = (((1,), (1,)), ((), ()))


def _proj_kernel(a_ref, wt_ref, o_ref, wbf_ref):
    @pl.when(pl.program_id(1) == 0)
    def _():
        wbf_ref[...] = wt_ref[...].astype(BF)

    acc = lax.dot_general(a_ref[...], wbf_ref[...], NT_DIMS, preferred_element_type=F32)
    o_ref[...] = acc.astype(o_ref.dtype)


def _proj(h, wt, l, row0, n_cols, tn, out_dtype, name):
    return pl.pallas_call(
        _proj_kernel,
        grid=(n_cols // tn, NT_ALL),
        in_specs=[
            pl.BlockSpec((TM, D), lambda j, i: (i, 0)),
            pl.BlockSpec((None, tn, D), lambda j, i: (l, row0 // tn + j, 0)),
        ],
        out_specs=pl.BlockSpec((TM, tn), lambda j, i: (i, j)),
        out_shape=jax.ShapeDtypeStruct((R, n_cols), out_dtype),
        scratch_shapes=[pltpu.VMEM((tn, D), BF)],
        compiler_params=_cparams(("arbitrary", "arbitrary")),
        name=name,
    )(h, wt)


PB_TN = 512
PB_COLS = 2048 + 4096 + 512
PB_QT = 2048 // PB_TN
PB_KVT = PB_COLS // PB_TN - 1
LOG2E = 1.4426950408889634
Q_SCALE = HD ** -0.5 * LOG2E


def _rope(x, cos, sin_a, sin_b):
    return x * cos + pltpu.roll(x, 112, axis=1) * sin_a + pltpu.roll(x, 16, axis=1) * sin_b


def _proj_att_kernel(a_ref, wt_ref, cos_ref, sa_ref, sb_ref, o_ref, wbf_ref):
    j = pl.program_id(0)
    i = pl.program_id(1)

    @pl.when(i == 0)
    def _():
        wbf_ref[...] = wt_ref[0].astype(BF)

    acc = lax.dot_general(a_ref[...], wbf_ref[...], NT_DIMS, preferred_element_type=F32)
    latent = i > 0
    is_q = j < PB_QT
    is_kv = j == PB_KVT

    @pl.when(jnp.logical_and(is_q, latent))
    def _():
        for s in range(PB_TN // 128):
            x = acc[:, s * 128:(s + 1) * 128]
            y = _rope(x, cos_ref[...], sa_ref[...], sb_ref[...]) * Q_SCALE
            o_ref[:, s * 128:(s + 1) * 128] = y.astype(BF)

    @pl.when(jnp.logical_and(is_q, jnp.logical_not(latent)))
    def _():
        o_ref[...] = (acc * Q_SCALE).astype(BF)

    @pl.when(jnp.logical_and(is_kv, latent))
    def _():
        for s in range(HKV * HD // 128):
            x = acc[:, s * 128:(s + 1) * 128]
            y = _rope(x, cos_ref[...], sa_ref[...], sb_ref[...])
            o_ref[:, s * 128:(s + 1) * 128] = y.astype(BF)
        o_ref[:, HKV * HD:] = acc[:, HKV * HD:].astype(BF)

    @pl.when(jnp.logical_not(jnp.logical_or(is_q, jnp.logical_and(is_kv, latent))))
    def _():
        o_ref[...] = acc.astype(BF)


def _proj_att(h, wt, l, cos, sin_a, sin_b):
    def tab(j, i):
        return (jnp.where(i == 0, 0, (i - 1) % TPB), 0)

    def wrow(j, i):
        u = 2 * GRANK
        off = jnp.where(j < PB_QT, C_AQ // u + j * (PB_TN // u),
                        jnp.where(j < PB_KVT, C_G1 // u + (j - PB_QT) * (PB_TN // u), C_AK // u))
        return (l, off * u, 0)

    return pl.pallas_call(
        _proj_att_kernel,
        grid=(PB_COLS // PB_TN, NT_ALL),
        in_specs=[
            pl.BlockSpec((TM, D), lambda j, i: (i, 0)),
            pl.BlockSpec((pl.Element(1), pl.Element(PB_TN), pl.Element(D)), wrow),
            pl.BlockSpec((TM, 128), tab),
            pl.BlockSpec((TM, 128), tab),
            pl.BlockSpec((TM, 128), tab),
        ],
        out_specs=pl.BlockSpec((TM, PB_TN), lambda j, i: (i, j)),
        out_shape=jax.ShapeDtypeStruct((R, PB_COLS), BF),
        scratch_shapes=[pltpu.VMEM((PB_TN, D), BF)],
        compiler_params=_cparams(("arbitrary", "arbitrary")),
        name="proj_att",
    )(h, wt, cos, sin_a, sin_b)


def _rope_tables():
    rows = L // GRID_W
    row = jnp.repeat(jnp.arange(rows), GRID_W)
    col = jnp.tile(jnp.arange(GRID_W), rows)
    half = HD // 2
    inv = ROPE_BASE ** (-jnp.arange(0, half, 2, dtype=F32) / half)

    def angles(p):
        a = p.astype(F32)[:, None] * inv[None, :]
        return jnp.concatenate([a, a], axis=-1)

    ang = jnp.concatenate([angles(row), angles(col)], axis=-1)
    ang = jnp.concatenate([ang, ang], axis=-1)
    cos, sin = jnp.cos(ang), jnp.sin(ang)
    first = (jnp.arange(128) % 32) < 16
    return cos, jnp.where(first, -sin, 0.0), jnp.where(first, 0.0, sin)


GG = 256
NG = 1 + L // GG


def _dot_split(m01, a):
    hi, lo = _split(a)
    return jnp.dot(m01, hi, preferred_element_type=F32) + jnp.dot(m01, lo, preferred_element_type=F32)


def _gla_prep_kernel(lr_ref, wlr_ref, blr_ref, q_ref, k_ref, *out_refs):
    lh, ll = _split(lr_ref[...])
    ri = lax.broadcasted_iota(jnp.int32, (TM, TM), 0)
    ci = lax.broadcasted_iota(jnp.int32, (TM, TM), 1)
    same = jnp.right_shift(ri, GC.bit_length() - 1) == jnp.right_shift(ci, GC.bit_length() - 1)
    blk = jnp.where(same, 1.0, 0.0).astype(BF)
    qf = q_ref[...].astype(F32) * (GDK ** -0.5)
    kf = k_ref[...].astype(F32)
    for d in range(2):
        wh, wl = _split(wlr_ref[d])
        z = (jnp.dot(lh, wh, preferred_element_type=F32) + jnp.dot(ll, wh, preferred_element_type=F32)
             + jnp.dot(lh, wl, preferred_element_type=F32) + blr_ref[d])
        la = (jnp.minimum(z, 0.0) - jnp.log1p(jnp.exp(-jnp.abs(z)))) * (1.0 / GTAU)
        keep = (ri >= ci) if d == 0 else (ri <= ci)
        tri = jnp.where(jnp.logical_and(same, keep), 1.0, 0.0).astype(BF)
        bcum = _dot_split(tri, la)
        tot = _dot_split(blk, la)
        qi_ref, ki_ref, ko_ref, dec_ref = out_refs[4 * d:4 * d + 4]
        qi_ref[...] = (qf * jnp.exp(bcum)).astype(BF)
        ki_ref[...] = (kf * jnp.exp(-bcum)).astype(BF)
        ko_ref[...] = (kf * jnp.exp(tot - bcum)).astype(BF)
        dec_ref[...] = jnp.exp(tot)


def _gla_prep(pa, lr, wlr, blr):
    hk = GH * GDK
    ospec = pl.BlockSpec((TM, GDK), lambda i, h: (i, h))
    return pl.pallas_call(
        _gla_prep_kernel,
        grid=(NT_ALL, GH),
        in_specs=[
            pl.BlockSpec((TM, 2 * GRANK), lambda i, h: (i, 0)),
            pl.BlockSpec((2, 2 * GRANK, GDK), lambda i, h: (0, 0, h)),
            pl.BlockSpec((2, 1, GDK), lambda i, h: (0, 0, h)),
            pl.BlockSpec((TM, GDK), lambda i, h: (i, C_GQ // GDK + h)),
            pl.BlockSpec((TM, GDK), lambda i, h: (i, C_GK // GDK + h)),
        ],
        out_specs=[ospec] * 8,
        out_shape=[jax.ShapeDtypeStruct((R, hk), BF)] * 3 + [jax.ShapeDtypeStruct((R, hk), F32)]
                  + [jax.ShapeDtypeStruct((R, hk), BF)] * 3 + [jax.ShapeDtypeStruct((R, hk), F32)],
        compiler_params=_cparams(("arbitrary", "arbitrary")),
        name="gla_prep",
    )(lr, wlr, blr, pa, pa)


HP = 2


def _gla_scan_kernel(qf_ref, kif_ref, kof_ref, df_ref, vf_ref, qb_ref, kib_ref, kob_ref, db_ref, vb_ref,
                     of_ref, ob_ref, s_ref):
    @pl.when(pl.program_id(2) == 0)
    def _():
        s_ref[...] = jnp.zeros_like(s_ref)

    ri = lax.broadcasted_iota(jnp.int32, (GC, GC), 0)
    ci = lax.broadcasted_iota(jnp.int32, (GC, GC), 1)
    tn = (((0,), (0,)), ((), ()))
    dirs = ((qf_ref, kif_ref, kof_ref, df_ref, vf_ref, of_ref, ri >= ci, range(GG // GC)),
            (qb_ref, kib_ref, kob_ref, db_ref, vb_ref, ob_ref, ri <= ci, reversed(range(GG // GC))))
    for d, (q_ref, ki_ref, ko_ref, dec_ref, v_ref, o_ref, keep, order) in enumerate(dirs):
        for c in order:
            rows = slice(c * GC, (c + 1) * GC)
            for hh in range(HP):
                kc = slice(hh * GDK, (hh + 1) * GDK)
                vc = slice(hh * GDV, (hh + 1) * GDV)
                q_in = q_ref[rows, kc]
                v = v_ref[rows, vc]
                a = lax.dot_general(q_in, ki_ref[rows, kc], NT_DIMS, preferred_element_type=F32)
                a = jnp.where(keep, a, 0.0).astype(BF)
                st = s_ref[d, hh]
                o = jnp.dot(a, v, preferred_element_type=F32)
                o = o + lax.dot_general(q_in, st.astype(BF), NT_DIMS, preferred_element_type=F32)
                o_ref[rows, vc] = o
                upd = lax.dot_general(v, ko_ref[rows, kc], tn, preferred_element_type=F32)
                s_ref[d, hh] = st * dec_ref[c * GC:c * GC + 1, kc] + upd


def _gla_scan(prep, pa):
    lat0 = TC // GG

    def spec(width, d, col0):
        def index(b, p, g):
            lat = (g - 1) if d == 0 else (NG - 1 - g)
            return (jnp.where(g == 0, b, lat0 + b * (L // GG) + lat), col0 + p)
        return pl.BlockSpec((GG, width), index)

    in_specs = []
    for d in range(2):
        in_specs += [spec(HP * GDK, d, 0)] * 4 + [spec(HP * GDV, d, C_GV // (HP * GDV))]
    out_specs = [spec(HP * GDV, d, 0) for d in range(2)]
    return pl.pallas_call(
        _gla_scan_kernel,
        grid=(B, GH // HP, NG),
        in_specs=in_specs,
        out_specs=out_specs,
        out_shape=[jax.ShapeDtypeStruct((R, GH * GDV), F32)] * 2,
        scratch_shapes=[pltpu.VMEM((2, HP, GDV, GDK), F32)],
        compiler_params=_cparams(("arbitrary", "arbitrary", "arbitrary")),
        name="gla_scan",
    )(*prep[0:4], pa, *prep[4:8], pa)


def _gla_readout_kernel(of_ref, ob_ref, r_ref, gg_ref, o_ref):
    for h in range(GH):
        cols = slice(h * GDV, (h + 1) * GDV)
        o = of_ref[:, cols] + ob_ref[:, cols]
        on = o * lax.rsqrt(jnp.mean(o * o, axis=-1, keepdims=True) + EPS) * gg_ref[...]
        o_ref[:, cols] = (on * _silu(r_ref[:, cols].astype(F32))).astype(BF)


def _gla_readout(o_f, o_b, pa, g_gla):
    spec = pl.BlockSpec((TM, GH * GDV), lambda i: (i, 0))
    return pl.pallas_call(
        _gla_readout_kernel,
        grid=(NT_ALL,),
        in_specs=[spec, spec, pl.BlockSpec((TM, GH * GDV), lambda i: (i, C_GR // (GH * GDV))),
                  pl.BlockSpec((1, GDV), lambda i: (0, 0))],
        out_specs=spec,
        out_shape=jax.ShapeDtypeStruct((R, GH * GDV), BF),
        compiler_params=_cparams(("arbitrary",)),
        name="gla_readout",
    )(o_f, o_b, pa, g_gla.reshape(1, GDV))


AB = 128
NB = L // AB
NCB = LC // AB
NEG = float("-inf")


def _attn_kernel(q_ref, kvp_ref, kvc_ref, kvn_ref, kvx_ref, sink_ref, o_ref):
    j = pl.program_id(1)
    latent = j >= NCB
    ri = lax.broadcasted_iota(jnp.int32, (AB, AB), 0)
    ci = lax.broadcasted_iota(jnp.int32, (AB, AB), 1)
    bias_p = jnp.where(jnp.logical_and(ci >= ri, j > NCB), 0.0, NEG)
    bias_c = jnp.where(latent, 0.0, NEG)
    bias_n = jnp.where(jnp.logical_and(ci <= ri, jnp.logical_and(latent, j < NCB + NB - 1)), 0.0, NEG)
    gsz = HQ // HKV
    for hk in range(HKV):
        ks = slice(hk * HD, (hk + 1) * HD)
        vs = slice(HKV * HD + hk * HD, HKV * HD + (hk + 1) * HD)
        k_all = jnp.concatenate([kvp_ref[:, ks], kvc_ref[:, ks], kvn_ref[:, ks], kvx_ref[:, ks]], axis=0)
        v_all = jnp.concatenate([kvp_ref[:, vs], kvc_ref[:, vs], kvn_ref[:, vs], kvx_ref[:, vs]], axis=0)
        qs = jnp.concatenate([q_ref[:, (hk * gsz + g) * HD:(hk * gsz + g + 1) * HD] for g in range(gsz)], axis=0)
        s_all = lax.dot_general(qs, k_all, NT_DIMS, preferred_element_type=F32)
        ps, dens = [], []
        for g in range(gsz):
            sg = s_all[g * AB:(g + 1) * AB]
            sg = jnp.concatenate([sg[:, 0:AB] + bias_p, sg[:, AB:2 * AB] + bias_c,
                                  sg[:, 2 * AB:3 * AB] + bias_n, sg[:, 3 * AB:]], axis=1)
            snk = sink_ref[hk * gsz + g] * LOG2E
            m = jnp.maximum(jnp.max(sg, axis=-1, keepdims=True), snk)
            e = jnp.exp2(sg - m)
            dens.append(jnp.sum(e, axis=-1, keepdims=True) + jnp.exp2(snk - m))
            ps.append(e.astype(BF))
        o_all = jnp.dot(jnp.concatenate(ps, axis=0), v_all, preferred_element_type=F32)
        for g in range(0, gsz, 2):
            o2 = jnp.concatenate([o_all[g * AB:(g + 1) * AB] / dens[g],
                                  o_all[(g + 1) * AB:(g + 2) * AB] / dens[g + 1]], axis=1)
            o_ref[:, (hk * gsz + g) * HD:(hk * gsz + g + 2) * HD] = o2.astype(BF)


def _attention(pb, sink):
    lat0 = TC // AB

    def qrow(b, j):
        return jnp.where(j < NCB, b * NCB + j, lat0 + b * NB + j - NCB)

    def krow(off):
        def f(b, j):
            i = jnp.clip(j - NCB + off, 0, NB - 1)
            return (lat0 + b * NB + i, PB_KVT)
        return f

    return pl.pallas_call(
        _attn_kernel,
        grid=(B, NCB + NB),
        in_specs=[
            pl.BlockSpec((AB, HQ * HD), lambda b, j: (qrow(b, j), 0)),
            pl.BlockSpec((AB, PB_TN), krow(-1)),
            pl.BlockSpec((AB, PB_TN), krow(0)),
            pl.BlockSpec((AB, PB_TN), krow(1)),
            pl.BlockSpec((LC, PB_TN), lambda b, j: (b, PB_KVT)),
            pl.BlockSpec(memory_space=pltpu.SMEM),
        ],
        out_specs=pl.BlockSpec((AB, HQ * HD), lambda b, j: (qrow(b, j), 0)),
        out_shape=jax.ShapeDtypeStruct((R, HQ * HD), BF),
        compiler_params=_cparams(("arbitrary", "arbitrary")),
        name="attention",
    )(pb, pb, pb, pb, pb, sink)


MG_TN = 512


def _merge_kernel(gla_ref, att_ref, g1_ref, g2_ref, w1_ref, w2_ref, o_ref, w1b_ref, w2b_ref):
    @pl.when(pl.program_id(1) == 0)
    def _():
        w1b_ref[...] = w1_ref[...].astype(BF)
        w2b_ref[...] = w2_ref[...].astype(BF)

    y1 = jnp.dot(gla_ref[...], w1b_ref[...], preferred_element_type=F32)
    y2 = jnp.dot(att_ref[...], w2b_ref[...], preferred_element_type=F32)
    y = _sigmoid(g1_ref[...].astype(F32)) * y1 + _sigmoid(g2_ref[...].astype(F32)) * y2
    o_ref[...] = y.astype(BF)


def _merge(gla, att, pb, w1, w2, l, t0):
    nt = NT_ALL - t0
    g1c = 2048 // MG_TN
    g2c = 4096 // MG_TN
    return pl.pallas_call(
        _merge_kernel,
        grid=(D // MG_TN, nt),
        in_specs=[
            pl.BlockSpec((TM, GH * GDV), lambda j, i: (i + t0, 0)),
            pl.BlockSpec((TM, HQ * HD), lambda j, i: (i + t0, 0)),
            pl.BlockSpec((TM, MG_TN), lambda j, i: (i + t0, g1c + j)),
            pl.BlockSpec((TM, MG_TN), lambda j, i: (i + t0, g2c + j)),
            pl.BlockSpec((None, GH * GDV, MG_TN), lambda j, i: (l, 0, j)),
            pl.BlockSpec((None, HQ * HD, MG_TN), lambda j, i: (l, 0, j)),
        ],
        out_specs=pl.BlockSpec((TM, MG_TN), lambda j, i: (i, j)),
        out_shape=jax.ShapeDtypeStruct((nt * TM, D), BF),
        scratch_shapes=[pltpu.VMEM((GH * GDV, MG_TN), BF), pltpu.VMEM((HQ * HD, MG_TN), BF)],
        compiler_params=_cparams(("arbitrary", "arbitrary")),
        name="merge",
    )(gla, att, pb, pb, w1, w2)


RN_TK = 512


def _residual_norm(y, x, gate, g_post):
    ms = jnp.mean(y * y, axis=-1, keepdims=True)
    return x + gate * (y * lax.rsqrt(ms + EPS) * g_post)


def _rownorm_kernel(a_ref, w_ref, x_ref, gt_ref, gp_ref, gn_ref, sc_ref, sh_ref, o_ref, h_ref, acc_ref):
    k = pl.program_id(1)

    @pl.when(k == 0)
    def _():
        acc_ref[...] = jnp.zeros_like(acc_ref)

    acc_ref[...] += jnp.dot(a_ref[...], w_ref[...].astype(BF), preferred_element_type=F32)

    @pl.when(k == pl.num_programs(1) - 1)
    def _():
        xn = _residual_norm(acc_ref[...], x_ref[...], gt_ref[...], gp_ref[...])
        o_ref[...] = xn
        h_ref[...] = _norm_mod(xn, gn_ref[...], sc_ref[...], sh_ref[...]).astype(BF)


def _rownorm(a, w_full, l, kdim, *, a_ctx, t0, x, modl, k_gate, g_post, g_next, modn, k_sc, k_sh):
    nt = NT_ALL - t0
    a_off = t0 if a_ctx else 0

    def mspec(k_chunk):
        return pl.BlockSpec((None, 1, D), lambda i, k: (_mrow(i + t0) * 6 + k_chunk, 0, 0))

    vec = pl.BlockSpec((1, D), lambda i, k: (0, 0))
    return pl.pallas_call(
        _rownorm_kernel,
        grid=(nt, kdim // RN_TK),
        in_specs=[
            pl.BlockSpec((TM, RN_TK), lambda i, k: (i + a_off, k)),
            pl.BlockSpec((None, RN_TK, D), lambda i, k: (l, k, 0)),
            pl.BlockSpec((TM, D), lambda i, k: (i + t0, 0)),
            mspec(k_gate), vec, vec, mspec(k_sc), mspec(k_sh),
        ],
        out_specs=[pl.BlockSpec((TM, D), lambda i, k: (i, 0)), pl.BlockSpec((TM, D), lambda i, k: (i, 0))],
        out_shape=[jax.ShapeDtypeStruct((nt * TM, D), F32), jax.ShapeDtypeStruct((nt * TM, D), BF)],
        scratch_shapes=[pltpu.VMEM((TM, D), F32)],
        compiler_params=_cparams(("arbitrary", "arbitrary")),
        name="rownorm",
    )(a, w_full, x, modl, g_post.reshape(1, D), g_next.reshape(1, D), modn, modn)


FF_TN = 512


def _new_expert(te_ref, i):
    return jnp.logical_or(i == 0, te_ref[i] != te_ref[jnp.maximum(i - 1, 0)])


def _ffn_up_kernel(te_ref, act_ref, a_ref, wg_ref, wu_ref, o_ref, wgb_ref, wub_ref):
    i = pl.program_id(1)

    @pl.when(_new_expert(te_ref, i))
    def _():
        wgb_ref[...] = wg_ref[...].astype(BF)
        wub_ref[...] = wu_ref[...].astype(BF)

    @pl.when(act_ref[i] == 1)
    def _():
        a = a_ref[...]
        yg = jnp.dot(a, wgb_ref[...], preferred_element_type=F32)
        yu = jnp.dot(a, wub_ref[...], preferred_element_type=F32)
        o_ref[...] = (_silu(yg) * yu).astype(BF)

    @pl.when(act_ref[i] == 0)
    def _():
        o_ref[...] = jnp.zeros_like(o_ref)


def _ffn_up(a, wg, wu, lead, te, act):
    nt = a.shape[0] // TM
    wspec = pl.BlockSpec((None, None, D, FF_TN), lambda j, i, te, act: (lead, te[i], 0, j))
    return pl.pallas_call(
        _ffn_up_kernel,
        grid_spec=pltpu.PrefetchScalarGridSpec(
            num_scalar_prefetch=2,
            grid=(DFF // FF_TN, nt),
            in_specs=[pl.BlockSpec((TM, D), lambda j, i, te, act: (i, 0)), wspec, wspec],
            out_specs=pl.BlockSpec((TM, FF_TN), lambda j, i, te, act: (i, j)),
            scratch_shapes=[pltpu.VMEM((D, FF_TN), BF), pltpu.VMEM((D, FF_TN), BF)],
        ),
        out_shape=jax.ShapeDtypeStruct((nt * TM, DFF), BF),
        compiler_params=_cparams(("arbitrary", "arbitrary")),
        name="ffn_up",
    )(te, act, a, wg, wu)


DN_TN = 512


def _moe_down_kernel(te_ref, act_ref, a_ref, w_ref, s_ref, o_ref, wb_ref):
    i = pl.program_id(1)

    @pl.when(_new_expert(te_ref, i))
    def _():
        wb_ref[...] = w_ref[...].astype(BF)

    @pl.when(act_ref[i] == 1)
    def _():
        y = jnp.dot(a_ref[...], wb_ref[...], preferred_element_type=F32)
        o_ref[...] = (y * s_ref[...]).astype(BF)

    @pl.when(act_ref[i] == 0)
    def _():
        o_ref[...] = jnp.zeros_like(o_ref)


def _moe_down(u, wd, lead, wrow, te, act):
    nt = u.shape[0] // TM
    return pl.pallas_call(
        _moe_down_kernel,
        grid_spec=pltpu.PrefetchScalarGridSpec(
            num_scalar_prefetch=2,
            grid=(D // DN_TN, nt),
            in_specs=[
                pl.BlockSpec((TM, DFF), lambda j, i, te, act: (i, 0)),
                pl.BlockSpec((None, None, DFF, DN_TN), lambda j, i, te, act: (lead, te[i], 0, j)),
                pl.BlockSpec((TM, 1), lambda j, i, te, act: (i, 0)),
            ],
            out_specs=pl.BlockSpec((TM, DN_TN), lambda j, i, te, act: (i, j)),
            scratch_shapes=[pltpu.VMEM((DFF, DN_TN), BF)],
        ),
        out_shape=jax.ShapeDtypeStruct((nt * TM, D), BF),
        compiler_params=_cparams(("arbitrary", "arbitrary")),
        name="moe_down",
    )(te, act, u, wd, wrow)


def _router_kernel(x_ref, g_ref, sc_ref, sh_ref, wr_ref, br_ref, o_ref, pos_ref, sel_ref, carry_ref):
    @pl.when(pl.program_id(0) == 0)
    def _():
        carry_ref[...] = jnp.zeros_like(carry_ref)

    h = _norm_mod(x_ref[...], g_ref[...], sc_ref[...], sh_ref[...])
    hh, hl = _split(h)
    wh, wl = _split(wr_ref[...])
    nt = (((1,), (1,)), ((), ()))
    lg = (lax.dot_general(wh, hh, nt, preferred_element_type=F32)
          + lax.dot_general(wh, hl, nt, preferred_element_type=F32)
          + lax.dot_general(wl, hh, nt, preferred_element_type=F32) + br_ref[...])
    idx = lax.broadcasted_iota(jnp.int32, lg.shape, 0)
    m1 = jnp.max(lg, axis=0, keepdims=True)
    i1 = jnp.min(jnp.where(lg == m1, idx, NE), axis=0, keepdims=True)
    l2 = jnp.where(idx == i1, NEG, lg)
    m2 = jnp.max(l2, axis=0, keepdims=True)
    i2 = jnp.min(jnp.where(l2 == m2, idx, NE), axis=0, keepdims=True)
    e2 = jnp.exp(m2 - m1)
    w1 = 1.0 / (1.0 + e2)
    w2 = e2 / (1.0 + e2)
    o_ref[...] = jnp.where(idx == i1, w1, 0.0) + jnp.where(idx == i2, w2, 0.0)
    sel = jnp.where(idx == i1, 1.0, jnp.where(idx == i2, 1.0, 0.0))
    si = lax.broadcasted_iota(jnp.int32, (TM, TM), 0)
    ti = lax.broadcasted_iota(jnp.int32, (TM, TM), 1)
    before = jnp.where(si < ti, 1.0, 0.0).astype(BF)
    excl = jnp.dot(sel.astype(BF), before, preferred_element_type=F32)
    pos_ref[...] = (excl + carry_ref[...]).astype(jnp.int32)
    sel_ref[...] = sel.astype(jnp.int32)
    carry_ref[...] += jnp.sum(sel, axis=1, keepdims=True)


def _router(x_lat, g, modl, k_sc, k_sh, w_router, b_router):
    ospec = pl.BlockSpec((NE, TM), lambda i: (0, i))
    return pl.pallas_call(
        _router_kernel,
        grid=(T // TM,),
        in_specs=[
            pl.BlockSpec((TM, D), lambda i: (i, 0)),
            pl.BlockSpec((1, D), lambda i: (0, 0)),
            pl.BlockSpec((None, 1, D), lambda i: (_mrow(i + 1) * 6 + k_sc, 0, 0)),
            pl.BlockSpec((None, 1, D), lambda i: (_mrow(i + 1) * 6 + k_sh, 0, 0)),
            pl.BlockSpec((NE, D), lambda i: (0, 0)),
            pl.BlockSpec((NE, 1), lambda i: (0, 0)),
        ],
        out_specs=[ospec, ospec, ospec],
        out_shape=[jax.ShapeDtypeStruct((NE, T), F32), jax.ShapeDtypeStruct((NE, T), jnp.int32),
                   jax.ShapeDtypeStruct((NE, T), jnp.int32)],
        scratch_shapes=[pltpu.VMEM((NE, 1), F32)],
        compiler_params=_cparams(("arbitrary",)),
        name="router",
    )(x_lat, g.reshape(1, D), modl, modl, w_router.T, b_router.reshape(NE, 1))


NSB = T // TM
NP = 2 * T // TM + NE
NSTEP = NE * NSB * 2


def _route_tables(pos, sel):
    i32 = jnp.int32
    counts = pos[:, -1] + sel[:, -1]
    ntile = (counts + TM - 1) // TM
    tend = jnp.cumsum(ntile)
    seg = (tend - ntile) * TM
    dest = jnp.where(sel > 0, seg[:, None] + pos, -1)
    d_hi = jnp.max(dest, axis=0)
    d_lo = jnp.sum(dest, axis=0) + (NE - 2) - d_hi
    tiles = jnp.arange(NP, dtype=i32)
    total = tend[-1]
    te = jnp.sum((tiles[:, None] >= tend[None, :]).astype(i32), axis=1)
    te_last = jnp.sum(((total - 1) >= tend).astype(i32))
    act = (tiles < total).astype(i32)
    te = jnp.where(act > 0, te, te_last)
    pb = pos[:, ::TM]
    pe = jnp.concatenate([pb[:, 1:], counts[:, None]], axis=1)
    first_row = seg[:, None] + pb
    cnt = pe - pb
    t0 = first_row // TM
    nblk = jnp.where(cnt > 0, (first_row + cnt - 1) // TM - t0 + 1, 0)
    kk = jnp.arange(2, dtype=i32)
    valid = kk[None, None, :] < nblk[:, :, None]
    blk = jnp.minimum(t0[:, :, None] + kk[None, None, :], NP - 1)
    v_d = valid.reshape(NSTEP)
    b_d = blk.reshape(NSTEP)
    cm = lax.cummax(jnp.where(v_d, b_d, -1), axis=0)
    prev = jnp.concatenate([jnp.full((1,), -1, i32), cm[:-1]])
    disp = dict(blk=jnp.maximum(cm, 0), valid=v_d.astype(i32),
                first=jnp.logical_and(v_d, cm != prev).astype(i32))
    v_c = jnp.transpose(valid, (1, 0, 2)).reshape(NSTEP)
    b_c = jnp.transpose(blk, (1, 0, 2)).reshape(NSTEP)
    steps = jnp.arange(NSTEP, dtype=i32)
    src = jnp.maximum(lax.cummax(jnp.where(v_c, steps, -1), axis=0), 0)
    comb = dict(blk=b_c[src], valid=v_c.astype(i32))
    return dict(dest=dest, d_hi=d_hi.reshape(T, 1), d_lo=d_lo.reshape(T, 1), te=te, act=act,
                disp=disp, comb=comb)


def _dispatch_kernel(blk_ref, val_ref, first_ref, h_ref, dest_ref, cw_ref, xs_ref, wr_ref):
    e = pl.program_id(0)
    s = (e * NSB + pl.program_id(1)) * 2 + pl.program_id(2)

    @pl.when(first_ref[s] == 1)
    def _():
        xs_ref[...] = jnp.zeros_like(xs_ref)
        wr_ref[...] = jnp.zeros_like(wr_ref)

    @pl.when(val_ref[s] == 1)
    def _():
        drow = dest_ref[pl.ds(e, 1), :]
        crow = cw_ref[pl.ds(e, 1), :]
        r = lax.broadcasted_iota(jnp.int32, (TM, TM), 0) + blk_ref[s] * TM
        hit = drow == r
        onehot = jnp.where(hit, 1.0, 0.0).astype(BF)
        g = jnp.dot(onehot, h_ref[...], preferred_element_type=F32)
        xs_ref[...] = (xs_ref[...].astype(F32) + g).astype(BF)
        wr_ref[...] += jnp.sum(jnp.where(hit, crow, 0.0), axis=1, keepdims=True)


def _dispatch(h_lat, dest, comb_w, tb):
    def omap(e, sb, k, blk, val, first):
        return (blk[(e * NSB + sb) * 2 + k], 0)

    return pl.pallas_call(
        _dispatch_kernel,
        grid_spec=pltpu.PrefetchScalarGridSpec(
            num_scalar_prefetch=3,
            grid=(NE, NSB, 2),
            in_specs=[
                pl.BlockSpec((TM, D), lambda e, sb, k, *_: (sb, 0)),
                pl.BlockSpec((NE, TM), lambda e, sb, k, *_: (0, sb)),
                pl.BlockSpec((NE, TM), lambda e, sb, k, *_: (0, sb)),
            ],
            out_specs=[pl.BlockSpec((TM, D), omap), pl.BlockSpec((TM, 1), omap)],
        ),
        out_shape=[jax.ShapeDtypeStruct((NP * TM, D), BF), jax.ShapeDtypeStruct((NP * TM, 1), F32)],
        compiler_params=_cparams(("arbitrary", "arbitrary", "arbitrary")),
        name="moe_dispatch",
    )(tb["blk"], tb["valid"], tb["first"], h_lat, dest, comb_w)


def _combine_kernel(blk_ref, val_ref, y_ref, dhi_ref, dlo_ref, x_ref, gt_ref, gp_ref, o_ref, acc_ref):
    e = pl.program_id(1)
    k = pl.program_id(2)
    s = (pl.program_id(0) * NE + e) * 2 + k

    @pl.when(jnp.logical_and(e == 0, k == 0))
    def _():
        acc_ref[...] = jnp.zeros_like(acc_ref)

    @pl.when(val_ref[s] == 1)
    def _():
        c = lax.broadcasted_iota(jnp.int32, (TM, TM), 1) + blk_ref[s] * TM
        onehot = jnp.where(dhi_ref[...] == c, 1.0, jnp.where(dlo_ref[...] == c, 1.0, 0.0)).astype(BF)
        acc_ref[...] += jnp.dot(onehot, y_ref[...], preferred_element_type=F32)

    @pl.when(jnp.logical_and(e == NE - 1, k == 1))
    def _():
        o_ref[...] = _residual_norm(acc_ref[...], x_ref[...], gt_ref[...], gp_ref[...])


def _combine(y, d_hi, d_lo, x_lat, modl, k_gate, g_post, tb):
    return pl.pallas_call(
        _combine_kernel,
        grid_spec=pltpu.PrefetchScalarGridSpec(
            num_scalar_prefetch=2,
            grid=(NSB, NE, 2),
            in_specs=[
                pl.BlockSpec((TM, D), lambda t, e, k, blk, val: (blk[(t * NE + e) * 2 + k], 0)),
                pl.BlockSpec((TM, 1), lambda t, e, k, *_: (t, 0)),
                pl.BlockSpec((TM, 1), lambda t, e, k, *_: (t, 0)),
                pl.BlockSpec((TM, D), lambda t, e, k, *_: (t, 0)),
                pl.BlockSpec((None, 1, D), lambda t, e, k, *_: (_mrow(t + 1) * 6 + k_gate, 0, 0)),
                pl.BlockSpec((1, D), lambda t, e, k, *_: (0, 0)),
            ],
            out_specs=pl.BlockSpec((TM, D), lambda t, e, k, *_: (t, 0)),
            scratch_shapes=[pltpu.VMEM((TM, D), F32)],
        ),
        out_shape=jax.ShapeDtypeStruct((T, D), F32),
        compiler_params=_cparams(("arbitrary", "arbitrary", "arbitrary")),
        name="moe_combine",
    )(tb["blk"], tb["valid"], y, d_hi, d_lo, x_lat, modl, g_post.reshape(1, D))


def kernel(x, c, ctx, c_ctx, w_mod, b_mod, g_pre_mix, g_post_mix, g_pre_ffn, g_post_ffn,
           w_in, w_lr_f, b_lr_f, w_lr_b, b_lr_b, g_gla, att_sink, w_br_gla, w_br_att, w_out,
           w_ffn_gate, w_ffn_up, w_ffn_down, w_router, b_router, w_exp_gate, w_exp_up, w_exp_down):
    cvec = jnp.zeros((8, D), F32).at[0:B].set(c).at[B].set(c_ctx)
    mod = _modulation(cvec, w_mod, b_mod).reshape(DEPTH, 8 * 6, 1, D)
    xa = jnp.concatenate([ctx.reshape(TC, D), x.reshape(T, D)], axis=0)
    cos, sin_a, sin_b = _rope_tables()
    wt = jnp.swapaxes(w_in, 1, 2)

    h = _prenorm(xa, g_pre_mix[0], mod[0], 1, 0)
    for l in range(DEPTH):
        last = l == DEPTH - 1
        t0 = 1 if last else 0
        modl = mod[l]
        pa = _proj(h, wt, l, 0, C_LF, 512, BF, "proj_gla")
        pb = _proj_att(h, wt, l, cos, sin_a, sin_b)
        lr = _proj(h, wt, l, C_LF, 2 * GRANK, 2 * GRANK, F32, "proj_decay")

        wlr = jnp.zeros((2, 2 * GRANK, GH * GDK), F32)
        wlr = wlr.at[0, 0:GRANK].set(w_lr_f[l]).at[1, GRANK:2 * GRANK].set(w_lr_b[l])
        blr = jnp.stack([b_lr_f[l], b_lr_b[l]]).reshape(2, 1, GH * GDK)
        o_f, o_b = _gla_scan(_gla_prep(pa, lr, wlr, blr), pa)
        gla = _gla_readout(o_f, o_b, pa, g_gla[l])
        att = _attention(pb, att_sink[l])
        mm = _merge(gla, att, pb, w_br_gla, w_br_att, l, t0)

        xa, h2 = _rownorm(mm, w_out, l, D, a_ctx=False, t0=t0, x=xa, modl=modl, k_gate=2,
                          g_post=g_post_mix[l], g_next=g_pre_ffn[l], modn=modl, k_sc=4, k_sh=3)
        if not last:
            ones = jnp.ones((NT_ALL,), jnp.int32)
            u = _ffn_up(h2, w_ffn_gate[:, None], w_ffn_up[:, None], l // 2, 0 * ones, ones)
            xa, h = _rownorm(u, w_ffn_down, l // 2, DFF, a_ctx=True, t0=0, x=xa, modl=modl, k_gate=5,
                             g_post=g_post_ffn[l], g_next=g_pre_mix[l + 1], modn=mod[l + 1], k_sc=1, k_sh=0)
        else:
            comb_w, pos, sel = _router(xa, g_pre_ffn[l], modl, 4, 3, w_router[l // 2], b_router[l // 2])
            rt = _route_tables(pos, sel)
            xs, wrow = _dispatch(h2, rt["dest"], comb_w, rt["disp"])
            u = _ffn_up(xs, w_exp_gate, w_exp_up, l // 2, rt["te"], rt["act"])
            y = _moe_down(u, w_exp_down, l // 2, wrow, rt["te"], rt["act"])
            xa = _combine(y, rt["d_hi"], rt["d_lo"], xa, modl, 5, g_post_ffn[l], rt["comb"])
    return xa.reshape(B, L, D)
```

```python
import functools

import jax
import jax.numpy as jnp
from jax import lax
from jax.experimental import pallas as pl
from jax.experimental.pallas import tpu as pltpu

BF = jnp.bfloat16
F32 = jnp.float32

D = 2048
B = 2
L = 4096
LC = 256
T = B * L
TC = B * LC
R = TC + T
DEPTH = 2
GRID_W = 64
EPS = 1e-6

GH = 4
GDK = 256
GDV = 512
GRANK = 16
GTAU = 16.0
GC = 64

HD = 64
HQ = 32
HKV = 4
WIN = 128
ROPE_BASE = 10000.0

DFF = 5632
NE = 8

TM = 512
NT_ALL = R // TM
TPB = L // TM
VMEM_LIMIT = 56 * 1024 * 1024

C_GQ, C_GK, C_GV, C_GR, C_LF, C_LB, C_AQ, C_AK, C_AV, C_G1, C_G2, C_END = (
    0, 1024, 2048, 4096, 6144, 6160, 6176, 8224, 8480, 8736, 10784, 12832)


def _cparams(sem):
    return pltpu.CompilerParams(dimension_semantics=sem, vmem_limit_bytes=VMEM_LIMIT)


def _mrow(gi):
    return jnp.where(gi == 0, 2, (gi - 1) // TPB)


def _sigmoid(z):
    return 1.0 / (1.0 + jnp.exp(-z))


def _silu(z):
    return z * _sigmoid(z)


def _split(a):
    hi = a.astype(BF)
    lo = (a - hi.astype(F32)).astype(BF)
    return hi, lo


def _mod_kernel(c_ref, w_ref, b_ref, o_ref):
    a = _silu(c_ref[...]).astype(BF)
    o_ref[...] = jnp.dot(a, w_ref[...].astype(BF), preferred_element_type=F32) + b_ref[...]


def _modulation(cvec, w_mod, b_mod):
    tn = 1024
    return pl.pallas_call(
        _mod_kernel,
        grid=(DEPTH, 6 * D // tn),
        in_specs=[
            pl.BlockSpec((8, D), lambda l, j: (0, 0)),
            pl.BlockSpec((None, D, tn), lambda l, j: (l, 0, j)),
            pl.BlockSpec((None, 1, tn), lambda l, j: (l, 0, j)),
        ],
        out_specs=pl.BlockSpec((None, 8, tn), lambda l, j: (l, 0, j)),
        out_shape=jax.ShapeDtypeStruct((DEPTH, 8, 6 * D), F32),
        compiler_params=_cparams(("arbitrary", "arbitrary")),
        name="modulation",
    )(cvec, w_mod, b_mod.reshape(DEPTH, 1, 6 * D))


def _norm_mod(x, g, sc, sh):
    ms = jnp.mean(x * x, axis=-1, keepdims=True)
    return (x * lax.rsqrt(ms + EPS) * g) * (1.0 + sc) + sh


def _prenorm_kernel(x_ref, g_ref, sc_ref, sh_ref, o_ref):
    o_ref[...] = _norm_mod(x_ref[...], g_ref[...], sc_ref[...], sh_ref[...]).astype(BF)


def _prenorm(xa, g, modl, k_sc, k_sh):
    return pl.pallas_call(
        _prenorm_kernel,
        grid=(NT_ALL,),
        in_specs=[
            pl.BlockSpec((TM, D), lambda i: (i, 0)),
            pl.BlockSpec((1, D), lambda i: (0, 0)),
            pl.BlockSpec((None, 1, D), lambda i: (_mrow(i) * 6 + k_sc, 0, 0)),
            pl.BlockSpec((None, 1, D), lambda i: (_mrow(i) * 6 + k_sh, 0, 0)),
        ],
        out_specs=pl.BlockSpec((TM, D), lambda i: (i, 0)),
        out_shape=jax.ShapeDtypeStruct((R, D), BF),
        compiler_params=_cparams(("arbitrary",)),
        name="prenorm",
    )(xa, g.reshape(1, D), modl, modl)


NT_DIMS = (((1,), (1,)), ((), ()))


def _proj_kernel(a_ref, wt_ref, o_ref, wbf_ref):
    @pl.when(pl.program_id(1) == 0)
    def _():
        wbf_ref[...] = wt_ref[...].astype(BF)

    acc = lax.dot_general(a_ref[...], wbf_ref[...], NT_DIMS, preferred_element_type=F32)
    o_ref[...] = acc.astype(o_ref.dtype)


def _proj(h, wt, l, row0, n_cols, tn, out_dtype, name):
    return pl.pallas_call(
        _proj_kernel,
        grid=(n_cols // tn, NT_ALL),
        in_specs=[
            pl.BlockSpec((TM, D), lambda j, i: (i, 0)),
            pl.BlockSpec((None, tn, D), lambda j, i: (l, row0 // tn + j, 0)),
        ],
        out_specs=pl.BlockSpec((TM, tn), lambda j, i: (i, j)),
        out_shape=jax.ShapeDtypeStruct((R, n_cols), out_dtype),
        scratch_shapes=[pltpu.VMEM((tn, D), BF)],
        compiler_params=_cparams(("arbitrary", "arbitrary")),
        name=name,
    )(h, wt)


PB_TN = 1024
PB_COLS = 2048 + 4096
PB_QT = 2048 // PB_TN
KV_COLS = 2 * HKV * HD
LOG2E = 1.4426950408889634
Q_SCALE = HD ** -0.5 * LOG2E


def _rope(x, cos, sin_a, sin_b):
    return x * cos + pltpu.roll(x, 112, axis=1) * sin_a + pltpu.roll(x, 16, axis=1) * sin_b


def _proj_att_kernel(a_ref, wt_ref, cos_ref, sa_ref, sb_ref, o_ref, wbf_ref, *, n_q, rope_cols):
    j = pl.program_id(0)
    i = pl.program_id(1)

    @pl.when(i == 0)
    def _():
        wbf_ref[...] = wt_ref[0].astype(BF)

    acc = lax.dot_general(a_ref[...], wbf_ref[...], NT_DIMS, preferred_element_type=F32)
    tn = acc.shape[1]
    latent = i > 0
    is_q = j < n_q

    def rotated(cols, scale):
        for s in range(cols // 128):
            y = _rope(acc[:, s * 128:(s + 1) * 128], cos_ref[...], sa_ref[...], sb_ref[...])
            o_ref[:, s * 128:(s + 1) * 128] = (y * scale).astype(BF)

    @pl.when(jnp.logical_and(is_q, latent))
    def _():
        rotated(tn, Q_SCALE)

    @pl.when(jnp.logical_and(is_q, jnp.logical_not(latent)))
    def _():
        o_ref[...] = (acc * Q_SCALE).astype(BF)

    plain = jnp.logical_not(is_q)
    if rope_cols:
        plain = jnp.logical_and(plain, jnp.logical_not(latent))

        @pl.when(jnp.logical_and(jnp.logical_not(is_q), latent))
        def _():
            rotated(rope_cols, 1.0)
            o_ref[:, rope_cols:] = acc[:, rope_cols:].astype(BF)

    @pl.when(plain)
    def _():
        o_ref[...] = acc.astype(BF)


def _proj_att_call(h, wt, l, tabs, *, tn, n_tiles, n_q, rope_cols, wrow_units, name):
    def tab(j, i):
        return (jnp.where(i == 0, 0, (i - 1) % TPB), 0)

    return pl.pallas_call(
        functools.partial(_proj_att_kernel, n_q=n_q, rope_cols=rope_cols),
        grid=(n_tiles, NT_ALL),
        in_specs=[
            pl.BlockSpec((TM, D), lambda j, i: (i, 0)),
            pl.BlockSpec((pl.Element(1), pl.Element(tn), pl.Element(D)),
                         lambda j, i: (l, wrow_units(j) * (2 * GRANK), 0)),
            pl.BlockSpec((TM, 128), tab),
            pl.BlockSpec((TM, 128), tab),
            pl.BlockSpec((TM, 128), tab),
        ],
        out_specs=pl.BlockSpec((TM, tn), lambda j, i: (i, j)),
        out_shape=jax.ShapeDtypeStruct((R, n_tiles * tn), BF),
        scratch_shapes=[pltpu.VMEM((tn, D), BF)],
        compiler_params=_cparams(("arbitrary", "arbitrary")),
        name=name,
    )(h, wt, *tabs)


def _proj_att(h, wt, l, tabs):
    u = 2 * GRANK
    pb = _proj_att_call(
        h, wt, l, tabs, tn=PB_TN, n_tiles=PB_COLS // PB_TN, n_q=PB_QT, rope_cols=0, name="proj_att",
        wrow_units=lambda j: jnp.where(j < PB_QT, C_AQ // u + j * (PB_TN // u),
                                       C_G1 // u + (j - PB_QT) * (PB_TN // u)))
    pkv = _proj_att_call(h, wt, l, tabs, tn=KV_COLS, n_tiles=1, n_q=0, rope_cols=HKV * HD, name="proj_kv",
                         wrow_units=lambda j: C_AK // u + j)
    return pb, pkv


def _rope_tables():
    rows = L // GRID_W
    row = jnp.repeat(jnp.arange(rows), GRID_W)
    col = jnp.tile(jnp.arange(GRID_W), rows)
    half = HD // 2
    inv = ROPE_BASE ** (-jnp.arange(0, half, 2, dtype=F32) / half)

    def angles(p):
        a = p.astype(F32)[:, None] * inv[None, :]
        return jnp.concatenate([a, a], axis=-1)

    ang = jnp.concatenate([angles(row), angles(col)], axis=-1)
    ang = jnp.concatenate([ang, ang], axis=-1)
    cos, sin = jnp.cos(ang), jnp.sin(ang)
    first = (jnp.arange(128) % 32) < 16
    return cos, jnp.where(first, -sin, 0.0), jnp.where(first, 0.0, sin)


GG = 256
NG = 1 + L // GG


def _dot_split(m01, a):
    hi, lo = _split(a)
    return jnp.dot(m01, hi, preferred_element_type=F32) + jnp.dot(m01, lo, preferred_element_type=F32)


def _gla_prep_kernel(lr_ref, wlr_ref, blr_ref, q_ref, k_ref, *out_refs):
    lh, ll = _split(lr_ref[...])
    ri = lax.broadcasted_iota(jnp.int32, (TM, TM), 0)
    ci = lax.broadcasted_iota(jnp.int32, (TM, TM), 1)
    same = jnp.right_shift(ri, GC.bit_length() - 1) == jnp.right_shift(ci, GC.bit_length() - 1)
    blk = jnp.where(same, 1.0, 0.0).astype(BF)
    qf = q_ref[...].astype(F32) * (GDK ** -0.5)
    kf = k_ref[...].astype(F32)
    for d in range(2):
        wh, wl = _split(wlr_ref[d])
        z = (jnp.dot(lh, wh, preferred_element_type=F32) + jnp.dot(ll, wh, preferred_element_type=F32)
             + jnp.dot(lh, wl, preferred_element_type=F32) + blr_ref[d])
        la = (jnp.minimum(z, 0.0) - jnp.log1p(jnp.exp(-jnp.abs(z)))) * (1.0 / GTAU)
        keep = (ri >= ci) if d == 0 else (ri <= ci)
        tri = jnp.where(jnp.logical_and(same, keep), 1.0, 0.0).astype(BF)
        bcum = _dot_split(tri, la)
        tot = _dot_split(blk, la)
        qi_ref, ki_ref, ko_ref, dec_ref = out_refs[4 * d:4 * d + 4]
        qi_ref[...] = (qf * jnp.exp(bcum)).astype(BF)
        ki_ref[...] = (kf * jnp.exp(-bcum)).astype(BF)
        ko_ref[...] = (kf * jnp.exp(tot - bcum)).astype(BF)
        dec_ref[...] = jnp.exp(tot)


def _gla_prep(pa, lr, wlr, blr):
    hk = GH * GDK
    ospec = pl.BlockSpec((TM, GDK), lambda i, h: (i, h))
    return pl.pallas_call(
        _gla_prep_kernel,
        grid=(NT_ALL, GH),
        in_specs=[
            pl.BlockSpec((TM, 2 * GRANK), lambda i, h: (i, 0)),
            pl.BlockSpec((2, 2 * GRANK, GDK), lambda i, h: (0, 0, h)),
            pl.BlockSpec((2, 1, GDK), lambda i, h: (0, 0, h)),
            pl.BlockSpec((TM, GDK), lambda i, h: (i, C_GQ // GDK + h)),
            pl.BlockSpec((TM, GDK), lambda i, h: (i, C_GK // GDK + h)),
        ],
        out_specs=[ospec] * 8,
        out_shape=[jax.ShapeDtypeStruct((R, hk), BF)] * 3 + [jax.ShapeDtypeStruct((R, hk), F32)]
                  + [jax.ShapeDtypeStruct((R, hk), BF)] * 3 + [jax.ShapeDtypeStruct((R, hk), F32)],
        compiler_params=_cparams(("arbitrary", "arbitrary")),
        name="gla_prep",
    )(lr, wlr, blr, pa, pa)


HP = 2


def _gla_scan_kernel(qf_ref, kif_ref, kof_ref, df_ref, vf_ref, qb_ref, kib_ref, kob_ref, db_ref, vb_ref,
                     of_ref, ob_ref, s_ref):
    @pl.when(pl.program_id(2) == 0)
    def _():
        s_ref[...] = jnp.zeros_like(s_ref)

    ri = lax.broadcasted_iota(jnp.int32, (GC, GC), 0)
    ci = lax.broadcasted_iota(jnp.int32, (GC, GC), 1)
    tn = (((0,), (0,)), ((), ()))
    dirs = ((qf_ref, kif_ref, kof_ref, df_ref, vf_ref, of_ref, ri >= ci, range(GG // GC)),
            (qb_ref, kib_ref, kob_ref, db_ref, vb_ref, ob_ref, ri <= ci, reversed(range(GG // GC))))
    for d, (q_ref, ki_ref, ko_ref, dec_ref, v_ref, o_ref, keep, order) in enumerate(dirs):
        for c in order:
            rows = slice(c * GC, (c + 1) * GC)
            for hh in range(HP):
                kc = slice(hh * GDK, (hh + 1) * GDK)
                vc = slice(hh * GDV, (hh + 1) * GDV)
                q_in = q_ref[rows, kc]
                v = v_ref[rows, vc]
                a = lax.dot_general(q_in, ki_ref[rows, kc], NT_DIMS, preferred_element_type=F32)
                a = jnp.where(keep, a, 0.0).astype(BF)
                st = s_ref[d, hh]
                o = jnp.dot(a, v, preferred_element_type=F32)
                o = o + lax.dot_general(q_in, st.astype(BF), NT_DIMS, preferred_element_type=F32)
                o_ref[rows, vc] = o
                upd = lax.dot_general(v, ko_ref[rows, kc], tn, preferred_element_type=F32)
                s_ref[d, hh] = st * dec_ref[c * GC:c * GC + 1, kc] + upd


def _gla_scan(prep, pa):
    lat0 = TC // GG

    def spec(width, d, col0):
        def index(b, p, g):
            lat = (g - 1) if d == 0 else (NG - 1 - g)
            return (jnp.where(g == 0, b, lat0 + b * (L // GG) + lat), col0 + p)
        return pl.BlockSpec((GG, width), index)

    in_specs = []
    for d in range(2):
        in_specs += [spec(HP * GDK, d, 0)] * 4 + [spec(HP * GDV, d, C_GV // (HP * GDV))]
    out_specs = [spec(HP * GDV, d, 0) for d in range(2)]
    return pl.pallas_call(
        _gla_scan_kernel,
        grid=(B, GH // HP, NG),
        in_specs=in_specs,
        out_specs=out_specs,
        out_shape=[jax.ShapeDtypeStruct((R, GH * GDV), F32)] * 2,
        scratch_shapes=[pltpu.VMEM((2, HP, GDV, GDK), F32)],
        compiler_params=_cparams(("arbitrary", "arbitrary", "arbitrary")),
        name="gla_scan",
    )(*prep[0:4], pa, *prep[4:8], pa)


def _gla_readout_kernel(of_ref, ob_ref, r_ref, gg_ref, o_ref):
    for h in range(GH):
        cols = slice(h * GDV, (h + 1) * GDV)
        o = of_ref[:, cols] + ob_ref[:, cols]
        on = o * lax.rsqrt(jnp.mean(o * o, axis=-1, keepdims=True) + EPS) * gg_ref[...]
        o_ref[:, cols] = (on * _silu(r_ref[:, cols].astype(F32))).astype(BF)


def _gla_readout(o_f, o_b, pa, g_gla):
    spec = pl.BlockSpec((TM, GH * GDV), lambda i: (i, 0))
    return pl.pallas_call(
        _gla_readout_kernel,
        grid=(NT_ALL,),
        in_specs=[spec, spec, pl.BlockSpec((TM, GH * GDV), lambda i: (i, C_GR // (GH * GDV))),
                  pl.BlockSpec((1, GDV), lambda i: (0, 0))],
        out_specs=spec,
        out_shape=jax.ShapeDtypeStruct((R, GH * GDV), BF),
        compiler_params=_cparams(("arbitrary",)),
        name="gla_readout",
    )(o_f, o_b, pa, g_gla.reshape(1, GDV))


AB = 128
NB = L // AB
NCB = LC // AB
NEG = float("-inf")


def _attn_kernel(q_ref, kvp_ref, kvc_ref, kvn_ref, kvx_ref, sink_ref, o_ref):
    j = pl.program_id(1)
    latent = j >= NCB
    ri = lax.broadcasted_iota(jnp.int32, (AB, AB), 0)
    ci = lax.broadcasted_iota(jnp.int32, (AB, AB), 1)
    bias_p = jnp.where(jnp.logical_and(ci >= ri, j > NCB), 0.0, NEG)
    bias_c = jnp.where(latent, 0.0, NEG)
    bias_n = jnp.where(jnp.logical_and(ci <= ri, jnp.logical_and(latent, j < NCB + NB - 1)), 0.0, NEG)
    gsz = HQ // HKV
    for hk in range(HKV):
        ks = slice(hk * HD, (hk + 1) * HD)
        vs = slice(HKV * HD + hk * HD, HKV * HD + (hk + 1) * HD)
        k_all = jnp.concatenate([kvp_ref[:, ks], kvc_ref[:, ks], kvn_ref[:, ks], kvx_ref[:, ks]], axis=0)
        v_all = jnp.concatenate([kvp_ref[:, vs], kvc_ref[:, vs], kvn_ref[:, vs], kvx_ref[:, vs]], axis=0)
        qs = jnp.concatenate([q_ref[:, (hk * gsz + g) * HD:(hk * gsz + g + 1) * HD] for g in range(gsz)], axis=0)
        s_all = lax.dot_general(qs, k_all, NT_DIMS, preferred_element_type=F32)
        ps, dens = [], []
        for g in range(gsz):
            sg = s_all[g * AB:(g + 1) * AB]
            sg = jnp.concatenate([sg[:, 0:AB] + bias_p, sg[:, AB:2 * AB] + bias_c,
                                  sg[:, 2 * AB:3 * AB] + bias_n, sg[:, 3 * AB:]], axis=1)
            snk = sink_ref[hk * gsz + g] * LOG2E
            m = jnp.maximum(jnp.max(sg, axis=-1, keepdims=True), snk)
            e = jnp.exp2(sg - m)
            dens.append(jnp.sum(e, axis=-1, keepdims=True) + jnp.exp2(snk - m))
            ps.append(e.astype(BF))
        o_all = jnp.dot(jnp.concatenate(ps, axis=0), v_all, preferred_element_type=F32)
        for g in range(0, gsz, 2):
            o2 = jnp.concatenate([o_all[g * AB:(g + 1) * AB] / dens[g],
                                  o_all[(g + 1) * AB:(g + 2) * AB] / dens[g + 1]], axis=1)
            o_ref[:, (hk * gsz + g) * HD:(hk * gsz + g + 2) * HD] = o2.astype(BF)


def _attention(pb, pkv, sink):
    lat0 = TC // AB

    def qrow(b, j):
        return jnp.where(j < NCB, b * NCB + j, lat0 + b * NB + j - NCB)

    def krow(off):
        def f(b, j):
            i = jnp.clip(j - NCB + off, 0, NB - 1)
            return (lat0 + b * NB + i, 0)
        return f

    return pl.pallas_call(
        _attn_kernel,
        grid=(B, NCB + NB),
        in_specs=[
            pl.BlockSpec((AB, HQ * HD), lambda b, j: (qrow(b, j), 0)),
            pl.BlockSpec((AB, KV_COLS), krow(-1)),
            pl.BlockSpec((AB, KV_COLS), krow(0)),
            pl.BlockSpec((AB, KV_COLS), krow(1)),
            pl.BlockSpec((LC, KV_COLS), lambda b, j: (b, 0)),
            pl.BlockSpec(memory_space=pltpu.SMEM),
        ],
        out_specs=pl.BlockSpec((AB, HQ * HD), lambda b, j: (qrow(b, j), 0)),
        out_shape=jax.ShapeDtypeStruct((R, HQ * HD), BF),
        compiler_params=_cparams(("arbitrary", "arbitrary")),
        name="attention",
    )(pb, pkv, pkv, pkv, pkv, sink)


MG_TN = 512


def _merge_kernel(gla_ref, att_ref, g1_ref, g2_ref, w1_ref, w2_ref, o_ref, w1b_ref, w2b_ref):
    @pl.when(pl.program_id(1) == 0)
    def _():
        w1b_ref[...] = w1_ref[...].astype(BF)
        w2b_ref[...] = w2_ref[...].astype(BF)

    y1 = jnp.dot(gla_ref[...], w1b_ref[...], preferred_element_type=F32)
    y2 = jnp.dot(att_ref[...], w2b_ref[...], preferred_element_type=F32)
    y = _sigmoid(g1_ref[...].astype(F32)) * y1 + _sigmoid(g2_ref[...].astype(F32)) * y2
    o_ref[...] = y.astype(BF)


def _merge(gla, att, pb, w1, w2, l, t0):
    nt = NT_ALL - t0
    g1c = 2048 // MG_TN
    g2c = 4096 // MG_TN
    return pl.pallas_call(
        _merge_kernel,
        grid=(D // MG_TN, nt),
        in_specs=[
            pl.BlockSpec((TM, GH * GDV), lambda j, i: (i + t0, 0)),
            pl.BlockSpec((TM, HQ * HD), lambda j, i: (i + t0, 0)),
            pl.BlockSpec((TM, MG_TN), lambda j, i: (i + t0, g1c + j)),
            pl.BlockSpec((TM, MG_TN), lambda j, i: (i + t0, g2c + j)),
            pl.BlockSpec((None, GH * GDV, MG_TN), lambda j, i: (l, 0, j)),
            pl.BlockSpec((None, HQ * HD, MG_TN), lambda j, i: (l, 0, j)),
        ],
        out_specs=pl.BlockSpec((TM, MG_TN), lambda j, i: (i, j)),
        out_shape=jax.ShapeDtypeStruct((nt * TM, D), BF),
        scratch_shapes=[pltpu.VMEM((GH * GDV, MG_TN), BF), pltpu.VMEM((HQ * HD, MG_TN), BF)],
        compiler_params=_cparams(("arbitrary", "arbitrary")),
        name="merge",
    )(gla, att, pb, pb, w1, w2)


def _residual_norm(y, x, gate, g_post):
    ms = jnp.mean(y * y, axis=-1, keepdims=True)
    return x + gate * (y * lax.rsqrt(ms + EPS) * g_post)


def _rownorm_kernel(a_ref, w_ref, x_ref, gt_ref, gp_ref, gn_ref, sc_ref, sh_ref, o_ref, h_ref, acc_ref):
    k = pl.program_id(1)

    @pl.when(k == 0)
    def _():
        acc_ref[...] = jnp.zeros_like(acc_ref)

    acc_ref[...] += jnp.dot(a_ref[...], w_ref[...].astype(BF), preferred_element_type=F32)

    @pl.when(k == pl.num_programs(1) - 1)
    def _():
        xn = _residual_norm(acc_ref[...], x_ref[...], gt_ref[...], gp_ref[...])
        o_ref[...] = xn
        h_ref[...] = _norm_mod(xn, gn_ref[...], sc_ref[...], sh_ref[...]).astype(BF)


DN_TK = DFF // 4


def _rownorm(a, w_full, l, kdim, tk, *, a_ctx, t0, x, modl, k_gate, g_post, g_next, modn, k_sc, k_sh):
    nt = NT_ALL - t0
    a_off = t0 if a_ctx else 0

    def mspec(k_chunk):
        return pl.BlockSpec((None, 1, D), lambda i, k: (_mrow(i + t0) * 6 + k_chunk, 0, 0))

    vec = pl.BlockSpec((1, D), lambda i, k: (0, 0))
    return pl.pallas_call(
        _rownorm_kernel,
        grid=(nt, kdim // tk),
        in_specs=[
            pl.BlockSpec((TM, tk), lambda i, k: (i + a_off, k)),
            pl.BlockSpec((None, tk, D), lambda i, k: (l, k, 0)),
            pl.BlockSpec((TM, D), lambda i, k: (i + t0, 0)),
            mspec(k_gate), vec, vec, mspec(k_sc), mspec(k_sh),
        ],
        out_specs=[pl.BlockSpec((TM, D), lambda i, k: (i, 0)), pl.BlockSpec((TM, D), lambda i, k: (i, 0))],
        out_shape=[jax.ShapeDtypeStruct((nt * TM, D), F32), jax.ShapeDtypeStruct((nt * TM, D), BF)],
        scratch_shapes=[pltpu.VMEM((TM, D), F32)],
        compiler_params=_cparams(("arbitrary", "arbitrary")),
        name="rownorm",
    )(a, w_full, x, modl, g_post.reshape(1, D), g_next.reshape(1, D), modn, modn)


WO_CH = 4


def _outproj_kernel(a_ref, w_ref, x_ref, gt_ref, gp_ref, gn_ref, sc_ref, sh_ref, o_ref, h_ref, wbf_ref):
    s = pl.program_id(0)

    @pl.when(s < WO_CH)
    def _():
        ck = D // WO_CH
        wbf_ref[pl.ds(pl.multiple_of(s * ck, ck), ck), :] = w_ref[...].astype(BF)

    @pl.when(s >= WO_CH - 1)
    def _():
        y = jnp.dot(a_ref[...], wbf_ref[...], preferred_element_type=F32)
        xn = _residual_norm(y, x_ref[...], gt_ref[...], gp_ref[...])
        o_ref[...] = xn
        h_ref[...] = _norm_mod(xn, gn_ref[...], sc_ref[...], sh_ref[...]).astype(BF)


def _outproj(a, w_full, l, *, t0, x, modl, k_gate, g_post, g_next, modn, k_sc, k_sh):
    nt = NT_ALL - t0
    ck = D // WO_CH

    def tile(s):
        return jnp.maximum(s - (WO_CH - 1), 0)

    def mspec(k_chunk):
        return pl.BlockSpec((None, 1, D), lambda s: (_mrow(tile(s) + t0) * 6 + k_chunk, 0, 0))

    vec = pl.BlockSpec((1, D), lambda s: (0, 0))
    row = pl.BlockSpec((TM, D), lambda s: (tile(s), 0))
    return pl.pallas_call(
        _outproj_kernel,
        grid=(nt + WO_CH - 1,),
        in_specs=[
            row,
            pl.BlockSpec((None, ck, D), lambda s: (l, jnp.minimum(s, WO_CH - 1), 0)),
            pl.BlockSpec((TM, D), lambda s: (tile(s) + t0, 0)),
            mspec(k_gate), vec, vec, mspec(k_sc), mspec(k_sh),
        ],
        out_specs=[row, row],
        out_shape=[jax.ShapeDtypeStruct((nt * TM, D), F32), jax.ShapeDtypeStruct((nt * TM, D), BF)],
        scratch_shapes=[pltpu.VMEM((D, D), BF)],
        compiler_params=_cparams(("arbitrary",)),
        name="outproj",
    )(a, w_full, x, modl, g_post.reshape(1, D), g_next.reshape(1, D), modn, modn)


FF_TN = 512


def _new_expert(te_ref, i):
    return jnp.logical_or(i == 0, te_ref[i] != te_ref[jnp.maximum(i - 1, 0)])


def _ffn_up_kernel(te_ref, act_ref, a_ref, wg_ref, wu_ref, o_ref, wgb_ref, wub_ref):
    i = pl.program_id(1)

    @pl.when(_new_expert(te_ref, i))
    def _():
        wgb_ref[...] = wg_ref[...].astype(BF)
        wub_ref[...] = wu_ref[...].astype(BF)

    @pl.when(act_ref[i] == 1)
    def _():
        a = a_ref[...]
        yg = jnp.dot(a, wgb_ref[...], preferred_element_type=F32)
        yu = jnp.dot(a, wub_ref[...], preferred_element_type=F32)
        o_ref[...] = (_silu(yg) * yu).astype(BF)

    @pl.when(act_ref[i] == 0)
    def _():
        o_ref[...] = jnp.zeros_like(o_ref)


def _ffn_up(a, wg, wu, lead, te, act):
    nt = a.shape[0] // TM
    wspec = pl.BlockSpec((None, None, D, FF_TN), lambda j, i, te, act: (lead, te[i], 0, j))
    return pl.pallas_call(
        _ffn_up_kernel,
        grid_spec=pltpu.PrefetchScalarGridSpec(
            num_scalar_prefetch=2,
            grid=(DFF // FF_TN, nt),
            in_specs=[pl.BlockSpec((TM, D), lambda j, i, te, act: (i, 0)), wspec, wspec],
            out_specs=pl.BlockSpec((TM, FF_TN), lambda j, i, te, act: (i, j)),
            scratch_shapes=[pltpu.VMEM((D, FF_TN), BF), pltpu.VMEM((D, FF_TN), BF)],
        ),
        out_shape=jax.ShapeDtypeStruct((nt * TM, DFF), BF),
        compiler_params=_cparams(("arbitrary", "arbitrary")),
        name="ffn_up",
    )(te, act, a, wg, wu)


DN_TN = 512


def _moe_down_kernel(te_ref, act_ref, a_ref, w_ref, s_ref, o_ref, wb_ref):
    i = pl.program_id(1)

    @pl.when(_new_expert(te_ref, i))
    def _():
        wb_ref[...] = w_ref[...].astype(BF)

    @pl.when(act_ref[i] == 1)
    def _():
        y = jnp.dot(a_ref[...], wb_ref[...], preferred_element_type=F32)
        o_ref[...] = (y * s_ref[...]).astype(BF)

    @pl.when(act_ref[i] == 0)
    def _():
        o_ref[...] = jnp.zeros_like(o_ref)


def _moe_down(u, wd, lead, wrow, te, act):
    nt = u.shape[0] // TM
    return pl.pallas_call(
        _moe_down_kernel,
        grid_spec=pltpu.PrefetchScalarGridSpec(
            num_scalar_prefetch=2,
            grid=(D // DN_TN, nt),
            in_specs=[
                pl.BlockSpec((TM, DFF), lambda j, i, te, act: (i, 0)),
                pl.BlockSpec((None, None, DFF, DN_TN), lambda j, i, te, act: (lead, te[i], 0, j)),
                pl.BlockSpec((TM, 1), lambda j, i, te, act: (i, 0)),
            ],
            out_specs=pl.BlockSpec((TM, DN_TN), lambda j, i, te, act: (i, j)),
            scratch_shapes=[pltpu.VMEM((DFF, DN_TN), BF)],
        ),
        out_shape=jax.ShapeDtypeStruct((nt * TM, D), BF),
        compiler_params=_cparams(("arbitrary", "arbitrary")),
        name="moe_down",
    )(te, act, u, wd, wrow)


def _router_kernel(x_ref, g_ref, sc_ref, sh_ref, wr_ref, br_ref, o_ref, pos_ref, sel_ref, carry_ref):
    @pl.when(pl.program_id(0) == 0)
    def _():
        carry_ref[...] = jnp.zeros_like(carry_ref)

    h = _norm_mod(x_ref[...], g_ref[...], sc_ref[...], sh_ref[...])
    hh, hl = _split(h)
    wh, wl = _split(wr_ref[...])
    nt = (((1,), (1,)), ((), ()))
    lg = (lax.dot_general(wh, hh, nt, preferred_element_type=F32)
          + lax.dot_general(wh, hl, nt, preferred_element_type=F32)
          + lax.dot_general(wl, hh, nt, preferred_element_type=F32) + br_ref[...])
    idx = lax.broadcasted_iota(jnp.int32, lg.shape, 0)
    m1 = jnp.max(lg, axis=0, keepdims=True)
    i1 = jnp.min(jnp.where(lg == m1, idx, NE), axis=0, keepdims=True)
    l2 = jnp.where(idx == i1, NEG, lg)
    m2 = jnp.max(l2, axis=0, keepdims=True)
    i2 = jnp.min(jnp.where(l2 == m2, idx, NE), axis=0, keepdims=True)
    e2 = jnp.exp(m2 - m1)
    w1 = 1.0 / (1.0 + e2)
    w2 = e2 / (1.0 + e2)
    o_ref[...] = jnp.where(idx == i1, w1, 0.0) + jnp.where(idx == i2, w2, 0.0)
    sel = jnp.where(idx == i1, 1.0, jnp.where(idx == i2, 1.0, 0.0))
    si = lax.broadcasted_iota(jnp.int32, (TM, TM), 0)
    ti = lax.broadcasted_iota(jnp.int32, (TM, TM), 1)
    before = jnp.where(si < ti, 1.0, 0.0).astype(BF)
    excl = jnp.dot(sel.astype(BF), before, preferred_element_type=F32)
    pos_ref[...] = (excl + carry_ref[...]).astype(jnp.int32)
    sel_ref[...] = sel.astype(jnp.int32)
    carry_ref[...] += jnp.sum(sel, axis=1, keepdims=True)


def _router(x_lat, g, modl, k_sc, k_sh, w_router, b_router):
    ospec = pl.BlockSpec((NE, TM), lambda i: (0, i))
    return pl.pallas_call(
        _router_kernel,
        grid=(T // TM,),
        in_specs=[
            pl.BlockSpec((TM, D), lambda i: (i, 0)),
            pl.BlockSpec((1, D), lambda i: (0, 0)),
            pl.BlockSpec((None, 1, D), lambda i: (_mrow(i + 1) * 6 + k_sc, 0, 0)),
            pl.BlockSpec((None, 1, D), lambda i: (_mrow(i + 1) * 6 + k_sh, 0, 0)),
            pl.BlockSpec((NE, D), lambda i: (0, 0)),
            pl.BlockSpec((NE, 1), lambda i: (0, 0)),
        ],
        out_specs=[ospec, ospec, ospec],
        out_shape=[jax.ShapeDtypeStruct((NE, T), F32), jax.ShapeDtypeStruct((NE, T), jnp.int32),
                   jax.ShapeDtypeStruct((NE, T), jnp.int32)],
        scratch_shapes=[pltpu.VMEM((NE, 1), F32)],
        compiler_params=_cparams(("arbitrary",)),
        name="router",
    )(x_lat, g.reshape(1, D), modl, modl, w_router.T, b_router.reshape(NE, 1))


NSB = T // TM
NP = 2 * T // TM + NE
NSTEP = NE * NSB * 2


def _route_tables(pos, sel):
    i32 = jnp.int32
    counts = pos[:, -1] + sel[:, -1]
    ntile = (counts + TM - 1) // TM
    tend = jnp.cumsum(ntile)
    seg = (tend - ntile) * TM
    dest = jnp.where(sel > 0, seg[:, None] + pos, -1)
    d_hi = jnp.max(dest, axis=0)
    d_lo = jnp.sum(dest, axis=0) + (NE - 2) - d_hi
    tiles = jnp.arange(NP, dtype=i32)
    total = tend[-1]
    te = jnp.sum((tiles[:, None] >= tend[None, :]).astype(i32), axis=1)
    te_last = jnp.sum(((total - 1) >= tend).astype(i32))
    act = (tiles < total).astype(i32)
    te = jnp.where(act > 0, te, te_last)
    pb = pos[:, ::TM]
    pe = jnp.concatenate([pb[:, 1:], counts[:, None]], axis=1)
    first_row = seg[:, None] + pb
    cnt = pe - pb
    t0 = first_row // TM
    nblk = jnp.where(cnt > 0, (first_row + cnt - 1) // TM - t0 + 1, 0)
    kk = jnp.arange(2, dtype=i32)
    valid = kk[None, None, :] < nblk[:, :, None]
    blk = jnp.minimum(t0[:, :, None] + kk[None, None, :], NP - 1)
    v_d = valid.reshape(NSTEP)
    b_d = blk.reshape(NSTEP)
    cm = lax.cummax(jnp.where(v_d, b_d, -1), axis=0)
    prev = jnp.concatenate([jnp.full((1,), -1, i32), cm[:-1]])
    disp = dict(blk=jnp.maximum(cm, 0), valid=v_d.astype(i32),
                first=jnp.logical_and(v_d, cm != prev).astype(i32))
    v_c = jnp.transpose(valid, (1, 0, 2)).reshape(NSTEP)
    b_c = jnp.transpose(blk, (1, 0, 2)).reshape(NSTEP)
    steps = jnp.arange(NSTEP, dtype=i32)
    src = jnp.maximum(lax.cummax(jnp.where(v_c, steps, -1), axis=0), 0)
    comb = dict(blk=b_c[src], valid=v_c.astype(i32))
    return dict(dest=dest, d_hi=d_hi.reshape(T, 1), d_lo=d_lo.reshape(T, 1), te=te, act=act,
                disp=disp, comb=comb)


def _dispatch_kernel(blk_ref, val_ref, first_ref, h_ref, dest_ref, cw_ref, xs_ref, wr_ref):
    e = pl.program_id(0)
    s = (e * NSB + pl.program_id(1)) * 2 + pl.program_id(2)

    @pl.when(first_ref[s] == 1)
    def _():
        xs_ref[...] = jnp.zeros_like(xs_ref)
        wr_ref[...] = jnp.zeros_like(wr_ref)

    @pl.when(val_ref[s] == 1)
    def _():
        drow = dest_ref[pl.ds(e, 1), :]
        crow = cw_ref[pl.ds(e, 1), :]
        r = lax.broadcasted_iota(jnp.int32, (TM, TM), 0) + blk_ref[s] * TM
        hit = drow == r
        onehot = jnp.where(hit, 1.0, 0.0).astype(BF)
        g = jnp.dot(onehot, h_ref[...], preferred_element_type=F32)
        xs_ref[...] = (xs_ref[...].astype(F32) + g).astype(BF)
        wr_ref[...] += jnp.sum(jnp.where(hit, crow, 0.0), axis=1, keepdims=True)


def _dispatch(h_lat, dest, comb_w, tb):
    def omap(e, sb, k, blk, val, first):
        return (blk[(e * NSB + sb) * 2 + k], 0)

    return pl.pallas_call(
        _dispatch_kernel,
        grid_spec=pltpu.PrefetchScalarGridSpec(
            num_scalar_prefetch=3,
            grid=(NE, NSB, 2),
            in_specs=[
                pl.BlockSpec((TM, D), lambda e, sb, k, *_: (sb, 0)),
                pl.BlockSpec((NE, TM), lambda e, sb, k, *_: (0, sb)),
                pl.BlockSpec((NE, TM), lambda e, sb, k, *_: (0, sb)),
            ],
            out_specs=[pl.BlockSpec((TM, D), omap), pl.BlockSpec((TM, 1), omap)],
        ),
        out_shape=[jax.ShapeDtypeStruct((NP * TM, D), BF), jax.ShapeDtypeStruct((NP * TM, 1), F32)],
        compiler_params=_cparams(("arbitrary", "arbitrary", "arbitrary")),
        name="moe_dispatch",
    )(tb["blk"], tb["valid"], tb["first"], h_lat, dest, comb_w)


def _combine_kernel(blk_ref, val_ref, y_ref, dhi_ref, dlo_ref, x_ref, gt_ref, gp_ref, o_ref, acc_ref):
    e = pl.program_id(1)
    k = pl.program_id(2)
    s = (pl.program_id(0) * NE + e) * 2 + k

    @pl.when(jnp.logical_and(e == 0, k == 0))
    def _():
        acc_ref[...] = jnp.zeros_like(acc_ref)

    @pl.when(val_ref[s] == 1)
    def _():
        c = lax.broadcasted_iota(jnp.int32, (TM, TM), 1) + blk_ref[s] * TM
        onehot = jnp.where(dhi_ref[...] == c, 1.0, jnp.where(dlo_ref[...] == c, 1.0, 0.0)).astype(BF)
        acc_ref[...] += jnp.dot(onehot, y_ref[...], preferred_element_type=F32)

    @pl.when(jnp.logical_and(e == NE - 1, k == 1))
    def _():
        o_ref[...] = _residual_norm(acc_ref[...], x_ref[...], gt_ref[...], gp_ref[...])


def _combine(y, d_hi, d_lo, x_lat, modl, k_gate, g_post, tb):
    return pl.pallas_call(
        _combine_kernel,
        grid_spec=pltpu.PrefetchScalarGridSpec(
            num_scalar_prefetch=2,
            grid=(NSB, NE, 2),
            in_specs=[
                pl.BlockSpec((TM, D), lambda t, e, k, blk, val: (blk[(t * NE + e) * 2 + k], 0)),
                pl.BlockSpec((TM, 1), lambda t, e, k, *_: (t, 0)),
                pl.BlockSpec((TM, 1), lambda t, e, k, *_: (t, 0)),
                pl.BlockSpec((TM, D), lambda t, e, k, *_: (t, 0)),
                pl.BlockSpec((None, 1, D), lambda t, e, k, *_: (_mrow(t + 1) * 6 + k_gate, 0, 0)),
                pl.BlockSpec((1, D), lambda t, e, k, *_: (0, 0)),
            ],
            out_specs=pl.BlockSpec((TM, D), lambda t, e, k, *_: (t, 0)),
            scratch_shapes=[pltpu.VMEM((TM, D), F32)],
        ),
        out_shape=jax.ShapeDtypeStruct((T, D), F32),
        compiler_params=_cparams(("arbitrary", "arbitrary", "arbitrary")),
        name="moe_combine",
    )(tb["blk"], tb["valid"], y, d_hi, d_lo, x_lat, modl, g_post.reshape(1, D))


def kernel(x, c, ctx, c_ctx, w_mod, b_mod, g_pre_mix, g_post_mix, g_pre_ffn, g_post_ffn,
           w_in, w_lr_f, b_lr_f, w_lr_b, b_lr_b, g_gla, att_sink, w_br_gla, w_br_att, w_out,
           w_ffn_gate, w_ffn_up, w_ffn_down, w_router, b_router, w_exp_gate, w_exp_up, w_exp_down):
    cvec = jnp.zeros((8, D), F32).at[0:B].set(c).at[B].set(c_ctx)
    mod = _modulation(cvec, w_mod, b_mod).reshape(DEPTH, 8 * 6, 1, D)
    xa = jnp.concatenate([ctx.reshape(TC, D), x.reshape(T, D)], axis=0)
    cos, sin_a, sin_b = _rope_tables()
    wt = jnp.swapaxes(w_in, 1, 2)

    h = _prenorm(xa, g_pre_mix[0], mod[0], 1, 0)
    for l in range(DEPTH):
        last = l == DEPTH - 1
        t0 = 1 if last else 0
        modl = mod[l]
        pa = _proj(h, wt, l, 0, C_LF, 1024, BF, "proj_gla")
        pb, pkv = _proj_att(h, wt, l, (cos, sin_a, sin_b))
        lr = _proj(h, wt, l, C_LF, 2 * GRANK, 2 * GRANK, F32, "proj_decay")

        wlr = jnp.zeros((2, 2 * GRANK, GH * GDK), F32)
        wlr = wlr.at[0, 0:GRANK].set(w_lr_f[l]).at[1, GRANK:2 * GRANK].set(w_lr_b[l])
        blr = jnp.stack([b_lr_f[l], b_lr_b[l]]).reshape(2, 1, GH * GDK)
        o_f, o_b = _gla_scan(_gla_prep(pa, lr, wlr, blr), pa)
        gla = _gla_readout(o_f, o_b, pa, g_gla[l])
        att = _attention(pb, pkv, att_sink[l])
        mm = _merge(gla, att, pb, w_br_gla, w_br_att, l, t0)

        xa, h2 = _outproj(mm, w_out, l, t0=t0, x=xa, modl=modl, k_gate=2,
                          g_post=g_post_mix[l], g_next=g_pre_ffn[l], modn=modl, k_sc=4, k_sh=3)
        if not last:
            ones = jnp.ones((NT_ALL,), jnp.int32)
            u = _ffn_up(h2, w_ffn_gate[:, None], w_ffn_up[:, None], l // 2, 0 * ones, ones)
            xa, h = _rownorm(u, w_ffn_down, l // 2, DFF, DN_TK, a_ctx=True, t0=0, x=xa, modl=modl, k_gate=5,
                             g_post=g_post_ffn[l], g_next=g_pre_mix[l + 1], modn=mod[l + 1], k_sc=1, k_sh=0)
        else:
            comb_w, pos, sel = _router(xa, g_pre_ffn[l], modl, 4, 3, w_router[l // 2], b_router[l // 2])
            rt = _route_tables(pos, sel)
            xs, wrow = _dispatch(h2, rt["dest"], comb_w, rt["disp"])
            u = _ffn_up(xs, w_exp_gate, w_exp_up, l // 2, rt["te"], rt["act"])
            y = _moe_down(u, w_exp_down, l // 2, wrow, rt["te"], rt["act"])
            xa = _combine(y, rt["d_hi"], rt["d_lo"], xa, modl, 5, g_post_ffn[l], rt["comb"])
    return xa.reshape(B, L, D)
```

```python
import functools

import jax
import jax.numpy as jnp
from jax import lax
from jax.experimental import pallas as pl
from jax.experimental.pallas import tpu as pltpu

BF = jnp.bfloat16
F32 = jnp.float32

D = 2048
B = 2
L = 4096
LC = 256
T = B * L
TC = B * LC
R = TC + T
DEPTH = 2
GRID_W = 64
EPS = 1e-6

GH = 4
GDK = 256
GDV = 512
GRANK = 16
GTAU = 16.0
GC = 64

HD = 64
HQ = 32
HKV = 4
WIN = 128
ROPE_BASE = 10000.0

DFF = 5632
NE = 8

TM = 512
NT_ALL = R // TM
TPB = L // TM
VMEM_LIMIT = 56 * 1024 * 1024

C_GQ, C_GK, C_GV, C_GR, C_LF, C_LB, C_AQ, C_AK, C_AV, C_G1, C_G2, C_END = (
    0, 1024, 2048, 4096, 6144, 6160, 6176, 8224, 8480, 8736, 10784, 12832)


def _cparams(sem):
    return pltpu.CompilerParams(dimension_semantics=sem, vmem_limit_bytes=VMEM_LIMIT)


def _mrow(gi):
    return jnp.where(gi == 0, 2, (gi - 1) // TPB)


def _sigmoid(z):
    return 1.0 / (1.0 + jnp.exp(-z))


def _silu(z):
    return z * _sigmoid(z)


def _split(a):
    hi = a.astype(BF)
    lo = (a - hi.astype(F32)).astype(BF)
    return hi, lo


def _mod_kernel(c_ref, w_ref, b_ref, o_ref):
    a = _silu(c_ref[...]).astype(BF)
    o_ref[...] = jnp.dot(a, w_ref[...].astype(BF), preferred_element_type=F32) + b_ref[...]


def _modulation(cvec, w_mod, b_mod):
    tn = 1024
    return pl.pallas_call(
        _mod_kernel,
        grid=(DEPTH, 6 * D // tn),
        in_specs=[
            pl.BlockSpec((8, D), lambda l, j: (0, 0)),
            pl.BlockSpec((None, D, tn), lambda l, j: (l, 0, j)),
            pl.BlockSpec((None, 1, tn), lambda l, j: (l, 0, j)),
        ],
        out_specs=pl.BlockSpec((None, 8, tn), lambda l, j: (l, 0, j)),
        out_shape=jax.ShapeDtypeStruct((DEPTH, 8, 6 * D), F32),
        compiler_params=_cparams(("arbitrary", "arbitrary")),
        name="modulation",
    )(cvec, w_mod, b_mod.reshape(DEPTH, 1, 6 * D))


def _norm_mod(x, g, sc, sh):
    ms = jnp.mean(x * x, axis=-1, keepdims=True)
    return (x * lax.rsqrt(ms + EPS) * g) * (1.0 + sc) + sh


def _prenorm_kernel(x_ref, g_ref, sc_ref, sh_ref, o_ref):
    o_ref[...] = _norm_mod(x_ref[...], g_ref[...], sc_ref[...], sh_ref[...]).astype(BF)


def _prenorm(xa, g, modl, k_sc, k_sh):
    return pl.pallas_call(
        _prenorm_kernel,
        grid=(NT_ALL,),
        in_specs=[
            pl.BlockSpec((TM, D), lambda i: (i, 0)),
            pl.BlockSpec((1, D), lambda i: (0, 0)),
            pl.BlockSpec((None, 1, D), lambda i: (_mrow(i) * 6 + k_sc, 0, 0)),
            pl.BlockSpec((None, 1, D), lambda i: (_mrow(i) * 6 + k_sh, 0, 0)),
        ],
        out_specs=pl.BlockSpec((TM, D), lambda i: (i, 0)),
        out_shape=jax.ShapeDtypeStruct((R, D), BF),
        compiler_params=_cparams(("arbitrary",)),
        name="prenorm",
    )(xa, g.reshape(1, D), modl, modl)


NT_DIMS = (((1,), (1,)), ((), ()))


def _proj_kernel(a_ref, wt_ref, o_ref, wbf_ref):
    @pl.when(pl.program_id(1) == 0)
    def _():
        wbf_ref[...] = wt_ref[...].astype(BF)

    acc = lax.dot_general(a_ref[...], wbf_ref[...], NT_DIMS, preferred_element_type=F32)
    o_ref[...] = acc.astype(o_ref.dtype)


def _proj(h, wt, l, row0, n_cols, tn, out_dtype, name):
    return pl.pallas_call(
        _proj_kernel,
        grid=(n_cols // tn, NT_ALL),
        in_specs=[
            pl.BlockSpec((TM, D), lambda j, i: (i, 0)),
            pl.BlockSpec((None, tn, D), lambda j, i: (l, row0 // tn + j, 0)),
        ],
        out_specs=pl.BlockSpec((TM, tn), lambda j, i: (i, j)),
        out_shape=jax.ShapeDtypeStruct((R, n_cols), out_dtype),
        scratch_shapes=[pltpu.VMEM((tn, D), BF)],
        compiler_params=_cparams(("arbitrary", "arbitrary")),
        name=name,
    )(h, wt)


PB_TN = 1024
PB_COLS = 2048 + 4096
PB_QT = 2048 // PB_TN
KV_COLS = 2 * HKV * HD
LOG2E = 1.4426950408889634
Q_SCALE = HD ** -0.5 * LOG2E


def _rope(x, cos, sin_a, sin_b):
    return x * cos + pltpu.roll(x, 112, axis=1) * sin_a + pltpu.roll(x, 16, axis=1) * sin_b


def _proj_att_kernel(a_ref, wt_ref, cos_ref, sa_ref, sb_ref, o_ref, wbf_ref, *, n_q, rope_cols):
    j = pl.program_id(0)
    i = pl.program_id(1)

    @pl.when(i == 0)
    def _():
        wbf_ref[...] = wt_ref[0].astype(BF)

    acc = lax.dot_general(a_ref[...], wbf_ref[...], NT_DIMS, preferred_element_type=F32)
    tn = acc.shape[1]
    latent = i > 0
    is_q = j < n_q

    def rotated(cols, scale):
        for s in range(cols // 128):
            y = _rope(acc[:, s * 128:(s + 1) * 128], cos_ref[...], sa_ref[...], sb_ref[...])
            o_ref[:, s * 128:(s + 1) * 128] = (y * scale).astype(BF)

    @pl.when(jnp.logical_and(is_q, latent))
    def _():
        rotated(tn, Q_SCALE)

    @pl.when(jnp.logical_and(is_q, jnp.logical_not(latent)))
    def _():
        o_ref[...] = (acc * Q_SCALE).astype(BF)

    plain = jnp.logical_not(is_q)
    if rope_cols:
        plain = jnp.logical_and(plain, jnp.logical_not(latent))

        @pl.when(jnp.logical_and(jnp.logical_not(is_q), latent))
        def _():
            rotated(rope_cols, 1.0)
            o_ref[:, rope_cols:] = acc[:, rope_cols:].astype(BF)

    @pl.when(plain)
    def _():
        o_ref[...] = acc.astype(BF)


def _proj_att_call(h, wt, l, tabs, *, tn, n_tiles, n_q, rope_cols, wrow_units, name):
    def tab(j, i):
        return (jnp.where(i == 0, 0, (i - 1) % TPB), 0)

    return pl.pallas_call(
        functools.partial(_proj_att_kernel, n_q=n_q, rope_cols=rope_cols),
        grid=(n_tiles, NT_ALL),
        in_specs=[
            pl.BlockSpec((TM, D), lambda j, i: (i, 0)),
            pl.BlockSpec((pl.Element(1), pl.Element(tn), pl.Element(D)),
                         lambda j, i: (l, wrow_units(j) * (2 * GRANK), 0)),
            pl.BlockSpec((TM, 128), tab),
            pl.BlockSpec((TM, 128), tab),
            pl.BlockSpec((TM, 128), tab),
        ],
        out_specs=pl.BlockSpec((TM, tn), lambda j, i: (i, j)),
        out_shape=jax.ShapeDtypeStruct((R, n_tiles * tn), BF),
        scratch_shapes=[pltpu.VMEM((tn, D), BF)],
        compiler_params=_cparams(("arbitrary", "arbitrary")),
        name=name,
    )(h, wt, *tabs)


def _proj_att(h, wt, l, tabs):
    u = 2 * GRANK
    pb = _proj_att_call(
        h, wt, l, tabs, tn=PB_TN, n_tiles=PB_COLS // PB_TN, n_q=PB_QT, rope_cols=0, name="proj_att",
        wrow_units=lambda j: jnp.where(j < PB_QT, C_AQ // u + j * (PB_TN // u),
                                       C_G1 // u + (j - PB_QT) * (PB_TN // u)))
    pkv = _proj_att_call(h, wt, l, tabs, tn=KV_COLS, n_tiles=1, n_q=0, rope_cols=HKV * HD, name="proj_kv",
                         wrow_units=lambda j: C_AK // u + j)
    return pb, pkv


def _rope_tables():
    rows = L // GRID_W
    row = jnp.repeat(jnp.arange(rows), GRID_W)
    col = jnp.tile(jnp.arange(GRID_W), rows)
    half = HD // 2
    inv = ROPE_BASE ** (-jnp.arange(0, half, 2, dtype=F32) / half)

    def angles(p):
        a = p.astype(F32)[:, None] * inv[None, :]
        return jnp.concatenate([a, a], axis=-1)

    ang = jnp.concatenate([angles(row), angles(col)], axis=-1)
    ang = jnp.concatenate([ang, ang], axis=-1)
    cos, sin = jnp.cos(ang), jnp.sin(ang)
    first = (jnp.arange(128) % 32) < 16
    return cos, jnp.where(first, -sin, 0.0), jnp.where(first, 0.0, sin)


GG = 256
NG = 1 + L // GG


def _dot_split(m01, a):
    hi, lo = _split(a)
    return jnp.dot(m01, hi, preferred_element_type=F32) + jnp.dot(m01, lo, preferred_element_type=F32)


def _gla_prep_kernel(lr_ref, wlr_ref, blr_ref, q_ref, k_ref, *out_refs):
    lh, ll = _split(lr_ref[...])
    ri = lax.broadcasted_iota(jnp.int32, (TM, TM), 0)
    ci = lax.broadcasted_iota(jnp.int32, (TM, TM), 1)
    same = jnp.right_shift(ri, GC.bit_length() - 1) == jnp.right_shift(ci, GC.bit_length() - 1)
    blk = jnp.where(same, 1.0, 0.0).astype(BF)
    qf = q_ref[...].astype(F32) * (GDK ** -0.5)
    kf = k_ref[...].astype(F32)
    for d in range(2):
        wh, wl = _split(wlr_ref[d])
        z = (jnp.dot(lh, wh, preferred_element_type=F32) + jnp.dot(ll, wh, preferred_element_type=F32)
             + jnp.dot(lh, wl, preferred_element_type=F32) + blr_ref[d])
        la = (jnp.minimum(z, 0.0) - jnp.log1p(jnp.exp(-jnp.abs(z)))) * (1.0 / GTAU)
        keep = (ri >= ci) if d == 0 else (ri <= ci)
        tri = jnp.where(jnp.logical_and(same, keep), 1.0, 0.0).astype(BF)
        bcum = _dot_split(tri, la)
        tot = _dot_split(blk, la)
        qi_ref, ki_ref, ko_ref, dec_ref = out_refs[4 * d:4 * d + 4]
        qi_ref[...] = (qf * jnp.exp(bcum)).astype(BF)
        ki_ref[...] = (kf * jnp.exp(-bcum)).astype(BF)
        ko_ref[...] = (kf * jnp.exp(tot - bcum)).astype(BF)
        dec_ref[...] = jnp.exp(tot)


def _gla_prep(pa, lr, wlr, blr):
    hk = GH * GDK
    ospec = pl.BlockSpec((TM, GDK), lambda i, h: (i, h))
    return pl.pallas_call(
        _gla_prep_kernel,
        grid=(NT_ALL, GH),
        in_specs=[
            pl.BlockSpec((TM, 2 * GRANK), lambda i, h: (i, 0)),
            pl.BlockSpec((2, 2 * GRANK, GDK), lambda i, h: (0, 0, h)),
            pl.BlockSpec((2, 1, GDK), lambda i, h: (0, 0, h)),
            pl.BlockSpec((TM, GDK), lambda i, h: (i, C_GQ // GDK + h)),
            pl.BlockSpec((TM, GDK), lambda i, h: (i, C_GK // GDK + h)),
        ],
        out_specs=[ospec] * 8,
        out_shape=[jax.ShapeDtypeStruct((R, hk), BF)] * 3 + [jax.ShapeDtypeStruct((R, hk), F32)]
                  + [jax.ShapeDtypeStruct((R, hk), BF)] * 3 + [jax.ShapeDtypeStruct((R, hk), F32)],
        compiler_params=_cparams(("arbitrary", "arbitrary")),
        name="gla_prep",
    )(lr, wlr, blr, pa, pa)


HP = 2


def _gla_scan_kernel(qf_ref, kif_ref, kof_ref, df_ref, vf_ref, qb_ref, kib_ref, kob_ref, db_ref, vb_ref,
                     of_ref, ob_ref, s_ref):
    @pl.when(pl.program_id(2) == 0)
    def _():
        s_ref[...] = jnp.zeros_like(s_ref)

    ri = lax.broadcasted_iota(jnp.int32, (GC, GC), 0)
    ci = lax.broadcasted_iota(jnp.int32, (GC, GC), 1)
    tn = (((0,), (0,)), ((), ()))
    dirs = ((qf_ref, kif_ref, kof_ref, df_ref, vf_ref, of_ref, ri >= ci, range(GG // GC)),
            (qb_ref, kib_ref, kob_ref, db_ref, vb_ref, ob_ref, ri <= ci, reversed(range(GG // GC))))
    for d, (q_ref, ki_ref, ko_ref, dec_ref, v_ref, o_ref, keep, order) in enumerate(dirs):
        for c in order:
            rows = slice(c * GC, (c + 1) * GC)
            for hh in range(HP):
                kc = slice(hh * GDK, (hh + 1) * GDK)
                vc = slice(hh * GDV, (hh + 1) * GDV)
                q_in = q_ref[rows, kc]
                v = v_ref[rows, vc]
                a = lax.dot_general(q_in, ki_ref[rows, kc], NT_DIMS, preferred_element_type=F32)
                a = jnp.where(keep, a, 0.0).astype(BF)
                st = s_ref[d, hh]
                o = jnp.dot(a, v, preferred_element_type=F32)
                o = o + lax.dot_general(q_in, st.astype(BF), NT_DIMS, preferred_element_type=F32)
                o_ref[rows, vc] = o
                upd = lax.dot_general(v, ko_ref[rows, kc], tn, preferred_element_type=F32)
                s_ref[d, hh] = st * dec_ref[c * GC:c * GC + 1, kc] + upd


def _gla_scan(prep, pa):
    lat0 = TC // GG

    def spec(width, d, col0):
        def index(b, p, g):
            lat = (g - 1) if d == 0 else (NG - 1 - g)
            return (jnp.where(g == 0, b, lat0 + b * (L // GG) + lat), col0 + p)
        return pl.BlockSpec((GG, width), index)

    in_specs = []
    for d in range(2):
        in_specs += [spec(HP * GDK, d, 0)] * 4 + [spec(HP * GDV, d, C_GV // (HP * GDV))]
    out_specs = [spec(HP * GDV, d, 0) for d in range(2)]
    return pl.pallas_call(
        _gla_scan_kernel,
        grid=(B, GH // HP, NG),
        in_specs=in_specs,
        out_specs=out_specs,
        out_shape=[jax.ShapeDtypeStruct((R, GH * GDV), F32)] * 2,
        scratch_shapes=[pltpu.VMEM((2, HP, GDV, GDK), F32)],
        compiler_params=_cparams(("arbitrary", "arbitrary", "arbitrary")),
        name="gla_scan",
    )(*prep[0:4], pa, *prep[4:8], pa)


def _gla_readout_kernel(of_ref, ob_ref, r_ref, gg_ref, o_ref):
    for h in range(GH):
        cols = slice(h * GDV, (h + 1) * GDV)
        o = of_ref[:, cols] + ob_ref[:, cols]
        on = o * lax.rsqrt(jnp.mean(o * o, axis=-1, keepdims=True) + EPS) * gg_ref[...]
        o_ref[:, cols] = (on * _silu(r_ref[:, cols].astype(F32))).astype(BF)


def _gla_readout(o_f, o_b, pa, g_gla):
    spec = pl.BlockSpec((TM, GH * GDV), lambda i: (i, 0))
    return pl.pallas_call(
        _gla_readout_kernel,
        grid=(NT_ALL,),
        in_specs=[spec, spec, pl.BlockSpec((TM, GH * GDV), lambda i: (i, C_GR // (GH * GDV))),
                  pl.BlockSpec((1, GDV), lambda i: (0, 0))],
        out_specs=spec,
        out_shape=jax.ShapeDtypeStruct((R, GH * GDV), BF),
        compiler_params=_cparams(("arbitrary",)),
        name="gla_readout",
    )(o_f, o_b, pa, g_gla.reshape(1, GDV))


AB = 128
NB = L // AB
NCB = LC // AB
NEG = float("-inf")


def _attn_kernel(q_ref, kvp_ref, kvc_ref, kvn_ref, kvx_ref, sink_ref, o_ref):
    j = pl.program_id(1)
    latent = j >= NCB
    ri = lax.broadcasted_iota(jnp.int32, (AB, AB), 0)
    ci = lax.broadcasted_iota(jnp.int32, (AB, AB), 1)
    bias_p = jnp.where(jnp.logical_and(ci >= ri, j > NCB), 0.0, NEG)
    bias_c = jnp.where(latent, 0.0, NEG)
    bias_n = jnp.where(jnp.logical_and(ci <= ri, jnp.logical_and(latent, j < NCB + NB - 1)), 0.0, NEG)
    gsz = HQ // HKV
    for hk in range(HKV):
        ks = slice(hk * HD, (hk + 1) * HD)
        vs = slice(HKV * HD + hk * HD, HKV * HD + (hk + 1) * HD)
        k_all = jnp.concatenate([kvp_ref[:, ks], kvc_ref[:, ks], kvn_ref[:, ks], kvx_ref[:, ks]], axis=0)
        v_all = jnp.concatenate([kvp_ref[:, vs], kvc_ref[:, vs], kvn_ref[:, vs], kvx_ref[:, vs]], axis=0)
        qs = jnp.concatenate([q_ref[:, (hk * gsz + g) * HD:(hk * gsz + g + 1) * HD] for g in range(gsz)], axis=0)
        s_all = lax.dot_general(qs, k_all, NT_DIMS, preferred_element_type=F32)
        ps, dens = [], []
        for g in range(gsz):
            sg = s_all[g * AB:(g + 1) * AB]
            sg = jnp.concatenate([sg[:, 0:AB] + bias_p, sg[:, AB:2 * AB] + bias_c,
                                  sg[:, 2 * AB:3 * AB] + bias_n, sg[:, 3 * AB:]], axis=1)
            snk = sink_ref[hk * gsz + g] * LOG2E
            m = jnp.maximum(jnp.max(sg, axis=-1, keepdims=True), snk)
            e = jnp.exp2(sg - m)
            dens.append(jnp.sum(e, axis=-1, keepdims=True) + jnp.exp2(snk - m))
            ps.append(e.astype(BF))
        o_all = jnp.dot(jnp.concatenate(ps, axis=0), v_all, preferred_element_type=F32)
        for g in range(0, gsz, 2):
            o2 = jnp.concatenate([o_all[g * AB:(g + 1) * AB] / dens[g],
                                  o_all[(g + 1) * AB:(g + 2) * AB] / dens[g + 1]], axis=1)
            o_ref[:, (hk * gsz + g) * HD:(hk * gsz + g + 2) * HD] = o2.astype(BF)


def _attention(pb, pkv, sink):
    lat0 = TC // AB

    def qrow(b, j):
        return jnp.where(j < NCB, b * NCB + j, lat0 + b * NB + j - NCB)

    def krow(off):
        def f(b, j):
            i = jnp.clip(j - NCB + off, 0, NB - 1)
            return (lat0 + b * NB + i, 0)
        return f

    return pl.pallas_call(
        _attn_kernel,
        grid=(B, NCB + NB),
        in_specs=[
            pl.BlockSpec((AB, HQ * HD), lambda b, j: (qrow(b, j), 0)),
            pl.BlockSpec((AB, KV_COLS), krow(-1)),
            pl.BlockSpec((AB, KV_COLS), krow(0)),
            pl.BlockSpec((AB, KV_COLS), krow(1)),
            pl.BlockSpec((LC, KV_COLS), lambda b, j: (b, 0)),
            pl.BlockSpec(memory_space=pltpu.SMEM),
        ],
        out_specs=pl.BlockSpec((AB, HQ * HD), lambda b, j: (qrow(b, j), 0)),
        out_shape=jax.ShapeDtypeStruct((R, HQ * HD), BF),
        compiler_params=_cparams(("arbitrary", "arbitrary")),
        name="attention",
    )(pb, pkv, pkv, pkv, pkv, sink)


MG_TN = 512


def _merge_kernel(gla_ref, att_ref, g1_ref, g2_ref, w1_ref, w2_ref, o_ref, w1b_ref, w2b_ref):
    @pl.when(pl.program_id(1) == 0)
    def _():
        w1b_ref[...] = w1_ref[...].astype(BF)
        w2b_ref[...] = w2_ref[...].astype(BF)

    y1 = jnp.dot(gla_ref[...], w1b_ref[...], preferred_element_type=F32)
    y2 = jnp.dot(att_ref[...], w2b_ref[...], preferred_element_type=F32)
    y = _sigmoid(g1_ref[...].astype(F32)) * y1 + _sigmoid(g2_ref[...].astype(F32)) * y2
    o_ref[...] = y.astype(BF)


def _merge(gla, att, pb, w1, w2, l, t0):
    nt = NT_ALL - t0
    g1c = 2048 // MG_TN
    g2c = 4096 // MG_TN
    return pl.pallas_call(
        _merge_kernel,
        grid=(D // MG_TN, nt),
        in_specs=[
            pl.BlockSpec((TM, GH * GDV), lambda j, i: (i + t0, 0)),
            pl.BlockSpec((TM, HQ * HD), lambda j, i: (i + t0, 0)),
            pl.BlockSpec((TM, MG_TN), lambda j, i: (i + t0, g1c + j)),
            pl.BlockSpec((TM, MG_TN), lambda j, i: (i + t0, g2c + j)),
            pl.BlockSpec((None, GH * GDV, MG_TN), lambda j, i: (l, 0, j)),
            pl.BlockSpec((None, HQ * HD, MG_TN), lambda j, i: (l, 0, j)),
        ],
        out_specs=pl.BlockSpec((TM, MG_TN), lambda j, i: (i, j)),
        out_shape=jax.ShapeDtypeStruct((nt * TM, D), BF),
        scratch_shapes=[pltpu.VMEM((GH * GDV, MG_TN), BF), pltpu.VMEM((HQ * HD, MG_TN), BF)],
        compiler_params=_cparams(("arbitrary", "arbitrary")),
        name="merge",
    )(gla, att, pb, pb, w1, w2)


def _residual_norm(y, x, gate, g_post):
    ms = jnp.mean(y * y, axis=-1, keepdims=True)
    return x + gate * (y * lax.rsqrt(ms + EPS) * g_post)


def _rownorm_kernel(a_ref, w_ref, x_ref, gt_ref, gp_ref, gn_ref, sc_ref, sh_ref, o_ref, h_ref, acc_ref):
    k = pl.program_id(1)

    @pl.when(k == 0)
    def _():
        acc_ref[...] = jnp.zeros_like(acc_ref)

    acc_ref[...] += jnp.dot(a_ref[...], w_ref[...].astype(BF), preferred_element_type=F32)

    @pl.when(k == pl.num_programs(1) - 1)
    def _():
        xn = _residual_norm(acc_ref[...], x_ref[...], gt_ref[...], gp_ref[...])
        o_ref[...] = xn
        h_ref[...] = _norm_mod(xn, gn_ref[...], sc_ref[...], sh_ref[...]).astype(BF)


DN_TK = DFF // 4


def _rownorm(a, w_full, l, kdim, tk, *, a_ctx, t0, x, modl, k_gate, g_post, g_next, modn, k_sc, k_sh):
    nt = NT_ALL - t0
    a_off = t0 if a_ctx else 0

    def mspec(k_chunk):
        return pl.BlockSpec((None, 1, D), lambda i, k: (_mrow(i + t0) * 6 + k_chunk, 0, 0))

    vec = pl.BlockSpec((1, D), lambda i, k: (0, 0))
    return pl.pallas_call(
        _rownorm_kernel,
        grid=(nt, kdim // tk),
        in_specs=[
            pl.BlockSpec((TM, tk), lambda i, k: (i + a_off, k)),
            pl.BlockSpec((None, tk, D), lambda i, k: (l, k, 0)),
            pl.BlockSpec((TM, D), lambda i, k: (i + t0, 0)),
            mspec(k_gate), vec, vec, mspec(k_sc), mspec(k_sh),
        ],
        out_specs=[pl.BlockSpec((TM, D), lambda i, k: (i, 0)), pl.BlockSpec((TM, D), lambda i, k: (i, 0))],
        out_shape=[jax.ShapeDtypeStruct((nt * TM, D), F32), jax.ShapeDtypeStruct((nt * TM, D), BF)],
        scratch_shapes=[pltpu.VMEM((TM, D), F32)],
        compiler_params=_cparams(("arbitrary", "arbitrary")),
        name="rownorm",
    )(a, w_full, x, modl, g_post.reshape(1, D), g_next.reshape(1, D), modn, modn)


WO_CH = 4


def _outproj_kernel(a_ref, w_ref, x_ref, gt_ref, gp_ref, gn_ref, sc_ref, sh_ref, o_ref, h_ref, wbf_ref):
    s = pl.program_id(0)

    @pl.when(s < WO_CH)
    def _():
        ck = D // WO_CH
        wbf_ref[pl.ds(pl.multiple_of(s * ck, ck), ck), :] = w_ref[...].astype(BF)

    @pl.when(s >= WO_CH - 1)
    def _():
        y = jnp.dot(a_ref[...], wbf_ref[...], preferred_element_type=F32)
        xn = _residual_norm(y, x_ref[...], gt_ref[...], gp_ref[...])
        o_ref[...] = xn
        h_ref[...] = _norm_mod(xn, gn_ref[...], sc_ref[...], sh_ref[...]).astype(BF)


def _outproj(a, w_full, l, *, t0, x, modl, k_gate, g_post, g_next, modn, k_sc, k_sh):
    nt = NT_ALL - t0
    ck = D // WO_CH

    def tile(s):
        return jnp.maximum(s - (WO_CH - 1), 0)

    def mspec(k_chunk):
        return pl.BlockSpec((None, 1, D), lambda s: (_mrow(tile(s) + t0) * 6 + k_chunk, 0, 0))

    vec = pl.BlockSpec((1, D), lambda s: (0, 0))
    row = pl.BlockSpec((TM, D), lambda s: (tile(s), 0))
    return pl.pallas_call(
        _outproj_kernel,
        grid=(nt + WO_CH - 1,),
        in_specs=[
            row,
            pl.BlockSpec((None, ck, D), lambda s: (l, jnp.minimum(s, WO_CH - 1), 0)),
            pl.BlockSpec((TM, D), lambda s: (tile(s) + t0, 0)),
            mspec(k_gate), vec, vec, mspec(k_sc), mspec(k_sh),
        ],
        out_specs=[row, row],
        out_shape=[jax.ShapeDtypeStruct((nt * TM, D), F32), jax.ShapeDtypeStruct((nt * TM, D), BF)],
        scratch_shapes=[pltpu.VMEM((D, D), BF)],
        compiler_params=_cparams(("arbitrary",)),
        name="outproj",
    )(a, w_full, x, modl, g_post.reshape(1, D), g_next.reshape(1, D), modn, modn)


FF_TN = 512


def _new_expert(te_ref, i):
    return jnp.logical_or(i == 0, te_ref[i] != te_ref[jnp.maximum(i - 1, 0)])


def _ffn_up_kernel(te_ref, act_ref, a_ref, wg_ref, wu_ref, o_ref, wgb_ref, wub_ref):
    i = pl.program_id(1)

    @pl.when(_new_expert(te_ref, i))
    def _():
        wgb_ref[...] = wg_ref[...].astype(BF)
        wub_ref[...] = wu_ref[...].astype(BF)

    @pl.when(act_ref[i] == 1)
    def _():
        a = a_ref[...]
        yg = jnp.dot(a, wgb_ref[...], preferred_element_type=F32)
        yu = jnp.dot(a, wub_ref[...], preferred_element_type=F32)
        o_ref[...] = (_silu(yg) * yu).astype(BF)

    @pl.when(act_ref[i] == 0)
    def _():
        o_ref[...] = jnp.zeros_like(o_ref)


def _ffn_up(a, wg, wu, lead, te, act):
    nt = a.shape[0] // TM
    wspec = pl.BlockSpec((None, None, D, FF_TN), lambda j, i, te, act: (lead, te[i], 0, j))
    return pl.pallas_call(
        _ffn_up_kernel,
        grid_spec=pltpu.PrefetchScalarGridSpec(
            num_scalar_prefetch=2,
            grid=(DFF // FF_TN, nt),
            in_specs=[pl.BlockSpec((TM, D), lambda j, i, te, act: (i, 0)), wspec, wspec],
            out_specs=pl.BlockSpec((TM, FF_TN), lambda j, i, te, act: (i, j)),
            scratch_shapes=[pltpu.VMEM((D, FF_TN), BF), pltpu.VMEM((D, FF_TN), BF)],
        ),
        out_shape=jax.ShapeDtypeStruct((nt * TM, DFF), BF),
        compiler_params=_cparams(("arbitrary", "arbitrary")),
        name="ffn_up",
    )(te, act, a, wg, wu)


DN_TN = 512


def _moe_down_kernel(te_ref, act_ref, a_ref, w_ref, s_ref, o_ref, wb_ref):
    i = pl.program_id(1)

    @pl.when(_new_expert(te_ref, i))
    def _():
        wb_ref[...] = w_ref[...].astype(BF)

    @pl.when(act_ref[i] == 1)
    def _():
        y = jnp.dot(a_ref[...], wb_ref[...], preferred_element_type=F32)
        o_ref[...] = (y * s_ref[...]).astype(BF)

    @pl.when(act_ref[i] == 0)
    def _():
        o_ref[...] = jnp.zeros_like(o_ref)


def _moe_down(u, wd, lead, wrow, te, act):
    nt = u.shape[0] // TM
    return pl.pallas_call(
        _moe_down_kernel,
        grid_spec=pltpu.PrefetchScalarGridSpec(
            num_scalar_prefetch=2,
            grid=(D // DN_TN, nt),
            in_specs=[
                pl.BlockSpec((TM, DFF), lambda j, i, te, act: (i, 0)),
                pl.BlockSpec((None, None, DFF, DN_TN), lambda j, i, te, act: (lead, te[i], 0, j)),
                pl.BlockSpec((TM, 1), lambda j, i, te, act: (i, 0)),
            ],
            out_specs=pl.BlockSpec((TM, DN_TN), lambda j, i, te, act: (i, j)),
            scratch_shapes=[pltpu.VMEM((DFF, DN_TN), BF)],
        ),
        out_shape=jax.ShapeDtypeStruct((nt * TM, D), BF),
        compiler_params=_cparams(("arbitrary", "arbitrary")),
        name="moe_down",
    )(te, act, u, wd, wrow)


def _router_kernel(x_ref, g_ref, sc_ref, sh_ref, wr_ref, br_ref, o_ref, pos_ref, sel_ref, carry_ref):
    @pl.when(pl.program_id(0) == 0)
    def _():
        carry_ref[...] = jnp.zeros_like(carry_ref)

    h = _norm_mod(x_ref[...], g_ref[...], sc_ref[...], sh_ref[...])
    hh, hl = _split(h)
    wh, wl = _split(wr_ref[...])
    nt = (((1,), (1,)), ((), ()))
    lg = (lax.dot_general(wh, hh, nt, preferred_element_type=F32)
          + lax.dot_general(wh, hl, nt, preferred_element_type=F32)
          + lax.dot_general(wl, hh, nt, preferred_element_type=F32) + br_ref[...])
    idx = lax.broadcasted_iota(jnp.int32, lg.shape, 0)
    m1 = jnp.max(lg, axis=0, keepdims=True)
    i1 = jnp.min(jnp.where(lg == m1, idx, NE), axis=0, keepdims=True)
    l2 = jnp.where(idx == i1, NEG, lg)
    m2 = jnp.max(l2, axis=0, keepdims=True)
    i2 = jnp.min(jnp.where(l2 == m2, idx, NE), axis=0, keepdims=True)
    e2 = jnp.exp(m2 - m1)
    w1 = 1.0 / (1.0 + e2)
    w2 = e2 / (1.0 + e2)
    o_ref[...] = jnp.where(idx == i1, w1, 0.0) + jnp.where(idx == i2, w2, 0.0)
    sel = jnp.where(idx == i1, 1.0, jnp.where(idx == i2, 1.0, 0.0))
    si = lax.broadcasted_iota(jnp.int32, (TM, TM), 0)
    ti = lax.broadcasted_iota(jnp.int32, (TM, TM), 1)
    before = jnp.where(si < ti, 1.0, 0.0).astype(BF)
    excl = jnp.dot(sel.astype(BF), before, preferred_element_type=F32)
    pos_ref[...] = (excl + carry_ref[...]).astype(jnp.int32)
    sel_ref[...] = sel.astype(jnp.int32)
    carry_ref[...] += jnp.sum(sel, axis=1, keepdims=True)


def _router(x_lat, g, modl, k_sc, k_sh, w_router, b_router):
    ospec = pl.BlockSpec((NE, TM), lambda i: (0, i))
    return pl.pallas_call(
        _router_kernel,
        grid=(T // TM,),
        in_specs=[
            pl.BlockSpec((TM, D), lambda i: (i, 0)),
            pl.BlockSpec((1, D), lambda i: (0, 0)),
            pl.BlockSpec((None, 1, D), lambda i: (_mrow(i + 1) * 6 + k_sc, 0, 0)),
            pl.BlockSpec((None, 1, D), lambda i: (_mrow(i + 1) * 6 + k_sh, 0, 0)),
            pl.BlockSpec((NE, D), lambda i: (0, 0)),
            pl.BlockSpec((NE, 1), lambda i: (0, 0)),
        ],
        out_specs=[ospec, ospec, ospec],
        out_shape=[jax.ShapeDtypeStruct((NE, T), F32), jax.ShapeDtypeStruct((NE, T), jnp.int32),
                   jax.ShapeDtypeStruct((NE, T), jnp.int32)],
        scratch_shapes=[pltpu.VMEM((NE, 1), F32)],
        compiler_params=_cparams(("arbitrary",)),
        name="router",
    )(x_lat, g.reshape(1, D), modl, modl, w_router.T, b_router.reshape(NE, 1))


NSB = T // TM
NP = 2 * T // TM + NE
NSTEP = NE * NSB * 2
CT = 256
NCT = T // CT
ROW_ALIGN = 16
WIN = CT + ROW_ALIGN
WBUF = CT + 128


def _route_tables(pos, sel):
    i32 = jnp.int32
    counts = pos[:, -1] + sel[:, -1]
    ntile = (counts + TM - 1) // TM
    tend = jnp.cumsum(ntile)
    seg = (tend - ntile) * TM
    dest = jnp.where(sel > 0, seg[:, None] + pos, -1)
    d_hi = jnp.max(dest, axis=0)
    d_lo = jnp.sum(dest, axis=0) + (NE - 2) - d_hi
    tiles = jnp.arange(NP, dtype=i32)
    total = tend[-1]
    te = jnp.sum((tiles[:, None] >= tend[None, :]).astype(i32), axis=1)
    te_last = jnp.sum(((total - 1) >= tend).astype(i32))
    act = (tiles < total).astype(i32)
    te = jnp.where(act > 0, te, te_last)
    pb = pos[:, ::TM]
    pe = jnp.concatenate([pb[:, 1:], counts[:, None]], axis=1)
    first_row = seg[:, None] + pb
    cnt = pe - pb
    t0 = first_row // TM
    nblk = jnp.where(cnt > 0, (first_row + cnt - 1) // TM - t0 + 1, 0)
    kk = jnp.arange(2, dtype=i32)
    valid = kk[None, None, :] < nblk[:, :, None]
    blk = jnp.minimum(t0[:, :, None] + kk[None, None, :], NP - 1)
    spare = total + jnp.arange(NE, dtype=i32)
    v_d = jnp.concatenate([valid.reshape(NSTEP), jnp.zeros((NE,), bool)])
    b_d = jnp.concatenate([blk.reshape(NSTEP), jnp.minimum(spare, NP - 1)])
    fresh = jnp.concatenate([v_d[:NSTEP], spare < NP])
    cm = lax.cummax(jnp.where(fresh, b_d, -1), axis=0)
    prev = jnp.concatenate([jnp.full((1,), -1, i32), cm[:-1]])
    pad = jnp.zeros((NE // 2,), i32)
    disp = dict(blk=jnp.maximum(cm, 0), valid=v_d.astype(i32),
                first=jnp.logical_and(fresh, cm != prev).astype(i32),
                r0=jnp.concatenate([first_row.reshape(NE * NSB), pad]),
                cnt=jnp.concatenate([cnt.reshape(NE * NSB), pad]))
    pbc = pos[:, ::CT]
    r0c = (seg[:, None] + pbc).T.reshape(NCT * NE)
    cntc = (jnp.concatenate([pbc[:, 1:], counts[:, None]], axis=1) - pbc).T.reshape(NCT * NE)
    a0c = jnp.minimum((r0c // ROW_ALIGN) * ROW_ALIGN, NP * TM - WIN)
    comb = dict(a0=a0c, r0=r0c, cnt=cntc)
    return dict(dest=dest, d_hi=d_hi.reshape(T, 1), d_lo=d_lo.reshape(T, 1), te=te, act=act,
                disp=disp, comb=comb)


def _dispatch_kernel(blk_ref, val_ref, first_ref, r0_ref, cnt_ref, h_ref, dest_ref, cw_ref, xs_ref, wr_ref):
    s = pl.program_id(0)
    e = jnp.minimum(s // (2 * NSB), NE - 1)

    @pl.when(first_ref[s] == 1)
    def _():
        xs_ref[...] = jnp.zeros_like(xs_ref)
        wr_ref[...] = jnp.zeros_like(wr_ref)

    r0 = r0_ref[s // 2]
    r1 = r0 + cnt_ref[s // 2]
    half = TM // 2
    for hf in range(2):
        lo = blk_ref[s] * TM + hf * half

        @pl.when(jnp.logical_and(val_ref[s] == 1, jnp.logical_and(r0 < lo + half, r1 > lo)))
        def _():
            rows = slice(hf * half, (hf + 1) * half)
            drow = dest_ref[pl.ds(e, 1), :]
            crow = cw_ref[pl.ds(e, 1), :]
            hit = drow == lax.broadcasted_iota(jnp.int32, (half, TM), 0) + lo
            onehot = jnp.where(hit, 1.0, 0.0).astype(BF)
            g = jnp.dot(onehot, h_ref[...], preferred_element_type=F32)
            xs_ref[rows, :] = (xs_ref[rows, :].astype(F32) + g).astype(BF)
            wr_ref[rows, :] += jnp.sum(jnp.where(hit, crow, 0.0), axis=1, keepdims=True)


def _dispatch(h_lat, dest, comb_w, tb):
    def omap(s, blk, *_):
        return (blk[s], 0)

    def tok(s, *_):
        return (s // 2) % NSB

    return pl.pallas_call(
        _dispatch_kernel,
        grid_spec=pltpu.PrefetchScalarGridSpec(
            num_scalar_prefetch=5,
            grid=(NSTEP + NE,),
            in_specs=[
                pl.BlockSpec((TM, D), lambda s, *_: (tok(s), 0)),
                pl.BlockSpec((NE, TM), lambda s, *_: (0, tok(s))),
                pl.BlockSpec((NE, TM), lambda s, *_: (0, tok(s))),
            ],
            out_specs=[pl.BlockSpec((TM, D), omap), pl.BlockSpec((TM, 1), omap)],
        ),
        out_shape=[jax.ShapeDtypeStruct((NP * TM, D), BF), jax.ShapeDtypeStruct((NP * TM, 1), F32)],
        compiler_params=_cparams(("arbitrary",)),
        name="moe_dispatch",
    )(tb["blk"], tb["valid"], tb["first"], tb["r0"], tb["cnt"], h_lat, dest, comb_w)


def _combine_kernel(a0_ref, r0_ref, cnt_ref, y_hbm, dhi_ref, dlo_ref, x_ref, gt_ref, gp_ref, o_ref,
                    ybuf, acc_ref, sem):
    i = pl.program_id(0)
    slot = lax.rem(i, 2)

    def window_copies(tile, buf):
        out = []
        for e in range(NE):
            a0 = pl.multiple_of(a0_ref[tile * NE + e], ROW_ALIGN)
            out.append(pltpu.make_async_copy(y_hbm.at[pl.ds(a0, WIN), :], ybuf.at[buf, e, pl.ds(0, WIN), :],
                                             sem.at[buf, e]))
        return out

    @pl.when(i == 0)
    def _():
        ybuf[:, :, WIN:, :] = jnp.zeros((2, NE, WBUF - WIN, D), BF)
        for cp in window_copies(0, 0):
            cp.start()

    @pl.when(i + 1 < pl.num_programs(0))
    def _():
        for cp in window_copies(i + 1, 1 - slot):
            cp.start()

    for cp in window_copies(i, slot):
        cp.wait()

    dhi = dhi_ref[...]
    dlo = dlo_ref[...]

    def onehot(e, c0, width):
        idx = i * NE + e
        r0 = r0_ref[idx]
        ids = lax.broadcasted_iota(jnp.int32, (1, width), 1) + (a0_ref[idx] + c0)
        ids = jnp.where(jnp.logical_and(ids >= r0, ids < r0 + cnt_ref[idx]), ids, -1)
        return jnp.where(dhi == ids, 1.0, jnp.where(dlo == ids, 1.0, 0.0)).astype(BF)

    acc = jnp.zeros((CT, D), F32)
    for e in range(NE):
        acc = acc + jnp.dot(onehot(e, 0, CT), ybuf[slot, e, 0:CT, :], preferred_element_type=F32)
    acc_ref[...] = acc
    for e in range(NE):
        idx = i * NE + e

        @pl.when(r0_ref[idx] + cnt_ref[idx] > a0_ref[idx] + CT)
        def _():
            acc_ref[...] += jnp.dot(onehot(e, CT, WBUF - CT), ybuf[slot, e, CT:, :], preferred_element_type=F32)

    o_ref[...] = _residual_norm(acc_ref[...], x_ref[...], gt_ref[...], gp_ref[...])


def _combine(y, d_hi, d_lo, x_lat, modl, k_gate, g_post, tb):
    return pl.pallas_call(
        _combine_kernel,
        grid_spec=pltpu.PrefetchScalarGridSpec(
            num_scalar_prefetch=3,
            grid=(NCT,),
            in_specs=[
                pl.BlockSpec(memory_space=pl.ANY),
                pl.BlockSpec((CT, 1), lambda t, *_: (t, 0)),
                pl.BlockSpec((CT, 1), lambda t, *_: (t, 0)),
                pl.BlockSpec((CT, D), lambda t, *_: (t, 0)),
                pl.BlockSpec((None, 1, D), lambda t, *_: (_mrow(t // (TM // CT) + 1) * 6 + k_gate, 0, 0)),
                pl.BlockSpec((1, D), lambda t, *_: (0, 0)),
            ],
            out_specs=pl.BlockSpec((CT, D), lambda t, *_: (t, 0)),
            scratch_shapes=[pltpu.VMEM((2, NE, WBUF, D), BF), pltpu.VMEM((CT, D), F32),
                            pltpu.SemaphoreType.DMA((2, NE))],
        ),
        out_shape=jax.ShapeDtypeStruct((T, D), F32),
        compiler_params=_cparams(("arbitrary",)),
        name="moe_combine",
    )(tb["a0"], tb["r0"], tb["cnt"], y, d_hi, d_lo, x_lat, modl, g_post.reshape(1, D))


def kernel(x, c, ctx, c_ctx, w_mod, b_mod, g_pre_mix, g_post_mix, g_pre_ffn, g_post_ffn,
           w_in, w_lr_f, b_lr_f, w_lr_b, b_lr_b, g_gla, att_sink, w_br_gla, w_br_att, w_out,
           w_ffn_gate, w_ffn_up, w_ffn_down, w_router, b_router, w_exp_gate, w_exp_up, w_exp_down):
    cvec = jnp.zeros((8, D), F32).at[0:B].set(c).at[B].set(c_ctx)
    mod = _modulation(cvec, w_mod, b_mod).reshape(DEPTH, 8 * 6, 1, D)
    xa = jnp.concatenate([ctx.reshape(TC, D), x.reshape(T, D)], axis=0)
    cos, sin_a, sin_b = _rope_tables()
    wt = jnp.swapaxes(w_in, 1, 2)

    h = _prenorm(xa, g_pre_mix[0], mod[0], 1, 0)
    for l in range(DEPTH):
        last = l == DEPTH - 1
        t0 = 1 if last else 0
        modl = mod[l]
        pa = _proj(h, wt, l, 0, C_LF, 1024, BF, "proj_gla")
        pb, pkv = _proj_att(h, wt, l, (cos, sin_a, sin_b))
        lr = _proj(h, wt, l, C_LF, 2 * GRANK, 2 * GRANK, F32, "proj_decay")

        wlr = jnp.zeros((2, 2 * GRANK, GH * GDK), F32)
        wlr = wlr.at[0, 0:GRANK].set(w_lr_f[l]).at[1, GRANK:2 * GRANK].set(w_lr_b[l])
        blr = jnp.stack([b_lr_f[l], b_lr_b[l]]).reshape(2, 1, GH * GDK)
        o_f, o_b = _gla_scan(_gla_prep(pa, lr, wlr, blr), pa)
        gla = _gla_readout(o_f, o_b, pa, g_gla[l])
        att = _attention(pb, pkv, att_sink[l])
        mm = _merge(gla, att, pb, w_br_gla, w_br_att, l, t0)

        xa, h2 = _outproj(mm, w_out, l, t0=t0, x=xa, modl=modl, k_gate=2,
                          g_post=g_post_mix[l], g_next=g_pre_ffn[l], modn=modl, k_sc=4, k_sh=3)
        if not last:
            ones = jnp.ones((NT_ALL,), jnp.int32)
            u = _ffn_up(h2, w_ffn_gate[:, None], w_ffn_up[:, None], l // 2, 0 * ones, ones)
            xa, h = _rownorm(u, w_ffn_down, l // 2, DFF, DN_TK, a_ctx=True, t0=0, x=xa, modl=modl, k_gate=5,
                             g_post=g_post_ffn[l], g_next=g_pre_mix[l + 1], modn=mod[l + 1], k_sc=1, k_sh=0)
        else:
            comb_w, pos, sel = _router(xa, g_pre_ffn[l], modl, 4, 3, w_router[l // 2], b_router[l // 2])
            rt = _route_tables(pos, sel)
            xs, wrow = _dispatch(h2, rt["dest"], comb_w, rt["disp"])
            u = _ffn_up(xs, w_exp_gate, w_exp_up, l // 2, rt["te"], rt["act"])
            y = _moe_down(u, w_exp_down, l // 2, wrow, rt["te"], rt["act"])
            xa = _combine(y, rt["d_hi"], rt["d_lo"], xa, modl, 5, g_post_ffn[l], rt["comb"])
    return xa.reshape(B, L, D)
```

```python
import functools

import jax
import jax.numpy as jnp
import numpy as np
from jax import lax
from jax.experimental import pallas as pl
from jax.experimental.pallas import tpu as pltpu

BF = jnp.bfloat16
F32 = jnp.float32

D = 2048
B = 2
L = 4096
LC = 256
T = B * L
TC = B * LC
R = TC + T
DEPTH = 2
GRID_W = 64
EPS = 1e-6

GH = 4
GDK = 256
GDV = 512
GRANK = 16
GTAU = 16.0
GC = 64

HD = 64
HQ = 32
HKV = 4
ROPE_BASE = 10000.0

DFF = 5632
NE = 8

TM = 512
NT_ALL = R // TM
TPB = L // TM
VMEM_LIMIT = 56 * 1024 * 1024

C_GQ, C_GK, C_GV, C_GR, C_LF, C_LB, C_AQ, C_AK, C_AV, C_G1, C_G2, C_END = (
    0, 1024, 2048, 4096, 6144, 6160, 6176, 8224, 8480, 8736, 10784, 12832)


def _cparams(sem):
    return pltpu.CompilerParams(dimension_semantics=sem, vmem_limit_bytes=VMEM_LIMIT)


def _mrow(gi):
    return jnp.where(gi == 0, 2, (gi - 1) // TPB)


def _sigmoid(z):
    return 1.0 / (1.0 + jnp.exp(-z))


def _silu(z):
    return z * _sigmoid(z)


def _split(a):
    hi = a.astype(BF)
    lo = (a - hi.astype(F32)).astype(BF)
    return hi, lo


def _mod_kernel(c_ref, w_ref, b_ref, o_ref):
    a = _silu(c_ref[...]).astype(BF)
    o_ref[...] = jnp.dot(a, w_ref[...].astype(BF), preferred_element_type=F32) + b_ref[...]


def _modulation(cvec, w_mod, b_mod):
    tn = 1024
    return pl.pallas_call(
        _mod_kernel,
        grid=(DEPTH, 6 * D // tn),
        in_specs=[
            pl.BlockSpec((8, D), lambda l, j: (0, 0)),
            pl.BlockSpec((None, D, tn), lambda l, j: (l, 0, j)),
            pl.BlockSpec((None, 1, tn), lambda l, j: (l, 0, j)),
        ],
        out_specs=pl.BlockSpec((None, 8, tn), lambda l, j: (l, 0, j)),
        out_shape=jax.ShapeDtypeStruct((DEPTH, 8, 6 * D), F32),
        compiler_params=_cparams(("arbitrary", "arbitrary")),
        name="modulation",
    )(cvec, w_mod, b_mod.reshape(DEPTH, 1, 6 * D))


def _norm_mod(x, g, sc, sh):
    ms = jnp.mean(x * x, axis=-1, keepdims=True)
    return (x * lax.rsqrt(ms + EPS) * g) * (1.0 + sc) + sh


def _prenorm_kernel(x_ref, g_ref, sc_ref, sh_ref, o_ref):
    o_ref[...] = _norm_mod(x_ref[...], g_ref[...], sc_ref[...], sh_ref[...]).astype(BF)


def _prenorm(xa, g, modl, k_sc, k_sh):
    return pl.pallas_call(
        _prenorm_kernel,
        grid=(NT_ALL,),
        in_specs=[
            pl.BlockSpec((TM, D), lambda i: (i, 0)),
            pl.BlockSpec((1, D), lambda i: (0, 0)),
            pl.BlockSpec((None, 1, D), lambda i: (_mrow(i) * 6 + k_sc, 0, 0)),
            pl.BlockSpec((None, 1, D), lambda i: (_mrow(i) * 6 + k_sh, 0, 0)),
        ],
        out_specs=pl.BlockSpec((TM, D), lambda i: (i, 0)),
        out_shape=jax.ShapeDtypeStruct((R, D), BF),
        compiler_params=_cparams(("arbitrary",)),
        name="prenorm",
    )(xa, g.reshape(1, D), modl, modl)


NT_DIMS = (((1,), (1,)), ((), ()))


def _proj_kernel(a_ref, wt_ref, o_ref, wbf_ref):
    @pl.when(pl.program_id(1) == 0)
    def _():
        wbf_ref[...] = wt_ref[...].astype(BF)

    acc = lax.dot_general(a_ref[...], wbf_ref[...], NT_DIMS, preferred_element_type=F32)
    o_ref[...] = acc.astype(o_ref.dtype)


def _proj(h, wt, l, row0, n_cols, tn, out_dtype, name):
    return pl.pallas_call(
        _proj_kernel,
        grid=(n_cols // tn, NT_ALL),
        in_specs=[
            pl.BlockSpec((TM, D), lambda j, i: (i, 0)),
            pl.BlockSpec((None, tn, D), lambda j, i: (l, row0 // tn + j, 0)),
        ],
        out_specs=pl.BlockSpec((TM, tn), lambda j, i: (i, j)),
        out_shape=jax.ShapeDtypeStruct((R, n_cols), out_dtype),
        scratch_shapes=[pltpu.VMEM((tn, D), BF)],
        compiler_params=_cparams(("arbitrary", "arbitrary")),
        name=name,
    )(h, wt)


PB_TN = 1024
PB_COLS = 2048 + 4096
PB_QT = 2048 // PB_TN
KV_COLS = 2 * HKV * HD
LOG2E = 1.4426950408889634
Q_SCALE = HD ** -0.5 * LOG2E


def _rope(x, cos, sin_a, sin_b):
    return x * cos + pltpu.roll(x, 112, axis=1) * sin_a + pltpu.roll(x, 16, axis=1) * sin_b


def _proj_att_kernel(a_ref, wt_ref, cos_ref, sa_ref, sb_ref, o_ref, wbf_ref, *, n_q, rope_cols):
    j = pl.program_id(0)
    i = pl.program_id(1)

    @pl.when(i == 0)
    def _():
        wbf_ref[...] = wt_ref[0].astype(BF)

    acc = lax.dot_general(a_ref[...], wbf_ref[...], NT_DIMS, preferred_element_type=F32)
    tn = acc.shape[1]
    latent = i > 0
    is_q = j < n_q

    def rotated(cols, scale):
        for s in range(cols // 128):
            y = _rope(acc[:, s * 128:(s + 1) * 128], cos_ref[...], sa_ref[...], sb_ref[...])
            o_ref[:, s * 128:(s + 1) * 128] = (y * scale).astype(BF)

    @pl.when(jnp.logical_and(is_q, latent))
    def _():
        rotated(tn, Q_SCALE)

    @pl.when(jnp.logical_and(is_q, jnp.logical_not(latent)))
    def _():
        o_ref[...] = (acc * Q_SCALE).astype(BF)

    plain = jnp.logical_not(is_q)
    if rope_cols:
        plain = jnp.logical_and(plain, jnp.logical_not(latent))

        @pl.when(jnp.logical_and(jnp.logical_not(is_q), latent))
        def _():
            rotated(rope_cols, 1.0)
            o_ref[:, rope_cols:] = acc[:, rope_cols:].astype(BF)

    @pl.when(plain)
    def _():
        o_ref[...] = acc.astype(BF)


def _proj_att_call(h, wt, l, tabs, *, tn, n_tiles, n_q, rope_cols, wrow_units, name):
    def tab(j, i):
        return (jnp.where(i == 0, 0, (i - 1) % TPB), 0)

    return pl.pallas_call(
        functools.partial(_proj_att_kernel, n_q=n_q, rope_cols=rope_cols),
        grid=(n_tiles, NT_ALL),
        in_specs=[
            pl.BlockSpec((TM, D), lambda j, i: (i, 0)),
            pl.BlockSpec((pl.Element(1), pl.Element(tn), pl.Element(D)),
                         lambda j, i: (l, wrow_units(j) * (2 * GRANK), 0)),
            pl.BlockSpec((TM, 128), tab),
            pl.BlockSpec((TM, 128), tab),
            pl.BlockSpec((TM, 128), tab),
        ],
        out_specs=pl.BlockSpec((TM, tn), lambda j, i: (i, j)),
        out_shape=jax.ShapeDtypeStruct((R, n_tiles * tn), BF),
        scratch_shapes=[pltpu.VMEM((tn, D), BF)],
        compiler_params=_cparams(("arbitrary", "arbitrary")),
        name=name,
    )(h, wt, *tabs)


def _proj_att(h, wt, l, tabs):
    u = 2 * GRANK
    pb = _proj_att_call(
        h, wt, l, tabs, tn=PB_TN, n_tiles=PB_COLS // PB_TN, n_q=PB_QT, rope_cols=0, name="proj_att",
        wrow_units=lambda j: jnp.where(j < PB_QT, C_AQ // u + j * (PB_TN // u),
                                       C_G1 // u + (j - PB_QT) * (PB_TN // u)))
    pkv = _proj_att_call(h, wt, l, tabs, tn=KV_COLS, n_tiles=1, n_q=0, rope_cols=HKV * HD, name="proj_kv",
                         wrow_units=lambda j: C_AK // u + j)
    return pb, pkv


def _rope_tables():
    rows = L // GRID_W
    row = np.repeat(np.arange(rows), GRID_W)
    col = np.tile(np.arange(GRID_W), rows)
    half = HD // 2
    inv = (ROPE_BASE ** (-np.arange(0, half, 2, dtype=np.float32) / half)).astype(np.float32)

    def angles(p):
        a = p.astype(np.float32)[:, None] * inv[None, :]
        return np.concatenate([a, a], axis=-1)

    ang = np.concatenate([angles(row), angles(col)], axis=-1)
    ang = np.concatenate([ang, ang], axis=-1)
    cos, sin = np.cos(ang).astype(np.float32), np.sin(ang).astype(np.float32)
    first = (np.arange(128) % 32) < 16
    zero = np.float32(0.0)
    return jnp.asarray(cos), jnp.asarray(np.where(first, -sin, zero)), jnp.asarray(np.where(first, zero, sin))


GG = 256
NG = 1 + L // GG


def _dot_split(m01, a):
    hi, lo = _split(a)
    return jnp.dot(m01, hi, preferred_element_type=F32) + jnp.dot(m01, lo, preferred_element_type=F32)


def _gla_prep_kernel(lr_ref, wlr_ref, blr_ref, q_ref, k_ref, *out_refs):
    lh, ll = _split(lr_ref[...])
    ri = lax.broadcasted_iota(jnp.int32, (TM, TM), 0)
    ci = lax.broadcasted_iota(jnp.int32, (TM, TM), 1)
    same = jnp.right_shift(ri, GC.bit_length() - 1) == jnp.right_shift(ci, GC.bit_length() - 1)
    blk = jnp.where(same, 1.0, 0.0).astype(BF)
    qf = q_ref[...].astype(F32) * (GDK ** -0.5)
    kf = k_ref[...].astype(F32)
    for d in range(2):
        wh, wl = _split(wlr_ref[d])
        z = (jnp.dot(lh, wh, preferred_element_type=F32) + jnp.dot(ll, wh, preferred_element_type=F32)
             + jnp.dot(lh, wl, preferred_element_type=F32) + blr_ref[d])
        la = (jnp.minimum(z, 0.0) - jnp.log1p(jnp.exp(-jnp.abs(z)))) * (1.0 / GTAU)
        keep = (ri >= ci) if d == 0 else (ri <= ci)
        tri = jnp.where(jnp.logical_and(same, keep), 1.0, 0.0).astype(BF)
        bcum = _dot_split(tri, la)
        tot = _dot_split(blk, la)
        qi_ref, ki_ref, ko_ref, dec_ref = out_refs[4 * d:4 * d + 4]
        qi_ref[...] = (qf * jnp.exp(bcum)).astype(BF)
        ki_ref[...] = (kf * jnp.exp(-bcum)).astype(BF)
        ko_ref[...] = (kf * jnp.exp(tot - bcum)).astype(BF)
        dec_ref[...] = jnp.exp(tot)


def _gla_prep(pa, lr, wlr, blr):
    hk = GH * GDK
    ospec = pl.BlockSpec((TM, GDK), lambda i, h: (i, h))
    return pl.pallas_call(
        _gla_prep_kernel,
        grid=(NT_ALL, GH),
        in_specs=[
            pl.BlockSpec((TM, 2 * GRANK), lambda i, h: (i, 0)),
            pl.BlockSpec((2, 2 * GRANK, GDK), lambda i, h: (0, 0, h)),
            pl.BlockSpec((2, 1, GDK), lambda i, h: (0, 0, h)),
            pl.BlockSpec((TM, GDK), lambda i, h: (i, C_GQ // GDK + h)),
            pl.BlockSpec((TM, GDK), lambda i, h: (i, C_GK // GDK + h)),
        ],
        out_specs=[ospec] * 8,
        out_shape=[jax.ShapeDtypeStruct((R, hk), BF)] * 3 + [jax.ShapeDtypeStruct((R, hk), F32)]
                  + [jax.ShapeDtypeStruct((R, hk), BF)] * 3 + [jax.ShapeDtypeStruct((R, hk), F32)],
        compiler_params=_cparams(("arbitrary", "arbitrary")),
        name="gla_prep",
    )(lr, wlr, blr, pa, pa)


HP = 2


def _gla_scan_kernel(qf_ref, kif_ref, kof_ref, df_ref, vf_ref, qb_ref, kib_ref, kob_ref, db_ref, vb_ref,
                     of_ref, ob_ref, s_ref):
    @pl.when(pl.program_id(2) == 0)
    def _():
        s_ref[...] = jnp.zeros_like(s_ref)

    ri = lax.broadcasted_iota(jnp.int32, (GC, GC), 0)
    ci = lax.broadcasted_iota(jnp.int32, (GC, GC), 1)
    tn = (((0,), (0,)), ((), ()))
    dirs = ((qf_ref, kif_ref, kof_ref, df_ref, vf_ref, of_ref, ri >= ci, range(GG // GC)),
            (qb_ref, kib_ref, kob_ref, db_ref, vb_ref, ob_ref, ri <= ci, reversed(range(GG // GC))))
    for d, (q_ref, ki_ref, ko_ref, dec_ref, v_ref, o_ref, keep, order) in enumerate(dirs):
        for c in order:
            rows = slice(c * GC, (c + 1) * GC)
            for hh in range(HP):
                kc = slice(hh * GDK, (hh + 1) * GDK)
                vc = slice(hh * GDV, (hh + 1) * GDV)
                q_in = q_ref[rows, kc]
                v = v_ref[rows, vc]
                a = lax.dot_general(q_in, ki_ref[rows, kc], NT_DIMS, preferred_element_type=F32)
                a = jnp.where(keep, a, 0.0).astype(BF)
                st = s_ref[d, hh]
                o = jnp.dot(a, v, preferred_element_type=F32)
                o = o + lax.dot_general(q_in, st.astype(BF), NT_DIMS, preferred_element_type=F32)
                o_ref[rows, vc] = o
                upd = lax.dot_general(v, ko_ref[rows, kc], tn, preferred_element_type=F32)
                s_ref[d, hh] = st * dec_ref[c * GC:c * GC + 1, kc] + upd


def _gla_scan(prep, pa):
    lat0 = TC // GG

    def spec(width, d, col0):
        def index(b, p, g):
            lat = (g - 1) if d == 0 else (NG - 1 - g)
            return (jnp.where(g == 0, b, lat0 + b * (L // GG) + lat), col0 + p)
        return pl.BlockSpec((GG, width), index)

    in_specs = []
    for d in range(2):
        in_specs += [spec(HP * GDK, d, 0)] * 4 + [spec(HP * GDV, d, C_GV // (HP * GDV))]
    out_specs = [spec(HP * GDV, d, 0) for d in range(2)]
    return pl.pallas_call(
        _gla_scan_kernel,
        grid=(B, GH // HP, NG),
        in_specs=in_specs,
        out_specs=out_specs,
        out_shape=[jax.ShapeDtypeStruct((R, GH * GDV), F32)] * 2,
        scratch_shapes=[pltpu.VMEM((2, HP, GDV, GDK), F32)],
        compiler_params=_cparams(("arbitrary", "arbitrary", "arbitrary")),
        name="gla_scan",
    )(*prep[0:4], pa, *prep[4:8], pa)


def _gla_readout_kernel(of_ref, ob_ref, r_ref, gg_ref, o_ref):
    for h in range(GH):
        cols = slice(h * GDV, (h + 1) * GDV)
        o = of_ref[:, cols] + ob_ref[:, cols]
        on = o * lax.rsqrt(jnp.mean(o * o, axis=-1, keepdims=True) + EPS) * gg_ref[...]
        o_ref[:, cols] = (on * _silu(r_ref[:, cols].astype(F32))).astype(BF)


def _gla_readout(o_f, o_b, pa, g_gla):
    spec = pl.BlockSpec((TM, GH * GDV), lambda i: (i, 0))
    return pl.pallas_call(
        _gla_readout_kernel,
        grid=(NT_ALL,),
        in_specs=[spec, spec, pl.BlockSpec((TM, GH * GDV), lambda i: (i, C_GR // (GH * GDV))),
                  pl.BlockSpec((1, GDV), lambda i: (0, 0))],
        out_specs=spec,
        out_shape=jax.ShapeDtypeStruct((R, GH * GDV), BF),
        compiler_params=_cparams(("arbitrary",)),
        name="gla_readout",
    )(o_f, o_b, pa, g_gla.reshape(1, GDV))


AB = 128
NB = L // AB
NCB = LC // AB
NEG = float("-inf")


def _attn_kernel(q_ref, kvp_ref, kvc_ref, kvn_ref, kvx_ref, sink_ref, o_ref):
    j = pl.program_id(1)
    latent = j >= NCB
    ri = lax.broadcasted_iota(jnp.int32, (AB, AB), 0)
    ci = lax.broadcasted_iota(jnp.int32, (AB, AB), 1)
    bias_p = jnp.where(jnp.logical_and(ci >= ri, j > NCB), 0.0, NEG)
    bias_c = jnp.where(latent, 0.0, NEG)
    bias_n = jnp.where(jnp.logical_and(ci <= ri, jnp.logical_and(latent, j < NCB + NB - 1)), 0.0, NEG)
    gsz = HQ // HKV
    for hk in range(HKV):
        ks = slice(hk * HD, (hk + 1) * HD)
        vs = slice(HKV * HD + hk * HD, HKV * HD + (hk + 1) * HD)
        k_all = jnp.concatenate([kvp_ref[:, ks], kvc_ref[:, ks], kvn_ref[:, ks], kvx_ref[:, ks]], axis=0)
        v_all = jnp.concatenate([kvp_ref[:, vs], kvc_ref[:, vs], kvn_ref[:, vs], kvx_ref[:, vs]], axis=0)
        qs = jnp.concatenate([q_ref[:, (hk * gsz + g) * HD:(hk * gsz + g + 1) * HD] for g in range(gsz)], axis=0)
        s_all = lax.dot_general(qs, k_all, NT_DIMS, preferred_element_type=F32)
        ps, dens = [], []
        for g in range(gsz):
            sg = s_all[g * AB:(g + 1) * AB]
            sg = jnp.concatenate([sg[:, 0:AB] + bias_p, sg[:, AB:2 * AB] + bias_c,
                                  sg[:, 2 * AB:3 * AB] + bias_n, sg[:, 3 * AB:]], axis=1)
            snk = sink_ref[hk * gsz + g] * LOG2E
            m = jnp.maximum(jnp.max(sg, axis=-1, keepdims=True), snk)
            e = jnp.exp2(sg - m)
            dens.append(jnp.sum(e, axis=-1, keepdims=True) + jnp.exp2(snk - m))
            ps.append(e.astype(BF))
        o_all = jnp.dot(jnp.concatenate(ps, axis=0), v_all, preferred_element_type=F32)
        for g in range(0, gsz, 2):
            o2 = jnp.concatenate([o_all[g * AB:(g + 1) * AB] / dens[g],
                                  o_all[(g + 1) * AB:(g + 2) * AB] / dens[g + 1]], axis=1)
            o_ref[:, (hk * gsz + g) * HD:(hk * gsz + g + 2) * HD] = o2.astype(BF)


def _attention(pb, pkv, sink):
    lat0 = TC // AB

    def qrow(b, j):
        return jnp.where(j < NCB, b * NCB + j, lat0 + b * NB + j - NCB)

    def krow(off):
        def f(b, j):
            i = jnp.clip(j - NCB + off, 0, NB - 1)
            return (lat0 + b * NB + i, 0)
        return f

    return pl.pallas_call(
        _attn_kernel,
        grid=(B, NCB + NB),
        in_specs=[
            pl.BlockSpec((AB, HQ * HD), lambda b, j: (qrow(b, j), 0)),
            pl.BlockSpec((AB, KV_COLS), krow(-1)),
            pl.BlockSpec((AB, KV_COLS), krow(0)),
            pl.BlockSpec((AB, KV_COLS), krow(1)),
            pl.BlockSpec((LC, KV_COLS), lambda b, j: (b, 0)),
            pl.BlockSpec(memory_space=pltpu.SMEM),
        ],
        out_specs=pl.BlockSpec((AB, HQ * HD), lambda b, j: (qrow(b, j), 0)),
        out_shape=jax.ShapeDtypeStruct((R, HQ * HD), BF),
        compiler_params=_cparams(("arbitrary", "arbitrary")),
        name="attention",
    )(pb, pkv, pkv, pkv, pkv, sink)


MG_TN = 512


def _merge_kernel(gla_ref, att_ref, g1_ref, g2_ref, w1_ref, w2_ref, o_ref, w1b_ref, w2b_ref):
    @pl.when(pl.program_id(1) == 0)
    def _():
        w1b_ref[...] = w1_ref[...].astype(BF)
        w2b_ref[...] = w2_ref[...].astype(BF)

    y1 = jnp.dot(gla_ref[...], w1b_ref[...], preferred_element_type=F32)
    y2 = jnp.dot(att_ref[...], w2b_ref[...], preferred_element_type=F32)
    y = _sigmoid(g1_ref[...].astype(F32)) * y1 + _sigmoid(g2_ref[...].astype(F32)) * y2
    o_ref[...] = y.astype(BF)


def _merge(gla, att, pb, w1, w2, l, t0):
    nt = NT_ALL - t0
    g1c = 2048 // MG_TN
    g2c = 4096 // MG_TN
    return pl.pallas_call(
        _merge_kernel,
        grid=(D // MG_TN, nt),
        in_specs=[
            pl.BlockSpec((TM, GH * GDV), lambda j, i: (i + t0, 0)),
            pl.BlockSpec((TM, HQ * HD), lambda j, i: (i + t0, 0)),
            pl.BlockSpec((TM, MG_TN), lambda j, i: (i + t0, g1c + j)),
            pl.BlockSpec((TM, MG_TN), lambda j, i: (i + t0, g2c + j)),
            pl.BlockSpec((None, GH * GDV, MG_TN), lambda j, i: (l, 0, j)),
            pl.BlockSpec((None, HQ * HD, MG_TN), lambda j, i: (l, 0, j)),
        ],
        out_specs=pl.BlockSpec((TM, MG_TN), lambda j, i: (i, j)),
        out_shape=jax.ShapeDtypeStruct((nt * TM, D), BF),
        scratch_shapes=[pltpu.VMEM((GH * GDV, MG_TN), BF), pltpu.VMEM((HQ * HD, MG_TN), BF)],
        compiler_params=_cparams(("arbitrary", "arbitrary")),
        name="merge",
    )(gla, att, pb, pb, w1, w2)


def _residual_norm(y, x, gate, g_post):
    ms = jnp.mean(y * y, axis=-1, keepdims=True)
    return x + gate * (y * lax.rsqrt(ms + EPS) * g_post)


WO_CK = 512


def _outproj_kernel(a_ref, w_ref, x_ref, gt_ref, gp_ref, gn_ref, sc_ref, sh_ref, o_ref, h_ref, wbf_ref, *, nch):
    s = pl.program_id(0)

    @pl.when(s < nch)
    def _():
        wbf_ref[pl.ds(pl.multiple_of(s * WO_CK, WO_CK), WO_CK), :] = w_ref[...].astype(BF)

    @pl.when(s >= nch - 1)
    def _():
        y = jnp.dot(a_ref[...], wbf_ref[...], preferred_element_type=F32)
        xn = _residual_norm(y, x_ref[...], gt_ref[...], gp_ref[...])
        o_ref[...] = xn
        h_ref[...] = _norm_mod(xn, gn_ref[...], sc_ref[...], sh_ref[...]).astype(BF)


def _outproj(a, w_full, l, kdim, tmr, *, t0, a_ctx, x, modl, k_gate, g_post, g_next, modn, k_sc, k_sh, name):
    nch = kdim // WO_CK
    skip = t0 * TM // tmr
    nt = R // tmr - skip
    a_off = skip if a_ctx else 0

    def tile(s):
        return jnp.maximum(s - (nch - 1), 0)

    def mrow(s):
        first = (tile(s) + skip) * tmr
        return jnp.where(first < TC, 2, (first - TC) // L)

    def mspec(k_chunk):
        return pl.BlockSpec((None, 1, D), lambda s: (mrow(s) * 6 + k_chunk, 0, 0))

    vec = pl.BlockSpec((1, D), lambda s: (0, 0))
    row = pl.BlockSpec((tmr, D), lambda s: (tile(s), 0))
    return pl.pallas_call(
        functools.partial(_outproj_kernel, nch=nch),
        grid=(nt + nch - 1,),
        in_specs=[
            pl.BlockSpec((tmr, kdim), lambda s: (tile(s) + a_off, 0)),
            pl.BlockSpec((None, WO_CK, D), lambda s: (l, jnp.minimum(s, nch - 1), 0)),
            pl.BlockSpec((tmr, D), lambda s: (tile(s) + skip, 0)),
            mspec(k_gate), vec, vec, mspec(k_sc), mspec(k_sh),
        ],
        out_specs=[row, row],
        out_shape=[jax.ShapeDtypeStruct((nt * tmr, D), F32), jax.ShapeDtypeStruct((nt * tmr, D), BF)],
        scratch_shapes=[pltpu.VMEM((kdim, D), BF)],
        compiler_params=_cparams(("arbitrary",)),
        name=name,
    )(a, w_full, x, modl, g_post.reshape(1, D), g_next.reshape(1, D), modn, modn)


FF_TN = 512


def _new_expert(te_ref, i):
    return jnp.logical_or(i == 0, te_ref[i] != te_ref[jnp.maximum(i - 1, 0)])


def _ffn_up_kernel(te_ref, act_ref, a_ref, wg_ref, wu_ref, o_ref, wgb_ref, wub_ref):
    i = pl.program_id(1)

    @pl.when(_new_expert(te_ref, i))
    def _():
        wgb_ref[...] = wg_ref[...].astype(BF)
        wub_ref[...] = wu_ref[...].astype(BF)

    @pl.when(act_ref[i] == 1)
    def _():
        a = a_ref[...]
        yg = jnp.dot(a, wgb_ref[...], preferred_element_type=F32)
        yu = jnp.dot(a, wub_ref[...], preferred_element_type=F32)
        o_ref[...] = (_silu(yg) * yu).astype(BF)

    @pl.when(act_ref[i] == 0)
    def _():
        o_ref[...] = jnp.zeros_like(o_ref)


def _ffn_up(a, wg, wu, lead, te, act):
    nt = a.shape[0] // TM
    wspec = pl.BlockSpec((None, None, D, FF_TN), lambda j, i, te, act: (lead, te[i], 0, j))
    return pl.pallas_call(
        _ffn_up_kernel,
        grid_spec=pltpu.PrefetchScalarGridSpec(
            num_scalar_prefetch=2,
            grid=(DFF // FF_TN, nt),
            in_specs=[pl.BlockSpec((TM, D), lambda j, i, te, act: (i, 0)), wspec, wspec],
            out_specs=pl.BlockSpec((TM, FF_TN), lambda j, i, te, act: (i, j)),
            scratch_shapes=[pltpu.VMEM((D, FF_TN), BF), pltpu.VMEM((D, FF_TN), BF)],
        ),
        out_shape=jax.ShapeDtypeStruct((nt * TM, DFF), BF),
        compiler_params=_cparams(("arbitrary", "arbitrary")),
        name="ffn_up",
    )(te, act, a, wg, wu)


DN_TN = 512


def _moe_down_kernel(te_ref, act_ref, a_ref, w_ref, s_ref, o_ref, wb_ref):
    i = pl.program_id(1)

    @pl.when(_new_expert(te_ref, i))
    def _():
        wb_ref[...] = w_ref[...].astype(BF)

    @pl.when(act_ref[i] == 1)
    def _():
        y = jnp.dot(a_ref[...], wb_ref[...], preferred_element_type=F32)
        o_ref[...] = (y * s_ref[...]).astype(BF)

    @pl.when(act_ref[i] == 0)
    def _():
        o_ref[...] = jnp.zeros_like(o_ref)


def _moe_down(u, wd, lead, wrow, te, act):
    nt = u.shape[0] // TM
    return pl.pallas_call(
        _moe_down_kernel,
        grid_spec=pltpu.PrefetchScalarGridSpec(
            num_scalar_prefetch=2,
            grid=(D // DN_TN, nt),
            in_specs=[
                pl.BlockSpec((TM, DFF), lambda j, i, te, act: (i, 0)),
                pl.BlockSpec((None, None, DFF, DN_TN), lambda j, i, te, act: (lead, te[i], 0, j)),
                pl.BlockSpec((TM, 1), lambda j, i, te, act: (i, 0)),
            ],
            out_specs=pl.BlockSpec((TM, DN_TN), lambda j, i, te, act: (i, j)),
            scratch_shapes=[pltpu.VMEM((DFF, DN_TN), BF)],
        ),
        out_shape=jax.ShapeDtypeStruct((nt * TM, D), BF),
        compiler_params=_cparams(("arbitrary", "arbitrary")),
        name="moe_down",
    )(te, act, u, wd, wrow)


def _router_kernel(x_ref, g_ref, sc_ref, sh_ref, wr_ref, br_ref, o_ref, pos_ref, sel_ref, carry_ref):
    @pl.when(pl.program_id(0) == 0)
    def _():
        carry_ref[...] = jnp.zeros_like(carry_ref)

    h = _norm_mod(x_ref[...], g_ref[...], sc_ref[...], sh_ref[...])
    hh, hl = _split(h)
    wh, wl = _split(wr_ref[...])
    nt = (((1,), (1,)), ((), ()))
    lg = (lax.dot_general(wh, hh, nt, preferred_element_type=F32)
          + lax.dot_general(wh, hl, nt, preferred_element_type=F32)
          + lax.dot_general(wl, hh, nt, preferred_element_type=F32) + br_ref[...])
    idx = lax.broadcasted_iota(jnp.int32, lg.shape, 0)
    m1 = jnp.max(lg, axis=0, keepdims=True)
    i1 = jnp.min(jnp.where(lg == m1, idx, NE), axis=0, keepdims=True)
    l2 = jnp.where(idx == i1, NEG, lg)
    m2 = jnp.max(l2, axis=0, keepdims=True)
    i2 = jnp.min(jnp.where(l2 == m2, idx, NE), axis=0, keepdims=True)
    e2 = jnp.exp(m2 - m1)
    w1 = 1.0 / (1.0 + e2)
    w2 = e2 / (1.0 + e2)
    o_ref[...] = jnp.where(idx == i1, w1, 0.0) + jnp.where(idx == i2, w2, 0.0)
    sel = jnp.where(idx == i1, 1.0, jnp.where(idx == i2, 1.0, 0.0))
    si = lax.broadcasted_iota(jnp.int32, (TM, TM), 0)
    ti = lax.broadcasted_iota(jnp.int32, (TM, TM), 1)
    before = jnp.where(si < ti, 1.0, 0.0).astype(BF)
    excl = jnp.dot(sel.astype(BF), before, preferred_element_type=F32)
    pos_ref[...] = (excl + carry_ref[...]).astype(jnp.int32)
    sel_ref[...] = sel.astype(jnp.int32)
    carry_ref[...] += jnp.sum(sel, axis=1, keepdims=True)


def _router(x_lat, g, modl, k_sc, k_sh, w_router, b_router):
    ospec = pl.BlockSpec((NE, TM), lambda i: (0, i))
    return pl.pallas_call(
        _router_kernel,
        grid=(T // TM,),
        in_specs=[
            pl.BlockSpec((TM, D), lambda i: (i, 0)),
            pl.BlockSpec((1, D), lambda i: (0, 0)),
            pl.BlockSpec((None, 1, D), lambda i: (_mrow(i + 1) * 6 + k_sc, 0, 0)),
            pl.BlockSpec((None, 1, D), lambda i: (_mrow(i + 1) * 6 + k_sh, 0, 0)),
            pl.BlockSpec((NE, D), lambda i: (0, 0)),
            pl.BlockSpec((NE, 1), lambda i: (0, 0)),
        ],
        out_specs=[ospec, ospec, ospec],
        out_shape=[jax.ShapeDtypeStruct((NE, T), F32), jax.ShapeDtypeStruct((NE, T), jnp.int32),
                   jax.ShapeDtypeStruct((NE, T), jnp.int32)],
        scratch_shapes=[pltpu.VMEM((NE, 1), F32)],
        compiler_params=_cparams(("arbitrary",)),
        name="router",
    )(x_lat, g.reshape(1, D), modl, modl, w_router.T, b_router.reshape(NE, 1))


NP = 2 * T // TM + NE
DT = 256
NDT = NP * TM // DT
WSTEP = 1408
WTOK = WSTEP + 128
NITEM = 136
CT = 256
NCT = T // CT
ROW_ALIGN = 16
WIN = CT + ROW_ALIGN
WBUF = CT + 128


def _route_tables(pos, sel):
    i32 = jnp.int32
    counts = pos[:, -1] + sel[:, -1]
    ntile = (counts + TM - 1) // TM
    tend = jnp.cumsum(ntile)
    seg = (tend - ntile) * TM
    dest = jnp.where(sel > 0, seg[:, None] + pos, -1)
    d_hi = jnp.max(dest, axis=0)
    d_lo = jnp.sum(dest, axis=0) + (NE - 2) - d_hi
    tiles = jnp.arange(NP, dtype=i32)
    total = tend[-1]
    te = jnp.sum((tiles[:, None] >= tend[None, :]).astype(i32), axis=1)
    te_last = jnp.sum(((total - 1) >= tend).astype(i32))
    act = (tiles < total).astype(i32)
    te = jnp.where(act > 0, te, te_last)
    dtile = jnp.arange(NDT, dtype=i32)
    de = te[dtile // (TM // DT)]
    p_lo = dtile * DT - seg[de]
    has = jnp.logical_and(dtile * DT < total * TM, p_lo < counts[de])
    p_end = jnp.minimum(p_lo + DT, counts[de])
    incl = jnp.take(pos + sel, de, axis=0)
    t_first = jnp.sum((incl <= p_lo[:, None]).astype(i32), axis=1)
    t_last = jnp.sum((incl <= (p_end - 1)[:, None]).astype(i32), axis=1)
    w0 = (t_first // 128) * 128
    nw = jnp.where(has, (t_last + 1 - w0 + WSTEP - 1) // WSTEP, 1)
    cum = jnp.cumsum(nw)
    item = jnp.minimum(jnp.arange(NITEM, dtype=i32), cum[-1] - 1)
    real = jnp.arange(NITEM, dtype=i32) < cum[-1]
    it_tile = jnp.sum((cum[None, :] <= item[:, None]).astype(i32), axis=1)
    k = item - (cum - nw)[it_tile]
    lo = w0[it_tile] + k * WSTEP
    disp = dict(tile=it_tile, expert=de[it_tile], lo=lo,
                tok=jnp.minimum(lo, T - WTOK) // 128,
                valid=jnp.logical_and(real, has[it_tile]).astype(i32),
                first=jnp.logical_and(real, k == 0).astype(i32))
    pbc = pos[:, ::CT]
    r0c = (seg[:, None] + pbc).T.reshape(NCT * NE)
    cntc = (jnp.concatenate([pbc[:, 1:], counts[:, None]], axis=1) - pbc).T.reshape(NCT * NE)
    a0c = jnp.minimum((r0c // ROW_ALIGN) * ROW_ALIGN, NP * TM - WIN)
    comb = dict(a0=a0c, r0=r0c, cnt=cntc)
    return dict(dest=dest, d_hi=d_hi.reshape(T, 1), d_lo=d_lo.reshape(T, 1), te=te, act=act,
                disp=disp, comb=comb)


def _dispatch_kernel(tile_ref, exp_ref, lo_ref, tok_ref, val_ref, first_ref, h_ref, dest_ref, cw_ref,
                     xs_ref, wr_ref):
    i = pl.program_id(0)

    @pl.when(first_ref[i] == 1)
    def _():
        xs_ref[...] = jnp.zeros_like(xs_ref)
        wr_ref[...] = jnp.zeros_like(wr_ref)

    @pl.when(val_ref[i] == 1)
    def _():
        e = exp_ref[i]
        lo = lo_ref[i]
        tok = lax.broadcasted_iota(jnp.int32, (1, WTOK), 1) + tok_ref[i] * 128
        mine = jnp.logical_and(tok >= lo, tok < lo + WSTEP)
        drow = jnp.where(mine, dest_ref[pl.ds(e, 1), :], -1)
        crow = cw_ref[pl.ds(e, 1), :]
        hit = drow == lax.broadcasted_iota(jnp.int32, (DT, WTOK), 0) + tile_ref[i] * DT
        onehot = jnp.where(hit, 1.0, 0.0).astype(BF)
        g = jnp.dot(onehot, h_ref[...], preferred_element_type=F32)
        xs_ref[...] = (xs_ref[...].astype(F32) + g).astype(BF)
        wr_ref[...] += jnp.sum(jnp.where(hit, crow, 0.0), axis=1, keepdims=True)


def _dispatch(h_lat, dest, comb_w, tb):
    def omap(i, tile, *_):
        return (tile[i], 0)

    def rows(i, tile, exp, lo, tok, *_):
        return (tok[i] * 128, 0)

    def lanes(i, tile, exp, lo, tok, *_):
        return (0, tok[i] * 128)

    return pl.pallas_call(
        _dispatch_kernel,
        grid_spec=pltpu.PrefetchScalarGridSpec(
            num_scalar_prefetch=6,
            grid=(NITEM,),
            in_specs=[
                pl.BlockSpec((pl.Element(WTOK), pl.Element(D)), rows),
                pl.BlockSpec((pl.Element(NE), pl.Element(WTOK)), lanes),
                pl.BlockSpec((pl.Element(NE), pl.Element(WTOK)), lanes),
            ],
            out_specs=[pl.BlockSpec((DT, D), omap), pl.BlockSpec((DT, 1), omap)],
        ),
        out_shape=[jax.ShapeDtypeStruct((NP * TM, D), BF), jax.ShapeDtypeStruct((NP * TM, 1), F32)],
        compiler_params=_cparams(("arbitrary",)),
        name="moe_dispatch",
    )(tb["tile"], tb["expert"], tb["lo"], tb["tok"], tb["valid"], tb["first"], h_lat, dest, comb_w)


def _combine_kernel(a0_ref, r0_ref, cnt_ref, y_hbm, dhi_ref, dlo_ref, x_ref, gt_ref, gp_ref, o_ref,
                    ybuf, acc_ref, sem):
    i = pl.program_id(0)
    slot = lax.rem(i, 2)

    def window_copies(tile, buf):
        out = []
        for e in range(NE):
            a0 = pl.multiple_of(a0_ref[tile * NE + e], ROW_ALIGN)
            out.append(pltpu.make_async_copy(y_hbm.at[pl.ds(a0, WIN), :], ybuf.at[buf, e, pl.ds(0, WIN), :],
                                             sem.at[buf, e]))
        return out

    @pl.when(i == 0)
    def _():
        ybuf[:, :, WIN:, :] = jnp.zeros((2, NE, WBUF - WIN, D), BF)
        for cp in window_copies(0, 0):
            cp.start()

    @pl.when(i + 1 < pl.num_programs(0))
    def _():
        for cp in window_copies(i + 1, 1 - slot):
            cp.start()

    for cp in window_copies(i, slot):
        cp.wait()

    dhi = dhi_ref[...]
    dlo = dlo_ref[...]

    def onehot(e, c0, width):
        idx = i * NE + e
        r0 = r0_ref[idx]
        ids = lax.broadcasted_iota(jnp.int32, (1, width), 1) + (a0_ref[idx] + c0)
        ids = jnp.where(jnp.logical_and(ids >= r0, ids < r0 + cnt_ref[idx]), ids, -1)
        return jnp.where(dhi == ids, 1.0, jnp.where(dlo == ids, 1.0, 0.0)).astype(BF)

    acc = jnp.zeros((CT, D), F32)
    for e in range(NE):
        acc = acc + jnp.dot(onehot(e, 0, CT), ybuf[slot, e, 0:CT, :], preferred_element_type=F32)
    acc_ref[...] = acc
    for e in range(NE):
        idx = i * NE + e

        @pl.when(r0_ref[idx] + cnt_ref[idx] > a0_ref[idx] + CT)
        def _():
            acc_ref[...] += jnp.dot(onehot(e, CT, WBUF - CT), ybuf[slot, e, CT:, :], preferred_element_type=F32)

    o_ref[...] = _residual_norm(acc_ref[...], x_ref[...], gt_ref[...], gp_ref[...])


def _combine(y, d_hi, d_lo, x_lat, modl, k_gate, g_post, tb):
    return pl.pallas_call(
        _combine_kernel,
        grid_spec=pltpu.PrefetchScalarGridSpec(
            num_scalar_prefetch=3,
            grid=(NCT,),
            in_specs=[
                pl.BlockSpec(memory_space=pl.ANY),
                pl.BlockSpec((CT, 1), lambda t, *_: (t, 0)),
                pl.BlockSpec((CT, 1), lambda t, *_: (t, 0)),
                pl.BlockSpec((CT, D), lambda t, *_: (t, 0)),
                pl.BlockSpec((None, 1, D), lambda t, *_: (_mrow(t // (TM // CT) + 1) * 6 + k_gate, 0, 0)),
                pl.BlockSpec((1, D), lambda t, *_: (0, 0)),
            ],
            out_specs=pl.BlockSpec((CT, D), lambda t, *_: (t, 0)),
            scratch_shapes=[pltpu.VMEM((2, NE, WBUF, D), BF), pltpu.VMEM((CT, D), F32),
                            pltpu.SemaphoreType.DMA((2, NE))],
        ),
        out_shape=jax.ShapeDtypeStruct((T, D), F32),
        compiler_params=_cparams(("arbitrary",)),
        name="moe_combine",
    )(tb["a0"], tb["r0"], tb["cnt"], y, d_hi, d_lo, x_lat, modl, g_post.reshape(1, D))


def kernel(x, c, ctx, c_ctx, w_mod, b_mod, g_pre_mix, g_post_mix, g_pre_ffn, g_post_ffn,
           w_in, w_lr_f, b_lr_f, w_lr_b, b_lr_b, g_gla, att_sink, w_br_gla, w_br_att, w_out,
           w_ffn_gate, w_ffn_up, w_ffn_down, w_router, b_router, w_exp_gate, w_exp_up, w_exp_down):
    cvec = jnp.zeros((8, D), F32).at[0:B].set(c).at[B].set(c_ctx)
    mod = _modulation(cvec, w_mod, b_mod).reshape(DEPTH, 8 * 6, 1, D)
    xa = jnp.concatenate([ctx.reshape(TC, D), x.reshape(T, D)], axis=0)
    cos, sin_a, sin_b = _rope_tables()
    wt = jnp.swapaxes(w_in, 1, 2)

    h = _prenorm(xa, g_pre_mix[0], mod[0], 1, 0)
    for l in range(DEPTH):
        last = l == DEPTH - 1
        t0 = 1 if last else 0
        modl = mod[l]
        pa = _proj(h, wt, l, 0, C_LF, 1024, BF, "proj_gla")
        pb, pkv = _proj_att(h, wt, l, (cos, sin_a, sin_b))
        lr = _proj(h, wt, l, C_LF, 2 * GRANK, 2 * GRANK, F32, "proj_decay")

        wlr = jnp.zeros((2, 2 * GRANK, GH * GDK), F32)
        wlr = wlr.at[0, 0:GRANK].set(w_lr_f[l]).at[1, GRANK:2 * GRANK].set(w_lr_b[l])
        blr = jnp.stack([b_lr_f[l], b_lr_b[l]]).reshape(2, 1, GH * GDK)
        o_f, o_b = _gla_scan(_gla_prep(pa, lr, wlr, blr), pa)
        gla = _gla_readout(o_f, o_b, pa, g_gla[l])
        att = _attention(pb, pkv, att_sink[l])
        mm = _merge(gla, att, pb, w_br_gla, w_br_att, l, t0)

        xa, h2 = _outproj(mm, w_out, l, D, TM, t0=t0, a_ctx=False, x=xa, modl=modl, k_gate=2, name="outproj",
                          g_post=g_post_mix[l], g_next=g_pre_ffn[l], modn=modl, k_sc=4, k_sh=3)
        if not last:
            ones = jnp.ones((NT_ALL,), jnp.int32)
            u = _ffn_up(h2, w_ffn_gate[:, None], w_ffn_up[:, None], l // 2, 0 * ones, ones)
            xa, h = _outproj(u, w_ffn_down, l // 2, DFF, TM // 2, t0=0, a_ctx=True, x=xa, modl=modl, k_gate=5,
                             name="ffn_down", g_post=g_post_ffn[l], g_next=g_pre_mix[l + 1], modn=mod[l + 1],
                             k_sc=1, k_sh=0)
        else:
            comb_w, pos, sel = _router(xa, g_pre_ffn[l], modl, 4, 3, w_router[l // 2], b_router[l // 2])
            rt = _route_tables(pos, sel)
            xs, wrow = _dispatch(h2, rt["dest"], comb_w, rt["disp"])
            u = _ffn_up(xs, w_exp_gate, w_exp_up, l // 2, rt["te"], rt["act"])
            y = _moe_down(u, w_exp_down, l // 2, wrow, rt["te"], rt["act"])
            xa = _combine(y, rt["d_hi"], rt["d_lo"], xa, modl, 5, g_post_ffn[l], rt["comb"])
    return xa.reshape(B, L, D)
```

```python
import functools

import jax
import jax.numpy as jnp
import numpy as np
from jax import lax
from jax.experimental import pallas as pl
from jax.experimental.pallas import tpu as pltpu

BF = jnp.bfloat16
F32 = jnp.float32

D = 2048
B = 2
L = 4096
LC = 256
T = B * L
TC = B * LC
R = TC + T
DEPTH = 2
GRID_W = 64
EPS = 1e-6

GH = 4
GDK = 256
GDV = 512
GRANK = 16
GTAU = 16.0
GC = 64

HD = 64
HQ = 32
HKV = 4
ROPE_BASE = 10000.0

DFF = 5632
NE = 8

TM = 512
NT_ALL = R // TM
TPB = L // TM
VMEM_LIMIT = 56 * 1024 * 1024

C_GQ, C_GK, C_GV, C_GR, C_LF, C_LB, C_AQ, C_AK, C_AV, C_G1, C_G2, C_END = (
    0, 1024, 2048, 4096, 6144, 6160, 6176, 8224, 8480, 8736, 10784, 12832)


def _cparams(sem):
    return pltpu.CompilerParams(dimension_semantics=sem, vmem_limit_bytes=VMEM_LIMIT)


def _mrow(gi):
    return jnp.where(gi == 0, 2, (gi - 1) // TPB)


def _sigmoid(z):
    return 1.0 / (1.0 + jnp.exp(-z))


def _silu(z):
    return z * _sigmoid(z)


def _split(a):
    hi = a.astype(BF)
    lo = (a - hi.astype(F32)).astype(BF)
    return hi, lo


def _mod_kernel(c_ref, w_ref, b_ref, o_ref):
    a = _silu(c_ref[...]).astype(BF)
    o_ref[...] = jnp.dot(a, w_ref[...].astype(BF), preferred_element_type=F32) + b_ref[...]


def _modulation(cvec, w_mod, b_mod):
    tn = 1024
    return pl.pallas_call(
        _mod_kernel,
        grid=(DEPTH, 6 * D // tn),
        in_specs=[
            pl.BlockSpec((8, D), lambda l, j: (0, 0)),
            pl.BlockSpec((None, D, tn), lambda l, j: (l, 0, j)),
            pl.BlockSpec((None, 1, tn), lambda l, j: (l, 0, j)),
        ],
        out_specs=pl.BlockSpec((None, 8, tn), lambda l, j: (l, 0, j)),
        out_shape=jax.ShapeDtypeStruct((DEPTH, 8, 6 * D), F32),
        compiler_params=_cparams(("arbitrary", "arbitrary")),
        name="modulation",
    )(cvec, w_mod, b_mod.reshape(DEPTH, 1, 6 * D))


def _norm_mod(x, g, sc, sh):
    ms = jnp.mean(x * x, axis=-1, keepdims=True)
    return (x * lax.rsqrt(ms + EPS) * g) * (1.0 + sc) + sh


def _prenorm_kernel(ctx_ref, x_ref, g_ref, sc_ref, sh_ref, xa_ref, o_ref):
    def emit(src_ref):
        v = src_ref[...]
        xa_ref[...] = v
        o_ref[...] = _norm_mod(v, g_ref[...], sc_ref[...], sh_ref[...]).astype(BF)

    @pl.when(pl.program_id(0) == 0)
    def _():
        emit(ctx_ref)

    @pl.when(pl.program_id(0) > 0)
    def _():
        emit(x_ref)


def _prenorm(ctx2d, x2d, g, modl, k_sc, k_sh):
    row = pl.BlockSpec((TM, D), lambda i: (i, 0))
    return pl.pallas_call(
        _prenorm_kernel,
        grid=(NT_ALL,),
        in_specs=[
            pl.BlockSpec((TM, D), lambda i: (0, 0)),
            pl.BlockSpec((TM, D), lambda i: (jnp.maximum(i - 1, 0), 0)),
            pl.BlockSpec((1, D), lambda i: (0, 0)),
            pl.BlockSpec((None, 1, D), lambda i: (_mrow(i) * 6 + k_sc, 0, 0)),
            pl.BlockSpec((None, 1, D), lambda i: (_mrow(i) * 6 + k_sh, 0, 0)),
        ],
        out_specs=[row, row],
        out_shape=[jax.ShapeDtypeStruct((R, D), F32), jax.ShapeDtypeStruct((R, D), BF)],
        compiler_params=_cparams(("arbitrary",)),
        name="prenorm",
    )(ctx2d, x2d, g.reshape(1, D), modl, modl)


NT_DIMS = (((1,), (1,)), ((), ()))


def _proj_kernel(a_ref, wt_ref, o_ref, wbf_ref):
    @pl.when(pl.program_id(1) == 0)
    def _():
        wbf_ref[...] = wt_ref[...].astype(BF)

    acc = lax.dot_general(a_ref[...], wbf_ref[...], NT_DIMS, preferred_element_type=F32)
    o_ref[...] = acc.astype(o_ref.dtype)


def _proj(h, wt, l, row0, n_cols, tn, out_dtype, name):
    return pl.pallas_call(
        _proj_kernel,
        grid=(n_cols // tn, NT_ALL),
        in_specs=[
            pl.BlockSpec((TM, D), lambda j, i: (i, 0)),
            pl.BlockSpec((None, tn, D), lambda j, i: (l, row0 // tn + j, 0)),
        ],
        out_specs=pl.BlockSpec((TM, tn), lambda j, i: (i, j)),
        out_shape=jax.ShapeDtypeStruct((R, n_cols), out_dtype),
        scratch_shapes=[pltpu.VMEM((tn, D), BF)],
        compiler_params=_cparams(("arbitrary", "arbitrary")),
        name=name,
    )(h, wt)


PB_TN = 1024
PB_COLS = 2048 + 4096
PB_QT = 2048 // PB_TN
KV_COLS = 2 * HKV * HD
LOG2E = 1.4426950408889634
Q_SCALE = HD ** -0.5 * LOG2E


def _rope(x, cos, sin_a, sin_b):
    return x * cos + pltpu.roll(x, 112, axis=1) * sin_a + pltpu.roll(x, 16, axis=1) * sin_b


def _proj_att_kernel(a_ref, wt_ref, cos_ref, sa_ref, sb_ref, o_ref, wbf_ref, *, n_q, rope_cols):
    j = pl.program_id(0)
    i = pl.program_id(1)

    @pl.when(i == 0)
    def _():
        wbf_ref[...] = wt_ref[0].astype(BF)

    acc = lax.dot_general(a_ref[...], wbf_ref[...], NT_DIMS, preferred_element_type=F32)
    tn = acc.shape[1]
    latent = i > 0
    is_q = j < n_q

    def rotated(cols, scale):
        for s in range(cols // 128):
            y = _rope(acc[:, s * 128:(s + 1) * 128], cos_ref[...], sa_ref[...], sb_ref[...])
            o_ref[:, s * 128:(s + 1) * 128] = (y * scale).astype(BF)

    @pl.when(jnp.logical_and(is_q, latent))
    def _():
        rotated(tn, Q_SCALE)

    @pl.when(jnp.logical_and(is_q, jnp.logical_not(latent)))
    def _():
        o_ref[...] = (acc * Q_SCALE).astype(BF)

    plain = jnp.logical_not(is_q)
    if rope_cols:
        plain = jnp.logical_and(plain, jnp.logical_not(latent))

        @pl.when(jnp.logical_and(jnp.logical_not(is_q), latent))
        def _():
            rotated(rope_cols, 1.0)
            o_ref[:, rope_cols:] = acc[:, rope_cols:].astype(BF)

    @pl.when(plain)
    def _():
        o_ref[...] = acc.astype(BF)


def _proj_att_call(h, wt, l, tabs, *, tn, n_tiles, n_q, rope_cols, wrow_units, name):
    def tab(j, i):
        return (jnp.where(i == 0, 0, (i - 1) % TPB), 0)

    return pl.pallas_call(
        functools.partial(_proj_att_kernel, n_q=n_q, rope_cols=rope_cols),
        grid=(n_tiles, NT_ALL),
        in_specs=[
            pl.BlockSpec((TM, D), lambda j, i: (i, 0)),
            pl.BlockSpec((pl.Element(1), pl.Element(tn), pl.Element(D)),
                         lambda j, i: (l, wrow_units(j) * (2 * GRANK), 0)),
            pl.BlockSpec((TM, 128), tab),
            pl.BlockSpec((TM, 128), tab),
            pl.BlockSpec((TM, 128), tab),
        ],
        out_specs=pl.BlockSpec((TM, tn), lambda j, i: (i, j)),
        out_shape=jax.ShapeDtypeStruct((R, n_tiles * tn), BF),
        scratch_shapes=[pltpu.VMEM((tn, D), BF)],
        compiler_params=_cparams(("arbitrary", "arbitrary")),
        name=name,
    )(h, wt, *tabs)


def _proj_att(h, wt, l, tabs):
    u = 2 * GRANK
    pb = _proj_att_call(
        h, wt, l, tabs, tn=PB_TN, n_tiles=PB_COLS // PB_TN, n_q=PB_QT, rope_cols=0, name="proj_att",
        wrow_units=lambda j: jnp.where(j < PB_QT, C_AQ // u + j * (PB_TN // u),
                                       C_G1 // u + (j - PB_QT) * (PB_TN // u)))
    pkv = _proj_att_call(h, wt, l, tabs, tn=KV_COLS, n_tiles=1, n_q=0, rope_cols=HKV * HD, name="proj_kv",
                         wrow_units=lambda j: C_AK // u + j)
    return pb, pkv


def _rope_tables():
    rows = L // GRID_W
    row = np.repeat(np.arange(rows), GRID_W)
    col = np.tile(np.arange(GRID_W), rows)
    half = HD // 2
    inv = (ROPE_BASE ** (-np.arange(0, half, 2, dtype=np.float32) / half)).astype(np.float32)

    def angles(p):
        a = p.astype(np.float32)[:, None] * inv[None, :]
        return np.concatenate([a, a], axis=-1)

    ang = np.concatenate([angles(row), angles(col)], axis=-1)
    ang = np.concatenate([ang, ang], axis=-1)
    cos, sin = np.cos(ang).astype(np.float32), np.sin(ang).astype(np.float32)
    first = (np.arange(128) % 32) < 16
    zero = np.float32(0.0)
    return jnp.asarray(cos), jnp.asarray(np.where(first, -sin, zero)), jnp.asarray(np.where(first, zero, sin))


GG = 256
NG = 1 + L // GG


def _dot_split(m01, a):
    hi, lo = _split(a)
    return jnp.dot(m01, hi, preferred_element_type=F32) + jnp.dot(m01, lo, preferred_element_type=F32)


def _chunk_masks():
    r = np.arange(TM)[:, None]
    c = np.arange(TM)[None, :]
    same = (r // GC) == (c // GC)
    return jnp.asarray(np.stack([same & (r >= c), same & (r <= c), same]).astype(np.float32), dtype=BF)


def _gla_prep_kernel(lr_ref, wlr_ref, blr_ref, q_ref, k_ref, mask_ref, *out_refs):
    lh, ll = _split(lr_ref[...])
    qf = q_ref[...].astype(F32) * (GDK ** -0.5)
    kf = k_ref[...].astype(F32)
    for d in range(2):
        wh, wl = _split(wlr_ref[d])
        z = (jnp.dot(lh, wh, preferred_element_type=F32) + jnp.dot(ll, wh, preferred_element_type=F32)
             + jnp.dot(lh, wl, preferred_element_type=F32) + blr_ref[d])
        la = (jnp.minimum(z, 0.0) - jnp.log1p(jnp.exp(-jnp.abs(z)))) * (1.0 / GTAU)
        bcum = _dot_split(mask_ref[d], la)
        tot = _dot_split(mask_ref[2], la)
        qi_ref, ki_ref, ko_ref, dec_ref = out_refs[4 * d:4 * d + 4]
        qi_ref[...] = (qf * jnp.exp(bcum)).astype(BF)
        ki_ref[...] = (kf * jnp.exp(-bcum)).astype(BF)
        ko_ref[...] = (kf * jnp.exp(tot - bcum)).astype(BF)
        dec_ref[...] = jnp.exp(tot)


def _gla_prep(pa, lr, wlr, blr):
    hk = GH * GDK
    ospec = pl.BlockSpec((TM, GDK), lambda i, h: (i, h))
    return pl.pallas_call(
        _gla_prep_kernel,
        grid=(NT_ALL, GH),
        in_specs=[
            pl.BlockSpec((TM, 2 * GRANK), lambda i, h: (i, 0)),
            pl.BlockSpec((2, 2 * GRANK, GDK), lambda i, h: (0, 0, h)),
            pl.BlockSpec((2, 1, GDK), lambda i, h: (0, 0, h)),
            pl.BlockSpec((TM, GDK), lambda i, h: (i, C_GQ // GDK + h)),
            pl.BlockSpec((TM, GDK), lambda i, h: (i, C_GK // GDK + h)),
            pl.BlockSpec((3, TM, TM), lambda i, h: (0, 0, 0)),
        ],
        out_specs=[ospec] * 8,
        out_shape=[jax.ShapeDtypeStruct((R, hk), BF)] * 3 + [jax.ShapeDtypeStruct((R, hk), F32)]
                  + [jax.ShapeDtypeStruct((R, hk), BF)] * 3 + [jax.ShapeDtypeStruct((R, hk), F32)],
        compiler_params=_cparams(("arbitrary", "arbitrary")),
        name="gla_prep",
    )(lr, wlr, blr, pa, pa, _chunk_masks())


HP = 2


def _gla_scan_kernel(qf_ref, kif_ref, kof_ref, df_ref, vf_ref, qb_ref, kib_ref, kob_ref, db_ref, vb_ref,
                     of_ref, ob_ref, s_ref):
    @pl.when(pl.program_id(2) == 0)
    def _():
        s_ref[...] = jnp.zeros_like(s_ref)

    ri = lax.broadcasted_iota(jnp.int32, (GC, GC), 0)
    ci = lax.broadcasted_iota(jnp.int32, (GC, GC), 1)
    tn = (((0,), (0,)), ((), ()))
    dirs = ((qf_ref, kif_ref, kof_ref, df_ref, vf_ref, of_ref, ri >= ci, range(GG // GC)),
            (qb_ref, kib_ref, kob_ref, db_ref, vb_ref, ob_ref, ri <= ci, reversed(range(GG // GC))))
    for d, (q_ref, ki_ref, ko_ref, dec_ref, v_ref, o_ref, keep, order) in enumerate(dirs):
        for c in order:
            rows = slice(c * GC, (c + 1) * GC)
            for hh in range(HP):
                kc = slice(hh * GDK, (hh + 1) * GDK)
                vc = slice(hh * GDV, (hh + 1) * GDV)
                q_in = q_ref[rows, kc]
                v = v_ref[rows, vc]
                a = lax.dot_general(q_in, ki_ref[rows, kc], NT_DIMS, preferred_element_type=F32)
                a = jnp.where(keep, a, 0.0).astype(BF)
                st = s_ref[d, hh]
                o = jnp.dot(a, v, preferred_element_type=F32)
                o = o + lax.dot_general(q_in, st.astype(BF), NT_DIMS, preferred_element_type=F32)
                o_ref[rows, vc] = o
                upd = lax.dot_general(v, ko_ref[rows, kc], tn, preferred_element_type=F32)
                s_ref[d, hh] = st * dec_ref[c * GC:c * GC + 1, kc] + upd


def _gla_scan(prep, pa):
    lat0 = TC // GG

    def spec(width, d, col0):
        def index(b, p, g):
            lat = (g - 1) if d == 0 else (NG - 1 - g)
            return (jnp.where(g == 0, b, lat0 + b * (L // GG) + lat), col0 + p)
        return pl.BlockSpec((GG, width), index)

    in_specs = []
    for d in range(2):
        in_specs += [spec(HP * GDK, d, 0)] * 4 + [spec(HP * GDV, d, C_GV // (HP * GDV))]
    out_specs = [spec(HP * GDV, d, 0) for d in range(2)]
    return pl.pallas_call(
        _gla_scan_kernel,
        grid=(B, GH // HP, NG),
        in_specs=in_specs,
        out_specs=out_specs,
        out_shape=[jax.ShapeDtypeStruct((R, GH * GDV), F32)] * 2,
        scratch_shapes=[pltpu.VMEM((2, HP, GDV, GDK), F32)],
        compiler_params=_cparams(("arbitrary", "arbitrary", "arbitrary")),
        name="gla_scan",
    )(*prep[0:4], pa, *prep[4:8], pa)


def _gla_readout_kernel(of_ref, ob_ref, r_ref, gg_ref, o_ref):
    for h in range(GH):
        cols = slice(h * GDV, (h + 1) * GDV)
        o = of_ref[:, cols] + ob_ref[:, cols]
        on = o * lax.rsqrt(jnp.mean(o * o, axis=-1, keepdims=True) + EPS) * gg_ref[...]
        o_ref[:, cols] = (on * _silu(r_ref[:, cols].astype(F32))).astype(BF)


def _gla_readout(o_f, o_b, pa, g_gla):
    spec = pl.BlockSpec((TM, GH * GDV), lambda i: (i, 0))
    return pl.pallas_call(
        _gla_readout_kernel,
        grid=(NT_ALL,),
        in_specs=[spec, spec, pl.BlockSpec((TM, GH * GDV), lambda i: (i, C_GR // (GH * GDV))),
                  pl.BlockSpec((1, GDV), lambda i: (0, 0))],
        out_specs=spec,
        out_shape=jax.ShapeDtypeStruct((R, GH * GDV), BF),
        compiler_params=_cparams(("arbitrary",)),
        name="gla_readout",
    )(o_f, o_b, pa, g_gla.reshape(1, GDV))


AB = 128
NB = L // AB
NCB = LC // AB
NEG = float("-inf")
QSTACK = 8


def _attn_kernel(q_ref, kvp_ref, kvc_ref, kvn_ref, kvx_ref, sink_ref, o_ref):
    j = pl.program_id(1)
    latent = j >= NCB
    ri = lax.broadcasted_iota(jnp.int32, (AB, AB), 0)
    ci = lax.broadcasted_iota(jnp.int32, (AB, AB), 1)
    bias_p = jnp.where(jnp.logical_and(ci >= ri, j > NCB), 0.0, NEG)
    bias_c = jnp.where(latent, 0.0, NEG)
    bias_n = jnp.where(jnp.logical_and(ci <= ri, jnp.logical_and(latent, j < NCB + NB - 1)), 0.0, NEG)
    gsz = HQ // HKV
    for hk in range(HKV):
        ks = slice(hk * HD, (hk + 1) * HD)
        vs = slice(HKV * HD + hk * HD, HKV * HD + (hk + 1) * HD)
        k_all = jnp.concatenate([kvp_ref[:, ks], kvc_ref[:, ks], kvn_ref[:, ks], kvx_ref[:, ks]], axis=0)
        v_all = jnp.concatenate([kvp_ref[:, vs], kvc_ref[:, vs], kvn_ref[:, vs], kvx_ref[:, vs]], axis=0)
        for h0 in range(hk * gsz, (hk + 1) * gsz, QSTACK):
            qs = jnp.concatenate([q_ref[:, (h0 + g) * HD:(h0 + g + 1) * HD] for g in range(QSTACK)], axis=0)
            s_all = lax.dot_general(qs, k_all, NT_DIMS, preferred_element_type=F32)
            ps, dens = [], []
            for g in range(QSTACK):
                sg = s_all[g * AB:(g + 1) * AB]
                sg = jnp.concatenate([sg[:, 0:AB] + bias_p, sg[:, AB:2 * AB] + bias_c,
                                      sg[:, 2 * AB:3 * AB] + bias_n, sg[:, 3 * AB:]], axis=1)
                snk = sink_ref[h0 + g] * LOG2E
                m = jnp.maximum(jnp.max(sg, axis=-1, keepdims=True), snk)
                e = jnp.exp2(sg - m)
                dens.append(jnp.sum(e, axis=-1, keepdims=True) + jnp.exp2(snk - m))
                ps.append(e.astype(BF))
            o_all = jnp.dot(jnp.concatenate(ps, axis=0), v_all, preferred_element_type=F32)
            for g in range(0, QSTACK, 2):
                o2 = jnp.concatenate([o_all[g * AB:(g + 1) * AB] / dens[g],
                                      o_all[(g + 1) * AB:(g + 2) * AB] / dens[g + 1]], axis=1)
                o_ref[:, (h0 + g) * HD:(h0 + g + 2) * HD] = o2.astype(BF)


def _attention(pb, pkv, sink):
    lat0 = TC // AB

    def qrow(b, j):
        return jnp.where(j < NCB, b * NCB + j, lat0 + b * NB + j - NCB)

    def krow(off):
        def f(b, j):
            i = jnp.clip(j - NCB + off, 0, NB - 1)
            return (lat0 + b * NB + i, 0)
        return f

    return pl.pallas_call(
        _attn_kernel,
        grid=(B, NCB + NB),
        in_specs=[
            pl.BlockSpec((AB, HQ * HD), lambda b, j: (qrow(b, j), 0)),
            pl.BlockSpec((AB, KV_COLS), krow(-1)),
            pl.BlockSpec((AB, KV_COLS), krow(0)),
            pl.BlockSpec((AB, KV_COLS), krow(1)),
            pl.BlockSpec((LC, KV_COLS), lambda b, j: (b, 0)),
            pl.BlockSpec(memory_space=pltpu.SMEM),
        ],
        out_specs=pl.BlockSpec((AB, HQ * HD), lambda b, j: (qrow(b, j), 0)),
        out_shape=jax.ShapeDtypeStruct((R, HQ * HD), BF),
        compiler_params=_cparams(("arbitrary", "arbitrary")),
        name="attention",
    )(pb, pkv, pkv, pkv, pkv, sink)


MG_TN = 512


def _merge_kernel(gla_ref, att_ref, g1_ref, g2_ref, w1_ref, w2_ref, o_ref, w1b_ref, w2b_ref):
    @pl.when(pl.program_id(1) == 0)
    def _():
        w1b_ref[...] = w1_ref[...].astype(BF)
        w2b_ref[...] = w2_ref[...].astype(BF)

    y1 = jnp.dot(gla_ref[...], w1b_ref[...], preferred_element_type=F32)
    y2 = jnp.dot(att_ref[...], w2b_ref[...], preferred_element_type=F32)
    y = _sigmoid(g1_ref[...].astype(F32)) * y1 + _sigmoid(g2_ref[...].astype(F32)) * y2
    o_ref[...] = y.astype(BF)


def _merge(gla, att, pb, w1, w2, l, t0):
    nt = NT_ALL - t0
    g1c = 2048 // MG_TN
    g2c = 4096 // MG_TN
    return pl.pallas_call(
        _merge_kernel,
        grid=(D // MG_TN, nt),
        in_specs=[
            pl.BlockSpec((TM, GH * GDV), lambda j, i: (i + t0, 0)),
            pl.BlockSpec((TM, HQ * HD), lambda j, i: (i + t0, 0)),
            pl.BlockSpec((TM, MG_TN), lambda j, i: (i + t0, g1c + j)),
            pl.BlockSpec((TM, MG_TN), lambda j, i: (i + t0, g2c + j)),
            pl.BlockSpec((None, GH * GDV, MG_TN), lambda j, i: (l, 0, j)),
            pl.BlockSpec((None, HQ * HD, MG_TN), lambda j, i: (l, 0, j)),
        ],
        out_specs=pl.BlockSpec((TM, MG_TN), lambda j, i: (i, j)),
        out_shape=jax.ShapeDtypeStruct((nt * TM, D), BF),
        scratch_shapes=[pltpu.VMEM((GH * GDV, MG_TN), BF), pltpu.VMEM((HQ * HD, MG_TN), BF)],
        compiler_params=_cparams(("arbitrary", "arbitrary")),
        name="merge",
    )(gla, att, pb, pb, w1, w2)


def _residual_norm(y, x, gate, g_post):
    ms = jnp.mean(y * y, axis=-1, keepdims=True)
    return x + gate * (y * lax.rsqrt(ms + EPS) * g_post)


WO_CK = 512


def _outproj_kernel(a_ref, w_ref, x_ref, gt_ref, gp_ref, gn_ref, sc_ref, sh_ref, o_ref, h_ref, wbf_ref, *, nch):
    s = pl.program_id(0)

    @pl.when(s < nch)
    def _():
        wbf_ref[pl.ds(pl.multiple_of(s * WO_CK, WO_CK), WO_CK), :] = w_ref[...].astype(BF)

    @pl.when(s >= nch - 1)
    def _():
        y = jnp.dot(a_ref[...], wbf_ref[...], preferred_element_type=F32)
        xn = _residual_norm(y, x_ref[...], gt_ref[...], gp_ref[...])
        o_ref[...] = xn
        h_ref[...] = _norm_mod(xn, gn_ref[...], sc_ref[...], sh_ref[...]).astype(BF)


def _outproj(a, w_full, l, kdim, tmr, *, t0, a_ctx, x, modl, k_gate, g_post, g_next, modn, k_sc, k_sh, name):
    nch = kdim // WO_CK
    skip = t0 * TM // tmr
    nt = R // tmr - skip
    a_off = skip if a_ctx else 0

    def tile(s):
        return jnp.maximum(s - (nch - 1), 0)

    def mrow(s):
        first = (tile(s) + skip) * tmr
        return jnp.where(first < TC, 2, (first - TC) // L)

    def mspec(k_chunk):
        return pl.BlockSpec((None, 1, D), lambda s: (mrow(s) * 6 + k_chunk, 0, 0))

    vec = pl.BlockSpec((1, D), lambda s: (0, 0))
    row = pl.BlockSpec((tmr, D), lambda s: (tile(s), 0))
    return pl.pallas_call(
        functools.partial(_outproj_kernel, nch=nch),
        grid=(nt + nch - 1,),
        in_specs=[
            pl.BlockSpec((tmr, kdim), lambda s: (tile(s) + a_off, 0)),
            pl.BlockSpec((None, WO_CK, D), lambda s: (l, jnp.minimum(s, nch - 1), 0)),
            pl.BlockSpec((tmr, D), lambda s: (tile(s) + skip, 0)),
            mspec(k_gate), vec, vec, mspec(k_sc), mspec(k_sh),
        ],
        out_specs=[row, row],
        out_shape=[jax.ShapeDtypeStruct((nt * tmr, D), F32), jax.ShapeDtypeStruct((nt * tmr, D), BF)],
        scratch_shapes=[pltpu.VMEM((kdim, D), BF)],
        compiler_params=_cparams(("arbitrary",)),
        name=name,
    )(a, w_full, x, modl, g_post.reshape(1, D), g_next.reshape(1, D), modn, modn)


FF_TN = 1024


def _new_expert(te_ref, i):
    return jnp.logical_or(i == 0, te_ref[i] != te_ref[jnp.maximum(i - 1, 0)])


def _ffn_up_kernel(te_ref, act_ref, a_ref, wg_ref, wu_ref, o_ref, wgb_ref, wub_ref):
    i = pl.program_id(1)

    @pl.when(_new_expert(te_ref, i))
    def _():
        wgb_ref[...] = wg_ref[...].astype(BF)
        wub_ref[...] = wu_ref[...].astype(BF)

    @pl.when(act_ref[i] == 1)
    def _():
        a = a_ref[...]
        yg = jnp.dot(a, wgb_ref[...], preferred_element_type=F32)
        yu = jnp.dot(a, wub_ref[...], preferred_element_type=F32)
        o_ref[...] = (_silu(yg) * yu).astype(BF)

    @pl.when(act_ref[i] == 0)
    def _():
        o_ref[...] = jnp.zeros_like(o_ref)


def _ffn_up(a, wg, wu, lead, te, act):
    nt = a.shape[0] // TM
    wspec = pl.BlockSpec((None, None, D, FF_TN), lambda j, i, te, act: (lead, te[i], 0, j))
    return pl.pallas_call(
        _ffn_up_kernel,
        grid_spec=pltpu.PrefetchScalarGridSpec(
            num_scalar_prefetch=2,
            grid=(pl.cdiv(DFF, FF_TN), nt),
            in_specs=[pl.BlockSpec((TM, D), lambda j, i, te, act: (i, 0)), wspec, wspec],
            out_specs=pl.BlockSpec((TM, FF_TN), lambda j, i, te, act: (i, j)),
            scratch_shapes=[pltpu.VMEM((D, FF_TN), BF), pltpu.VMEM((D, FF_TN), BF)],
        ),
        out_shape=jax.ShapeDtypeStruct((nt * TM, DFF), BF),
        compiler_params=_cparams(("arbitrary", "arbitrary")),
        name="ffn_up",
    )(te, act, a, wg, wu)


DN_TN = 512


def _moe_down_kernel(te_ref, act_ref, a_ref, w_ref, s_ref, o_ref, wb_ref):
    i = pl.program_id(1)

    @pl.when(_new_expert(te_ref, i))
    def _():
        wb_ref[...] = w_ref[...].astype(BF)

    @pl.when(act_ref[i] == 1)
    def _():
        y = jnp.dot(a_ref[...], wb_ref[...], preferred_element_type=F32)
        o_ref[...] = (y * s_ref[...]).astype(BF)

    @pl.when(act_ref[i] == 0)
    def _():
        o_ref[...] = jnp.zeros_like(o_ref)


def _moe_down(u, wd, lead, wrow, te, act):
    nt = u.shape[0] // TM
    return pl.pallas_call(
        _moe_down_kernel,
        grid_spec=pltpu.PrefetchScalarGridSpec(
            num_scalar_prefetch=2,
            grid=(D // DN_TN, nt),
            in_specs=[
                pl.BlockSpec((TM, DFF), lambda j, i, te, act: (i, 0)),
                pl.BlockSpec((None, None, DFF, DN_TN), lambda j, i, te, act: (lead, te[i], 0, j)),
                pl.BlockSpec((TM, 1), lambda j, i, te, act: (i, 0)),
            ],
            out_specs=pl.BlockSpec((TM, DN_TN), lambda j, i, te, act: (i, j)),
            scratch_shapes=[pltpu.VMEM((DFF, DN_TN), BF)],
        ),
        out_shape=jax.ShapeDtypeStruct((nt * TM, D), BF),
        compiler_params=_cparams(("arbitrary", "arbitrary")),
        name="moe_down",
    )(te, act, u, wd, wrow)


def _router_kernel(x_ref, g_ref, sc_ref, sh_ref, wr_ref, br_ref, o_ref, pos_ref, sel_ref, carry_ref):
    @pl.when(pl.program_id(0) == 0)
    def _():
        carry_ref[...] = jnp.zeros_like(carry_ref)

    h = _norm_mod(x_ref[...], g_ref[...], sc_ref[...], sh_ref[...])
    hh, hl = _split(h)
    wh, wl = _split(wr_ref[...])
    nt = (((1,), (1,)), ((), ()))
    lg = (lax.dot_general(wh, hh, nt, preferred_element_type=F32)
          + lax.dot_general(wh, hl, nt, preferred_element_type=F32)
          + lax.dot_general(wl, hh, nt, preferred_element_type=F32) + br_ref[...])
    idx = lax.broadcasted_iota(jnp.int32, lg.shape, 0)
    m1 = jnp.max(lg, axis=0, keepdims=True)
    i1 = jnp.min(jnp.where(lg == m1, idx, NE), axis=0, keepdims=True)
    l2 = jnp.where(idx == i1, NEG, lg)
    m2 = jnp.max(l2, axis=0, keepdims=True)
    i2 = jnp.min(jnp.where(l2 == m2, idx, NE), axis=0, keepdims=True)
    e2 = jnp.exp(m2 - m1)
    w1 = 1.0 / (1.0 + e2)
    w2 = e2 / (1.0 + e2)
    o_ref[...] = jnp.where(idx == i1, w1, 0.0) + jnp.where(idx == i2, w2, 0.0)
    sel = jnp.where(idx == i1, 1.0, jnp.where(idx == i2, 1.0, 0.0))
    si = lax.broadcasted_iota(jnp.int32, (TM, TM), 0)
    ti = lax.broadcasted_iota(jnp.int32, (TM, TM), 1)
    before = jnp.where(si < ti, 1.0, 0.0).astype(BF)
    excl = jnp.dot(sel.astype(BF), before, preferred_element_type=F32)
    pos_ref[...] = (excl + carry_ref[...]).astype(jnp.int32)
    sel_ref[...] = sel.astype(jnp.int32)
    carry_ref[...] += jnp.sum(sel, axis=1, keepdims=True)


def _router(x_lat, g, modl, k_sc, k_sh, w_router, b_router):
    ospec = pl.BlockSpec((NE, TM), lambda i: (0, i))
    return pl.pallas_call(
        _router_kernel,
        grid=(T // TM,),
        in_specs=[
            pl.BlockSpec((TM, D), lambda i: (i, 0)),
            pl.BlockSpec((1, D), lambda i: (0, 0)),
            pl.BlockSpec((None, 1, D), lambda i: (_mrow(i + 1) * 6 + k_sc, 0, 0)),
            pl.BlockSpec((None, 1, D), lambda i: (_mrow(i + 1) * 6 + k_sh, 0, 0)),
            pl.BlockSpec((NE, D), lambda i: (0, 0)),
            pl.BlockSpec((NE, 1), lambda i: (0, 0)),
        ],
        out_specs=[ospec, ospec, ospec],
        out_shape=[jax.ShapeDtypeStruct((NE, T), F32), jax.ShapeDtypeStruct((NE, T), jnp.int32),
                   jax.ShapeDtypeStruct((NE, T), jnp.int32)],
        scratch_shapes=[pltpu.VMEM((NE, 1), F32)],
        compiler_params=_cparams(("arbitrary",)),
        name="router",
    )(x_lat, g.reshape(1, D), modl, modl, w_router.T, b_router.reshape(NE, 1))


NP = 2 * T // TM + NE
DT = 256
NDT = NP * TM // DT
WSTEP = 1408
WTOK = WSTEP + 128
NITEM = 136
CT = 256
NCT = T // CT
ROW_ALIGN = 16
WIN = CT + ROW_ALIGN
WBUF = CT + 128


def _route_tables(pos, sel):
    i32 = jnp.int32
    counts = pos[:, -1] + sel[:, -1]
    ntile = (counts + TM - 1) // TM
    tend = jnp.cumsum(ntile)
    seg = (tend - ntile) * TM
    dest = jnp.where(sel > 0, seg[:, None] + pos, -1)
    d_hi = jnp.max(dest, axis=0)
    d_lo = jnp.sum(dest, axis=0) + (NE - 2) - d_hi
    tiles = jnp.arange(NP, dtype=i32)
    total = tend[-1]
    te = jnp.sum((tiles[:, None] >= tend[None, :]).astype(i32), axis=1)
    te_last = jnp.sum(((total - 1) >= tend).astype(i32))
    act = (tiles < total).astype(i32)
    te = jnp.where(act > 0, te, te_last)
    dtile = jnp.arange(NDT, dtype=i32)
    de = te[dtile // (TM // DT)]
    p_lo = dtile * DT - seg[de]
    has = jnp.logical_and(dtile * DT < total * TM, p_lo < counts[de])
    p_end = jnp.minimum(p_lo + DT, counts[de])
    incl = jnp.take(pos + sel, de, axis=0)
    t_first = jnp.sum((incl <= p_lo[:, None]).astype(i32), axis=1)
    t_last = jnp.sum((incl <= (p_end - 1)[:, None]).astype(i32), axis=1)
    w0 = (t_first // 128) * 128
    nw = jnp.where(has, (t_last + 1 - w0 + WSTEP - 1) // WSTEP, 1)
    cum = jnp.cumsum(nw)
    item = jnp.minimum(jnp.arange(NITEM, dtype=i32), cum[-1] - 1)
    real = jnp.arange(NITEM, dtype=i32) < cum[-1]
    it_tile = jnp.sum((cum[None, :] <= item[:, None]).astype(i32), axis=1)
    k = item - (cum - nw)[it_tile]
    lo = w0[it_tile] + k * WSTEP
    disp = dict(tile=it_tile, expert=de[it_tile], lo=lo,
                tok=jnp.minimum(lo, T - WTOK) // 128,
                valid=jnp.logical_and(real, has[it_tile]).astype(i32),
                first=jnp.logical_and(real, k == 0).astype(i32))
    pbc = pos[:, ::CT]
    r0c = (seg[:, None] + pbc).T.reshape(NCT * NE)
    cntc = (jnp.concatenate([pbc[:, 1:], counts[:, None]], axis=1) - pbc).T.reshape(NCT * NE)
    a0c = jnp.minimum((r0c // ROW_ALIGN) * ROW_ALIGN, NP * TM - WIN)
    comb = dict(a0=a0c, r0=r0c, cnt=cntc)
    return dict(dest=dest, d_hi=d_hi.reshape(T, 1), d_lo=d_lo.reshape(T, 1), te=te, act=act,
                disp=disp, comb=comb)


def _dispatch_kernel(tile_ref, exp_ref, lo_ref, tok_ref, val_ref, first_ref, h_ref, dest_ref, cw_ref,
                     xs_ref, wr_ref):
    i = pl.program_id(0)

    @pl.when(first_ref[i] == 1)
    def _():
        xs_ref[...] = jnp.zeros_like(xs_ref)
        wr_ref[...] = jnp.zeros_like(wr_ref)

    @pl.when(val_ref[i] == 1)
    def _():
        e = exp_ref[i]
        lo = lo_ref[i]
        tok = lax.broadcasted_iota(jnp.int32, (1, WTOK), 1) + tok_ref[i] * 128
        mine = jnp.logical_and(tok >= lo, tok < lo + WSTEP)
        drow = jnp.where(mine, dest_ref[pl.ds(e, 1), :], -1)
        crow = cw_ref[pl.ds(e, 1), :]
        hit = drow == lax.broadcasted_iota(jnp.int32, (DT, WTOK), 0) + tile_ref[i] * DT
        onehot = jnp.where(hit, 1.0, 0.0).astype(BF)
        g = jnp.dot(onehot, h_ref[...], preferred_element_type=F32)
        xs_ref[...] = (xs_ref[...].astype(F32) + g).astype(BF)
        wr_ref[...] += jnp.sum(jnp.where(hit, crow, 0.0), axis=1, keepdims=True)


def _dispatch(h_lat, dest, comb_w, tb):
    def omap(i, tile, *_):
        return (tile[i], 0)

    def rows(i, tile, exp, lo, tok, *_):
        return (tok[i] * 128, 0)

    def lanes(i, tile, exp, lo, tok, *_):
        return (0, tok[i] * 128)

    return pl.pallas_call(
        _dispatch_kernel,
        grid_spec=pltpu.PrefetchScalarGridSpec(
            num_scalar_prefetch=6,
            grid=(NITEM,),
            in_specs=[
                pl.BlockSpec((pl.Element(WTOK), pl.Element(D)), rows),
                pl.BlockSpec((pl.Element(NE), pl.Element(WTOK)), lanes),
                pl.BlockSpec((pl.Element(NE), pl.Element(WTOK)), lanes),
            ],
            out_specs=[pl.BlockSpec((DT, D), omap), pl.BlockSpec((DT, 1), omap)],
        ),
        out_shape=[jax.ShapeDtypeStruct((NP * TM, D), BF), jax.ShapeDtypeStruct((NP * TM, 1), F32)],
        compiler_params=_cparams(("arbitrary",)),
        name="moe_dispatch",
    )(tb["tile"], tb["expert"], tb["lo"], tb["tok"], tb["valid"], tb["first"], h_lat, dest, comb_w)


def _combine_kernel(a0_ref, r0_ref, cnt_ref, y_hbm, dhi_ref, dlo_ref, x_ref, gt_ref, gp_ref, o_ref,
                    ybuf, acc_ref, sem):
    i = pl.program_id(0)
    slot = lax.rem(i, 2)

    def window_copies(tile, buf):
        out = []
        for e in range(NE):
            a0 = pl.multiple_of(a0_ref[tile * NE + e], ROW_ALIGN)
            out.append(pltpu.make_async_copy(y_hbm.at[pl.ds(a0, WIN), :], ybuf.at[buf, e, pl.ds(0, WIN), :],
                                             sem.at[buf, e]))
        return out

    @pl.when(i == 0)
    def _():
        ybuf[:, :, WIN:, :] = jnp.zeros((2, NE, WBUF - WIN, D), BF)
        for cp in window_copies(0, 0):
            cp.start()

    @pl.when(i + 1 < pl.num_programs(0))
    def _():
        for cp in window_copies(i + 1, 1 - slot):
            cp.start()

    for cp in window_copies(i, slot):
        cp.wait()

    dhi = dhi_ref[...]
    dlo = dlo_ref[...]

    def onehot(e, c0, width):
        idx = i * NE + e
        r0 = r0_ref[idx]
        ids = lax.broadcasted_iota(jnp.int32, (1, width), 1) + (a0_ref[idx] + c0)
        ids = jnp.where(jnp.logical_and(ids >= r0, ids < r0 + cnt_ref[idx]), ids, -1)
        return jnp.where(dhi == ids, 1.0, jnp.where(dlo == ids, 1.0, 0.0)).astype(BF)

    acc = jnp.zeros((CT, D), F32)
    for e in range(NE):
        acc = acc + jnp.dot(onehot(e, 0, CT), ybuf[slot, e, 0:CT, :], preferred_element_type=F32)
    acc_ref[...] = acc
    for e in range(NE):
        idx = i * NE + e

        @pl.when(r0_ref[idx] + cnt_ref[idx] > a0_ref[idx] + CT)
        def _():
            acc_ref[...] += jnp.dot(onehot(e, CT, WBUF - CT), ybuf[slot, e, CT:, :], preferred_element_type=F32)

    o_ref[...] = _residual_norm(acc_ref[...], x_ref[...], gt_ref[...], gp_ref[...])


def _combine(y, d_hi, d_lo, x_lat, modl, k_gate, g_post, tb):
    return pl.pallas_call(
        _combine_kernel,
        grid_spec=pltpu.PrefetchScalarGridSpec(
            num_scalar_prefetch=3,
            grid=(NCT,),
            in_specs=[
                pl.BlockSpec(memory_space=pl.ANY),
                pl.BlockSpec((CT, 1), lambda t, *_: (t, 0)),
                pl.BlockSpec((CT, 1), lambda t, *_: (t, 0)),
                pl.BlockSpec((CT, D), lambda t, *_: (t, 0)),
                pl.BlockSpec((None, 1, D), lambda t, *_: (_mrow(t // (TM // CT) + 1) * 6 + k_gate, 0, 0)),
                pl.BlockSpec((1, D), lambda t, *_: (0, 0)),
            ],
            out_specs=pl.BlockSpec((CT, D), lambda t, *_: (t, 0)),
            scratch_shapes=[pltpu.VMEM((2, NE, WBUF, D), BF), pltpu.VMEM((CT, D), F32),
                            pltpu.SemaphoreType.DMA((2, NE))],
        ),
        out_shape=jax.ShapeDtypeStruct((T, D), F32),
        compiler_params=_cparams(("arbitrary",)),
        name="moe_combine",
    )(tb["a0"], tb["r0"], tb["cnt"], y, d_hi, d_lo, x_lat, modl, g_post.reshape(1, D))


def kernel(x, c, ctx, c_ctx, w_mod, b_mod, g_pre_mix, g_post_mix, g_pre_ffn, g_post_ffn,
           w_in, w_lr_f, b_lr_f, w_lr_b, b_lr_b, g_gla, att_sink, w_br_gla, w_br_att, w_out,
           w_ffn_gate, w_ffn_up, w_ffn_down, w_router, b_router, w_exp_gate, w_exp_up, w_exp_down):
    cvec = jnp.zeros((8, D), F32).at[0:B].set(c).at[B].set(c_ctx)
    mod = _modulation(cvec, w_mod, b_mod).reshape(DEPTH, 8 * 6, 1, D)
    cos, sin_a, sin_b = _rope_tables()
    wt = jnp.swapaxes(w_in, 1, 2)

    xa, h = _prenorm(ctx.reshape(TC, D), x.reshape(T, D), g_pre_mix[0], mod[0], 1, 0)
    for l in range(DEPTH):
        last = l == DEPTH - 1
        t0 = 1 if last else 0
        modl = mod[l]
        pa = _proj(h, wt, l, 0, C_LF, 1024, BF, "proj_gla")
        pb, pkv = _proj_att(h, wt, l, (cos, sin_a, sin_b))
        lr = _proj(h, wt, l, C_LF, 2 * GRANK, 2 * GRANK, F32, "proj_decay")

        wlr = jnp.zeros((2, 2 * GRANK, GH * GDK), F32)
        wlr = wlr.at[0, 0:GRANK].set(w_lr_f[l]).at[1, GRANK:2 * GRANK].set(w_lr_b[l])
        blr = jnp.stack([b_lr_f[l], b_lr_b[l]]).reshape(2, 1, GH * GDK)
        o_f, o_b = _gla_scan(_gla_prep(pa, lr, wlr, blr), pa)
        gla = _gla_readout(o_f, o_b, pa, g_gla[l])
        att = _attention(pb, pkv, att_sink[l])
        mm = _merge(gla, att, pb, w_br_gla, w_br_att, l, t0)

        xa, h2 = _outproj(mm, w_out, l, D, TM, t0=t0, a_ctx=False, x=xa, modl=modl, k_gate=2, name="outproj",
                          g_post=g_post_mix[l], g_next=g_pre_ffn[l], modn=modl, k_sc=4, k_sh=3)
        if not last:
            ones = jnp.ones((NT_ALL,), jnp.int32)
            u = _ffn_up(h2, w_ffn_gate[:, None], w_ffn_up[:, None], l // 2, 0 * ones, ones)
            xa, h = _outproj(u, w_ffn_down, l // 2, DFF, TM // 2, t0=0, a_ctx=True, x=xa, modl=modl, k_gate=5,
                             name="ffn_down", g_post=g_post_ffn[l], g_next=g_pre_mix[l + 1], modn=mod[l + 1],
                             k_sc=1, k_sh=0)
        else:
            comb_w, pos, sel = _router(xa, g_pre_ffn[l], modl, 4, 3, w_router[l // 2], b_router[l // 2])
            rt = _route_tables(pos, sel)
            xs, wrow = _dispatch(h2, rt["dest"], comb_w, rt["disp"])
            u = _ffn_up(xs, w_exp_gate, w_exp_up, l // 2, rt["te"], rt["act"])
            y = _moe_down(u, w_exp_down, l // 2, wrow, rt["te"], rt["act"])
            xa = _combine(y, rt["d_hi"], rt["d_lo"], xa, modl, 5, g_post_ffn[l], rt["comb"])
    return xa.reshape(B, L, D)
```

```python
import functools

import jax
import jax.numpy as jnp
import numpy as np
from jax import lax
from jax.experimental import pallas as pl
from jax.experimental.pallas import tpu as pltpu

BF = jnp.bfloat16
F32 = jnp.float32

D = 2048
B = 2
L = 4096
LC = 256
T = B * L
TC = B * LC
R = TC + T
DEPTH = 2
GRID_W = 64
EPS = 1e-6

GH = 4
GDK = 256
GDV = 512
GRANK = 16
GTAU = 16.0
GC = 64

HD = 64
HQ = 32
HKV = 4
ROPE_BASE = 10000.0

DFF = 5632
NE = 8

TM = 512
NT_ALL = R // TM
TPB = L // TM
VMEM_LIMIT = 56 * 1024 * 1024

C_GQ, C_GK, C_GV, C_GR, C_LF, C_LB, C_AQ, C_AK, C_AV, C_G1, C_G2, C_END = (
    0, 1024, 2048, 4096, 6144, 6160, 6176, 8224, 8480, 8736, 10784, 12832)


def _cparams(sem):
    return pltpu.CompilerParams(dimension_semantics=sem, vmem_limit_bytes=VMEM_LIMIT)


def _mrow(gi):
    return jnp.where(gi == 0, 2, (gi - 1) // TPB)


def _sigmoid(z):
    return 1.0 / (1.0 + jnp.exp(-z))


def _silu(z):
    return z * _sigmoid(z)


def _split(a):
    hi = a.astype(BF)
    lo = (a - hi.astype(F32)).astype(BF)
    return hi, lo


def _mod_kernel(c_ref, w_ref, b_ref, o_ref):
    a = _silu(c_ref[...]).astype(BF)
    o_ref[...] = jnp.dot(a, w_ref[...].astype(BF), preferred_element_type=F32) + b_ref[...]


def _modulation(cvec, w_mod, b_mod):
    tn = 1024
    return pl.pallas_call(
        _mod_kernel,
        grid=(DEPTH, 6 * D // tn),
        in_specs=[
            pl.BlockSpec((8, D), lambda l, j: (0, 0)),
            pl.BlockSpec((None, D, tn), lambda l, j: (l, 0, j)),
            pl.BlockSpec((None, 1, tn), lambda l, j: (l, 0, j)),
        ],
        out_specs=pl.BlockSpec((None, 8, tn), lambda l, j: (l, 0, j)),
        out_shape=jax.ShapeDtypeStruct((DEPTH, 8, 6 * D), F32),
        compiler_params=_cparams(("arbitrary", "arbitrary")),
        name="modulation",
    )(cvec, w_mod, b_mod.reshape(DEPTH, 1, 6 * D))


def _norm_mod(x, g, sc, sh):
    ms = jnp.mean(x * x, axis=-1, keepdims=True)
    return (x * lax.rsqrt(ms + EPS) * g) * (1.0 + sc) + sh


def _prenorm_kernel(ctx_ref, x_ref, g_ref, sc_ref, sh_ref, xa_ref, o_ref):
    def emit(src_ref):
        v = src_ref[...]
        xa_ref[...] = v
        o_ref[...] = _norm_mod(v, g_ref[...], sc_ref[...], sh_ref[...]).astype(BF)

    @pl.when(pl.program_id(0) == 0)
    def _():
        emit(ctx_ref)

    @pl.when(pl.program_id(0) > 0)
    def _():
        emit(x_ref)


def _prenorm(ctx2d, x2d, g, modl, k_sc, k_sh):
    row = pl.BlockSpec((TM, D), lambda i: (i, 0))
    return pl.pallas_call(
        _prenorm_kernel,
        grid=(NT_ALL,),
        in_specs=[
            pl.BlockSpec((TM, D), lambda i: (0, 0)),
            pl.BlockSpec((TM, D), lambda i: (jnp.maximum(i - 1, 0), 0)),
            pl.BlockSpec((1, D), lambda i: (0, 0)),
            pl.BlockSpec((None, 1, D), lambda i: (_mrow(i) * 6 + k_sc, 0, 0)),
            pl.BlockSpec((None, 1, D), lambda i: (_mrow(i) * 6 + k_sh, 0, 0)),
        ],
        out_specs=[row, row],
        out_shape=[jax.ShapeDtypeStruct((R, D), F32), jax.ShapeDtypeStruct((R, D), BF)],
        compiler_params=_cparams(("arbitrary",)),
        name="prenorm",
    )(ctx2d, x2d, g.reshape(1, D), modl, modl)


NT_DIMS = (((1,), (1,)), ((), ()))


def _proj_kernel(a_ref, wt_ref, o_ref, wbf_ref):
    @pl.when(pl.program_id(1) == 0)
    def _():
        wbf_ref[...] = wt_ref[...].astype(BF)

    acc = lax.dot_general(a_ref[...], wbf_ref[...], NT_DIMS, preferred_element_type=F32)
    o_ref[...] = acc.astype(o_ref.dtype)


def _proj(h, wt, l, row0, n_cols, tn, out_dtype, name):
    return pl.pallas_call(
        _proj_kernel,
        grid=(n_cols // tn, NT_ALL),
        in_specs=[
            pl.BlockSpec((TM, D), lambda j, i: (i, 0)),
            pl.BlockSpec((None, tn, D), lambda j, i: (l, row0 // tn + j, 0)),
        ],
        out_specs=pl.BlockSpec((TM, tn), lambda j, i: (i, j)),
        out_shape=jax.ShapeDtypeStruct((R, n_cols), out_dtype),
        scratch_shapes=[pltpu.VMEM((tn, D), BF)],
        compiler_params=_cparams(("arbitrary", "arbitrary")),
        name=name,
    )(h, wt)


PB_TN = 1024
PB_COLS = 2048 + 4096
PB_QT = 2048 // PB_TN
KV_COLS = 2 * HKV * HD
LOG2E = 1.4426950408889634
Q_SCALE = HD ** -0.5 * LOG2E


def _rope(x, cos, sin_a, sin_b):
    return x * cos + pltpu.roll(x, 112, axis=1) * sin_a + pltpu.roll(x, 16, axis=1) * sin_b


def _proj_att_kernel(a_ref, wt_ref, cos_ref, sa_ref, sb_ref, o_ref, wbf_ref, *, n_q, rope_cols):
    j = pl.program_id(0)
    i = pl.program_id(1)

    @pl.when(i == 0)
    def _():
        wbf_ref[...] = wt_ref[0].astype(BF)

    acc = lax.dot_general(a_ref[...], wbf_ref[...], NT_DIMS, preferred_element_type=F32)
    tn = acc.shape[1]
    latent = i > 0
    is_q = j < n_q

    def rotated(cols, scale):
        for s in range(cols // 128):
            y = _rope(acc[:, s * 128:(s + 1) * 128], cos_ref[...], sa_ref[...], sb_ref[...])
            o_ref[:, s * 128:(s + 1) * 128] = (y * scale).astype(BF)

    @pl.when(jnp.logical_and(is_q, latent))
    def _():
        rotated(tn, Q_SCALE)

    @pl.when(jnp.logical_and(is_q, jnp.logical_not(latent)))
    def _():
        o_ref[...] = (acc * Q_SCALE).astype(BF)

    plain = jnp.logical_not(is_q)
    if rope_cols:
        plain = jnp.logical_and(plain, jnp.logical_not(latent))

        @pl.when(jnp.logical_and(jnp.logical_not(is_q), latent))
        def _():
            rotated(rope_cols, 1.0)
            o_ref[:, rope_cols:] = acc[:, rope_cols:].astype(BF)

    @pl.when(plain)
    def _():
        o_ref[...] = acc.astype(BF)


def _proj_att_call(h, wt, l, tabs, *, tn, n_tiles, n_q, rope_cols, wrow_units, name):
    def tab(j, i):
        return (jnp.where(i == 0, 0, (i - 1) % TPB), 0)

    return pl.pallas_call(
        functools.partial(_proj_att_kernel, n_q=n_q, rope_cols=rope_cols),
        grid=(n_tiles, NT_ALL),
        in_specs=[
            pl.BlockSpec((TM, D), lambda j, i: (i, 0)),
            pl.BlockSpec((pl.Element(1), pl.Element(tn), pl.Element(D)),
                         lambda j, i: (l, wrow_units(j) * (2 * GRANK), 0)),
            pl.BlockSpec((TM, 128), tab),
            pl.BlockSpec((TM, 128), tab),
            pl.BlockSpec((TM, 128), tab),
        ],
        out_specs=pl.BlockSpec((TM, tn), lambda j, i: (i, j)),
        out_shape=jax.ShapeDtypeStruct((R, n_tiles * tn), BF),
        scratch_shapes=[pltpu.VMEM((tn, D), BF)],
        compiler_params=_cparams(("arbitrary", "arbitrary")),
        name=name,
    )(h, wt, *tabs)


def _proj_att(h, wt, l, tabs):
    u = 2 * GRANK
    pb = _proj_att_call(
        h, wt, l, tabs, tn=PB_TN, n_tiles=PB_COLS // PB_TN, n_q=PB_QT, rope_cols=0, name="proj_att",
        wrow_units=lambda j: jnp.where(j < PB_QT, C_AQ // u + j * (PB_TN // u),
                                       C_G1 // u + (j - PB_QT) * (PB_TN // u)))
    pkv = _proj_att_call(h, wt, l, tabs, tn=KV_COLS, n_tiles=1, n_q=0, rope_cols=HKV * HD, name="proj_kv",
                         wrow_units=lambda j: C_AK // u + j)
    return pb, pkv


def _rope_tables():
    rows = L // GRID_W
    row = np.repeat(np.arange(rows), GRID_W)
    col = np.tile(np.arange(GRID_W), rows)
    half = HD // 2
    inv = (ROPE_BASE ** (-np.arange(0, half, 2, dtype=np.float32) / half)).astype(np.float32)

    def angles(p):
        a = p.astype(np.float32)[:, None] * inv[None, :]
        return np.concatenate([a, a], axis=-1)

    ang = np.concatenate([angles(row), angles(col)], axis=-1)
    ang = np.concatenate([ang, ang], axis=-1)
    cos, sin = np.cos(ang).astype(np.float32), np.sin(ang).astype(np.float32)
    first = (np.arange(128) % 32) < 16
    zero = np.float32(0.0)
    return jnp.asarray(cos), jnp.asarray(np.where(first, -sin, zero)), jnp.asarray(np.where(first, zero, sin))


GG = 256
NG = 1 + L // GG


def _dot_split(m01, a):
    hi, lo = _split(a)
    return jnp.dot(m01, hi, preferred_element_type=F32) + jnp.dot(m01, lo, preferred_element_type=F32)


def _chunk_masks():
    r = np.arange(TM)[:, None]
    c = np.arange(TM)[None, :]
    same = (r // GC) == (c // GC)
    return jnp.asarray(np.stack([same & (r >= c), same & (r <= c), same]).astype(np.float32), dtype=BF)


def _gla_prep_kernel(lr_ref, wlr_ref, blr_ref, q_ref, k_ref, mask_ref, *out_refs):
    lh, ll = _split(lr_ref[...])
    qf = q_ref[...].astype(F32) * (GDK ** -0.5)
    kf = k_ref[...].astype(F32)
    for d in range(2):
        wh, wl = _split(wlr_ref[d])
        z = (jnp.dot(lh, wh, preferred_element_type=F32) + jnp.dot(ll, wh, preferred_element_type=F32)
             + jnp.dot(lh, wl, preferred_element_type=F32) + blr_ref[d])
        la = (jnp.minimum(z, 0.0) - jnp.log1p(jnp.exp(-jnp.abs(z)))) * (1.0 / GTAU)
        bcum = _dot_split(mask_ref[d], la)
        tot = _dot_split(mask_ref[2], la)
        qi_ref, ki_ref, ko_ref, dec_ref = out_refs[4 * d:4 * d + 4]
        qi_ref[...] = (qf * jnp.exp(bcum)).astype(BF)
        ki_ref[...] = (kf * jnp.exp(-bcum)).astype(BF)
        ko_ref[...] = (kf * jnp.exp(tot - bcum)).astype(BF)
        dec_ref[...] = jnp.exp(tot)


def _gla_prep(pa, lr, wlr, blr):
    hk = GH * GDK
    ospec = pl.BlockSpec((TM, GDK), lambda i, h: (i, h))
    return pl.pallas_call(
        _gla_prep_kernel,
        grid=(NT_ALL, GH),
        in_specs=[
            pl.BlockSpec((TM, 2 * GRANK), lambda i, h: (i, 0)),
            pl.BlockSpec((2, 2 * GRANK, GDK), lambda i, h: (0, 0, h)),
            pl.BlockSpec((2, 1, GDK), lambda i, h: (0, 0, h)),
            pl.BlockSpec((TM, GDK), lambda i, h: (i, C_GQ // GDK + h)),
            pl.BlockSpec((TM, GDK), lambda i, h: (i, C_GK // GDK + h)),
            pl.BlockSpec((3, TM, TM), lambda i, h: (0, 0, 0)),
        ],
        out_specs=[ospec] * 8,
        out_shape=[jax.ShapeDtypeStruct((R, hk), BF)] * 3 + [jax.ShapeDtypeStruct((R, hk), F32)]
                  + [jax.ShapeDtypeStruct((R, hk), BF)] * 3 + [jax.ShapeDtypeStruct((R, hk), F32)],
        compiler_params=_cparams(("arbitrary", "arbitrary")),
        name="gla_prep",
    )(lr, wlr, blr, pa, pa, _chunk_masks())


HP = 4


def _gla_scan_kernel(qf_ref, kif_ref, kof_ref, df_ref, vf_ref, qb_ref, kib_ref, kob_ref, db_ref, vb_ref,
                     of_ref, ob_ref, s_ref):
    @pl.when(pl.program_id(2) == 0)
    def _():
        s_ref[...] = jnp.zeros_like(s_ref)

    ri = lax.broadcasted_iota(jnp.int32, (GC, GC), 0)
    ci = lax.broadcasted_iota(jnp.int32, (GC, GC), 1)
    tn = (((0,), (0,)), ((), ()))
    dirs = ((qf_ref, kif_ref, kof_ref, df_ref, vf_ref, of_ref, ri >= ci, range(GG // GC)),
            (qb_ref, kib_ref, kob_ref, db_ref, vb_ref, ob_ref, ri <= ci, reversed(range(GG // GC))))
    for d, (q_ref, ki_ref, ko_ref, dec_ref, v_ref, o_ref, keep, order) in enumerate(dirs):
        for c in order:
            rows = slice(c * GC, (c + 1) * GC)
            for hh in range(HP):
                kc = slice(hh * GDK, (hh + 1) * GDK)
                vc = slice(hh * GDV, (hh + 1) * GDV)
                q_in = q_ref[rows, kc]
                v = v_ref[rows, vc]
                a = lax.dot_general(q_in, ki_ref[rows, kc], NT_DIMS, preferred_element_type=F32)
                a = jnp.where(keep, a, 0.0).astype(BF)
                st = s_ref[d, hh]
                o = jnp.dot(a, v, preferred_element_type=F32)
                o = o + lax.dot_general(q_in, st.astype(BF), NT_DIMS, preferred_element_type=F32)
                o_ref[rows, vc] = o
                upd = lax.dot_general(v, ko_ref[rows, kc], tn, preferred_element_type=F32)
                s_ref[d, hh] = st * dec_ref[c * GC:c * GC + 1, kc] + upd


def _gla_scan(prep, pa):
    lat0 = TC // GG

    def spec(width, d, col0):
        def index(b, p, g):
            lat = (g - 1) if d == 0 else (NG - 1 - g)
            return (jnp.where(g == 0, b, lat0 + b * (L // GG) + lat), col0 + p)
        return pl.BlockSpec((GG, width), index)

    in_specs = []
    for d in range(2):
        in_specs += [spec(HP * GDK, d, 0)] * 4 + [spec(HP * GDV, d, C_GV // (HP * GDV))]
    out_specs = [spec(HP * GDV, d, 0) for d in range(2)]
    return pl.pallas_call(
        _gla_scan_kernel,
        grid=(B, GH // HP, NG),
        in_specs=in_specs,
        out_specs=out_specs,
        out_shape=[jax.ShapeDtypeStruct((R, GH * GDV), F32)] * 2,
        scratch_shapes=[pltpu.VMEM((2, HP, GDV, GDK), F32)],
        compiler_params=_cparams(("arbitrary", "arbitrary", "arbitrary")),
        name="gla_scan",
    )(*prep[0:4], pa, *prep[4:8], pa)


def _gla_readout_kernel(of_ref, ob_ref, r_ref, gg_ref, o_ref):
    for h in range(GH):
        cols = slice(h * GDV, (h + 1) * GDV)
        o = of_ref[:, cols] + ob_ref[:, cols]
        on = o * lax.rsqrt(jnp.mean(o * o, axis=-1, keepdims=True) + EPS) * gg_ref[...]
        o_ref[:, cols] = (on * _silu(r_ref[:, cols].astype(F32))).astype(BF)


def _gla_readout(o_f, o_b, pa, g_gla):
    spec = pl.BlockSpec((TM, GH * GDV), lambda i: (i, 0))
    return pl.pallas_call(
        _gla_readout_kernel,
        grid=(NT_ALL,),
        in_specs=[spec, spec, pl.BlockSpec((TM, GH * GDV), lambda i: (i, C_GR // (GH * GDV))),
                  pl.BlockSpec((1, GDV), lambda i: (0, 0))],
        out_specs=spec,
        out_shape=jax.ShapeDtypeStruct((R, GH * GDV), BF),
        compiler_params=_cparams(("arbitrary",)),
        name="gla_readout",
    )(o_f, o_b, pa, g_gla.reshape(1, GDV))


AB = 128
NB = L // AB
NCB = LC // AB
NEG = float("-inf")
QSTACK = 8


def _attn_kernel(q_ref, kvp_ref, kvc_ref, kvn_ref, kvx_ref, sink_ref, o_ref):
    j = pl.program_id(1)
    latent = j >= NCB
    ri = lax.broadcasted_iota(jnp.int32, (AB, AB), 0)
    ci = lax.broadcasted_iota(jnp.int32, (AB, AB), 1)
    bias_p = jnp.where(jnp.logical_and(ci >= ri, j > NCB), 0.0, NEG)
    bias_c = jnp.where(latent, 0.0, NEG)
    bias_n = jnp.where(jnp.logical_and(ci <= ri, jnp.logical_and(latent, j < NCB + NB - 1)), 0.0, NEG)
    gsz = HQ // HKV
    for hk in range(HKV):
        ks = slice(hk * HD, (hk + 1) * HD)
        vs = slice(HKV * HD + hk * HD, HKV * HD + (hk + 1) * HD)
        k_all = jnp.concatenate([kvp_ref[:, ks], kvc_ref[:, ks], kvn_ref[:, ks], kvx_ref[:, ks]], axis=0)
        v_all = jnp.concatenate([kvp_ref[:, vs], kvc_ref[:, vs], kvn_ref[:, vs], kvx_ref[:, vs]], axis=0)
        for h0 in range(hk * gsz, (hk + 1) * gsz, QSTACK):
            qs = jnp.concatenate([q_ref[:, (h0 + g) * HD:(h0 + g + 1) * HD] for g in range(QSTACK)], axis=0)
            s_all = lax.dot_general(qs, k_all, NT_DIMS, preferred_element_type=F32)
            ps, dens = [], []
            for g in range(QSTACK):
                sg = s_all[g * AB:(g + 1) * AB]
                sg = jnp.concatenate([sg[:, 0:AB] + bias_p, sg[:, AB:2 * AB] + bias_c,
                                      sg[:, 2 * AB:3 * AB] + bias_n, sg[:, 3 * AB:]], axis=1)
                snk = sink_ref[h0 + g] * LOG2E
                m = jnp.maximum(jnp.max(sg, axis=-1, keepdims=True), snk)
                e = jnp.exp2(sg - m)
                dens.append(jnp.sum(e, axis=-1, keepdims=True) + jnp.exp2(snk - m))
                ps.append(e.astype(BF))
            o_all = jnp.dot(jnp.concatenate(ps, axis=0), v_all, preferred_element_type=F32)
            for g in range(0, QSTACK, 2):
                o2 = jnp.concatenate([o_all[g * AB:(g + 1) * AB] / dens[g],
                                      o_all[(g + 1) * AB:(g + 2) * AB] / dens[g + 1]], axis=1)
                o_ref[:, (h0 + g) * HD:(h0 + g + 2) * HD] = o2.astype(BF)


def _attention(pb, pkv, sink):
    lat0 = TC // AB

    def qrow(b, j):
        return jnp.where(j < NCB, b * NCB + j, lat0 + b * NB + j - NCB)

    def krow(off):
        def f(b, j):
            i = jnp.clip(j - NCB + off, 0, NB - 1)
            return (lat0 + b * NB + i, 0)
        return f

    return pl.pallas_call(
        _attn_kernel,
        grid=(B, NCB + NB),
        in_specs=[
            pl.BlockSpec((AB, HQ * HD), lambda b, j: (qrow(b, j), 0)),
            pl.BlockSpec((AB, KV_COLS), krow(-1)),
            pl.BlockSpec((AB, KV_COLS), krow(0)),
            pl.BlockSpec((AB, KV_COLS), krow(1)),
            pl.BlockSpec((LC, KV_COLS), lambda b, j: (b, 0)),
            pl.BlockSpec(memory_space=pltpu.SMEM),
        ],
        out_specs=pl.BlockSpec((AB, HQ * HD), lambda b, j: (qrow(b, j), 0)),
        out_shape=jax.ShapeDtypeStruct((R, HQ * HD), BF),
        compiler_params=_cparams(("arbitrary", "arbitrary")),
        name="attention",
    )(pb, pkv, pkv, pkv, pkv, sink)


MG_TN = 512


def _merge_kernel(gla_ref, att_ref, g1_ref, g2_ref, w1_ref, w2_ref, o_ref, w1b_ref, w2b_ref):
    @pl.when(pl.program_id(1) == 0)
    def _():
        w1b_ref[...] = w1_ref[...].astype(BF)
        w2b_ref[...] = w2_ref[...].astype(BF)

    y1 = jnp.dot(gla_ref[...], w1b_ref[...], preferred_element_type=F32)
    y2 = jnp.dot(att_ref[...], w2b_ref[...], preferred_element_type=F32)
    y = _sigmoid(g1_ref[...].astype(F32)) * y1 + _sigmoid(g2_ref[...].astype(F32)) * y2
    o_ref[...] = y.astype(BF)


def _merge(gla, att, pb, w1, w2, l, t0):
    nt = NT_ALL - t0
    g1c = 2048 // MG_TN
    g2c = 4096 // MG_TN
    return pl.pallas_call(
        _merge_kernel,
        grid=(D // MG_TN, nt),
        in_specs=[
            pl.BlockSpec((TM, GH * GDV), lambda j, i: (i + t0, 0)),
            pl.BlockSpec((TM, HQ * HD), lambda j, i: (i + t0, 0)),
            pl.BlockSpec((TM, MG_TN), lambda j, i: (i + t0, g1c + j)),
            pl.BlockSpec((TM, MG_TN), lambda j, i: (i + t0, g2c + j)),
            pl.BlockSpec((None, GH * GDV, MG_TN), lambda j, i: (l, 0, j)),
            pl.BlockSpec((None, HQ * HD, MG_TN), lambda j, i: (l, 0, j)),
        ],
        out_specs=pl.BlockSpec((TM, MG_TN), lambda j, i: (i, j)),
        out_shape=jax.ShapeDtypeStruct((nt * TM, D), BF),
        scratch_shapes=[pltpu.VMEM((GH * GDV, MG_TN), BF), pltpu.VMEM((HQ * HD, MG_TN), BF)],
        compiler_params=_cparams(("arbitrary", "arbitrary")),
        name="merge",
    )(gla, att, pb, pb, w1, w2)


def _residual_norm(y, x, gate, g_post):
    ms = jnp.mean(y * y, axis=-1, keepdims=True)
    return x + gate * (y * lax.rsqrt(ms + EPS) * g_post)


WO_CK = 512


def _outproj_kernel(a_ref, w_ref, x_ref, gt_ref, gp_ref, gn_ref, sc_ref, sh_ref, o_ref, h_ref, wbf_ref, *, nch):
    s = pl.program_id(0)

    @pl.when(s < nch)
    def _():
        wbf_ref[pl.ds(pl.multiple_of(s * WO_CK, WO_CK), WO_CK), :] = w_ref[...].astype(BF)

    @pl.when(s >= nch - 1)
    def _():
        y = jnp.dot(a_ref[...], wbf_ref[...], preferred_element_type=F32)
        xn = _residual_norm(y, x_ref[...], gt_ref[...], gp_ref[...])
        o_ref[...] = xn
        h_ref[...] = _norm_mod(xn, gn_ref[...], sc_ref[...], sh_ref[...]).astype(BF)


def _outproj(a, w_full, l, kdim, tmr, *, t0, a_ctx, x, modl, k_gate, g_post, g_next, modn, k_sc, k_sh, name):
    nch = kdim // WO_CK
    skip = t0 * TM // tmr
    nt = R // tmr - skip
    a_off = skip if a_ctx else 0

    def tile(s):
        return jnp.maximum(s - (nch - 1), 0)

    def mrow(s):
        first = (tile(s) + skip) * tmr
        return jnp.where(first < TC, 2, (first - TC) // L)

    def mspec(k_chunk):
        return pl.BlockSpec((None, 1, D), lambda s: (mrow(s) * 6 + k_chunk, 0, 0))

    vec = pl.BlockSpec((1, D), lambda s: (0, 0))
    row = pl.BlockSpec((tmr, D), lambda s: (tile(s), 0))
    return pl.pallas_call(
        functools.partial(_outproj_kernel, nch=nch),
        grid=(nt + nch - 1,),
        in_specs=[
            pl.BlockSpec((tmr, kdim), lambda s: (tile(s) + a_off, 0)),
            pl.BlockSpec((None, WO_CK, D), lambda s: (l, jnp.minimum(s, nch - 1), 0)),
            pl.BlockSpec((tmr, D), lambda s: (tile(s) + skip, 0)),
            mspec(k_gate), vec, vec, mspec(k_sc), mspec(k_sh),
        ],
        out_specs=[row, row],
        out_shape=[jax.ShapeDtypeStruct((nt * tmr, D), F32), jax.ShapeDtypeStruct((nt * tmr, D), BF)],
        scratch_shapes=[pltpu.VMEM((kdim, D), BF)],
        compiler_params=_cparams(("arbitrary",)),
        name=name,
    )(a, w_full, x, modl, g_post.reshape(1, D), g_next.reshape(1, D), modn, modn)


FF_TN = 1024


def _new_expert(te_ref, i):
    return jnp.logical_or(i == 0, te_ref[i] != te_ref[jnp.maximum(i - 1, 0)])


def _ffn_up_kernel(te_ref, act_ref, a_ref, wg_ref, wu_ref, o_ref, wgb_ref, wub_ref):
    i = pl.program_id(1)

    @pl.when(_new_expert(te_ref, i))
    def _():
        wgb_ref[...] = wg_ref[...].astype(BF)
        wub_ref[...] = wu_ref[...].astype(BF)

    @pl.when(act_ref[i] == 1)
    def _():
        a = a_ref[...]
        yg = jnp.dot(a, wgb_ref[...], preferred_element_type=F32)
        yu = jnp.dot(a, wub_ref[...], preferred_element_type=F32)
        o_ref[...] = (_silu(yg) * yu).astype(BF)

    @pl.when(act_ref[i] == 0)
    def _():
        o_ref[...] = jnp.zeros_like(o_ref)


def _ffn_up(a, wg, wu, lead, te, act):
    nt = a.shape[0] // TM
    wspec = pl.BlockSpec((None, None, D, FF_TN), lambda j, i, te, act: (lead, te[i], 0, j))
    return pl.pallas_call(
        _ffn_up_kernel,
        grid_spec=pltpu.PrefetchScalarGridSpec(
            num_scalar_prefetch=2,
            grid=(pl.cdiv(DFF, FF_TN), nt),
            in_specs=[pl.BlockSpec((TM, D), lambda j, i, te, act: (i, 0)), wspec, wspec],
            out_specs=pl.BlockSpec((TM, FF_TN), lambda j, i, te, act: (i, j)),
            scratch_shapes=[pltpu.VMEM((D, FF_TN), BF), pltpu.VMEM((D, FF_TN), BF)],
        ),
        out_shape=jax.ShapeDtypeStruct((nt * TM, DFF), BF),
        compiler_params=_cparams(("arbitrary", "arbitrary")),
        name="ffn_up",
    )(te, act, a, wg, wu)


DN_TN = 512


def _moe_down_kernel(te_ref, act_ref, a_ref, w_ref, s_ref, o_ref, wb_ref):
    i = pl.program_id(1)

    @pl.when(_new_expert(te_ref, i))
    def _():
        wb_ref[...] = w_ref[...].astype(BF)

    @pl.when(act_ref[i] == 1)
    def _():
        y = jnp.dot(a_ref[...], wb_ref[...], preferred_element_type=F32)
        o_ref[...] = (y * s_ref[...]).astype(BF)

    @pl.when(act_ref[i] == 0)
    def _():
        o_ref[...] = jnp.zeros_like(o_ref)


def _moe_down(u, wd, lead, wrow, te, act):
    nt = u.shape[0] // TM
    return pl.pallas_call(
        _moe_down_kernel,
        grid_spec=pltpu.PrefetchScalarGridSpec(
            num_scalar_prefetch=2,
            grid=(D // DN_TN, nt),
            in_specs=[
                pl.BlockSpec((TM, DFF), lambda j, i, te, act: (i, 0)),
                pl.BlockSpec((None, None, DFF, DN_TN), lambda j, i, te, act: (lead, te[i], 0, j)),
                pl.BlockSpec((TM, 1), lambda j, i, te, act: (i, 0)),
            ],
            out_specs=pl.BlockSpec((TM, DN_TN), lambda j, i, te, act: (i, j)),
            scratch_shapes=[pltpu.VMEM((DFF, DN_TN), BF)],
        ),
        out_shape=jax.ShapeDtypeStruct((nt * TM, D), BF),
        compiler_params=_cparams(("arbitrary", "arbitrary")),
        name="moe_down",
    )(te, act, u, wd, wrow)


def _router_kernel(x_ref, g_ref, sc_ref, sh_ref, wr_ref, br_ref, o_ref, pos_ref, sel_ref, carry_ref):
    @pl.when(pl.program_id(0) == 0)
    def _():
        carry_ref[...] = jnp.zeros_like(carry_ref)

    h = _norm_mod(x_ref[...], g_ref[...], sc_ref[...], sh_ref[...])
    hh, hl = _split(h)
    wh, wl = _split(wr_ref[...])
    nt = (((1,), (1,)), ((), ()))
    lg = (lax.dot_general(wh, hh, nt, preferred_element_type=F32)
          + lax.dot_general(wh, hl, nt, preferred_element_type=F32)
          + lax.dot_general(wl, hh, nt, preferred_element_type=F32) + br_ref[...])
    idx = lax.broadcasted_iota(jnp.int32, lg.shape, 0)
    m1 = jnp.max(lg, axis=0, keepdims=True)
    i1 = jnp.min(jnp.where(lg == m1, idx, NE), axis=0, keepdims=True)
    l2 = jnp.where(idx == i1, NEG, lg)
    m2 = jnp.max(l2, axis=0, keepdims=True)
    i2 = jnp.min(jnp.where(l2 == m2, idx, NE), axis=0, keepdims=True)
    e2 = jnp.exp(m2 - m1)
    w1 = 1.0 / (1.0 + e2)
    w2 = e2 / (1.0 + e2)
    o_ref[...] = jnp.where(idx == i1, w1, 0.0) + jnp.where(idx == i2, w2, 0.0)
    sel = jnp.where(idx == i1, 1.0, jnp.where(idx == i2, 1.0, 0.0))
    si = lax.broadcasted_iota(jnp.int32, (TM, TM), 0)
    ti = lax.broadcasted_iota(jnp.int32, (TM, TM), 1)
    before = jnp.where(si < ti, 1.0, 0.0).astype(BF)
    excl = jnp.dot(sel.astype(BF), before, preferred_element_type=F32)
    pos_ref[...] = (excl + carry_ref[...]).astype(jnp.int32)
    sel_ref[...] = sel.astype(jnp.int32)
    carry_ref[...] += jnp.sum(sel, axis=1, keepdims=True)


def _router(x_lat, g, modl, k_sc, k_sh, w_router, b_router):
    ospec = pl.BlockSpec((NE, TM), lambda i: (0, i))
    return pl.pallas_call(
        _router_kernel,
        grid=(T // TM,),
        in_specs=[
            pl.BlockSpec((TM, D), lambda i: (i, 0)),
            pl.BlockSpec((1, D), lambda i: (0, 0)),
            pl.BlockSpec((None, 1, D), lambda i: (_mrow(i + 1) * 6 + k_sc, 0, 0)),
            pl.BlockSpec((None, 1, D), lambda i: (_mrow(i + 1) * 6 + k_sh, 0, 0)),
            pl.BlockSpec((NE, D), lambda i: (0, 0)),
            pl.BlockSpec((NE, 1), lambda i: (0, 0)),
        ],
        out_specs=[ospec, ospec, ospec],
        out_shape=[jax.ShapeDtypeStruct((NE, T), F32), jax.ShapeDtypeStruct((NE, T), jnp.int32),
                   jax.ShapeDtypeStruct((NE, T), jnp.int32)],
        scratch_shapes=[pltpu.VMEM((NE, 1), F32)],
        compiler_params=_cparams(("arbitrary",)),
        name="router",
    )(x_lat, g.reshape(1, D), modl, modl, w_router.T, b_router.reshape(NE, 1))


NP = 2 * T // TM + NE
DT = 256
NDT = NP * TM // DT
WSTEP = 1408
WTOK = WSTEP + 128
NITEM = 136
CT = 256
NCT = T // CT
ROW_ALIGN = 16
WIN = CT + ROW_ALIGN
WBUF = CT + 128


def _route_tables(pos, sel):
    i32 = jnp.int32
    counts = pos[:, -1] + sel[:, -1]
    ntile = (counts + TM - 1) // TM
    tend = jnp.cumsum(ntile)
    seg = (tend - ntile) * TM
    dest = jnp.where(sel > 0, seg[:, None] + pos, -1)
    d_hi = jnp.max(dest, axis=0)
    d_lo = jnp.sum(dest, axis=0) + (NE - 2) - d_hi
    tiles = jnp.arange(NP, dtype=i32)
    total = tend[-1]
    te = jnp.sum((tiles[:, None] >= tend[None, :]).astype(i32), axis=1)
    te_last = jnp.sum(((total - 1) >= tend).astype(i32))
    act = (tiles < total).astype(i32)
    te = jnp.where(act > 0, te, te_last)
    dtile = jnp.arange(NDT, dtype=i32)
    de = te[dtile // (TM // DT)]
    p_lo = dtile * DT - seg[de]
    has = jnp.logical_and(dtile * DT < total * TM, p_lo < counts[de])
    p_end = jnp.minimum(p_lo + DT, counts[de])
    incl = jnp.take(pos + sel, de, axis=0)
    t_first = jnp.sum((incl <= p_lo[:, None]).astype(i32), axis=1)
    t_last = jnp.sum((incl <= (p_end - 1)[:, None]).astype(i32), axis=1)
    w0 = (t_first // 128) * 128
    nw = jnp.where(has, (t_last + 1 - w0 + WSTEP - 1) // WSTEP, 1)
    cum = jnp.cumsum(nw)
    item = jnp.minimum(jnp.arange(NITEM, dtype=i32), cum[-1] - 1)
    real = jnp.arange(NITEM, dtype=i32) < cum[-1]
    it_tile = jnp.sum((cum[None, :] <= item[:, None]).astype(i32), axis=1)
    k = item - (cum - nw)[it_tile]
    lo = w0[it_tile] + k * WSTEP
    disp = dict(tile=it_tile, expert=de[it_tile], lo=lo,
                tok=jnp.minimum(lo, T - WTOK) // 128,
                valid=jnp.logical_and(real, has[it_tile]).astype(i32),
                first=jnp.logical_and(real, k == 0).astype(i32))
    pbc = pos[:, ::CT]
    r0c = (seg[:, None] + pbc).T.reshape(NCT * NE)
    cntc = (jnp.concatenate([pbc[:, 1:], counts[:, None]], axis=1) - pbc).T.reshape(NCT * NE)
    a0c = jnp.minimum((r0c // ROW_ALIGN) * ROW_ALIGN, NP * TM - WIN)
    comb = dict(a0=a0c, r0=r0c, cnt=cntc)
    return dict(dest=dest, d_hi=d_hi.reshape(T, 1), d_lo=d_lo.reshape(T, 1), te=te, act=act,
                disp=disp, comb=comb)


def _dispatch_kernel(tile_ref, exp_ref, lo_ref, tok_ref, val_ref, first_ref, h_ref, dest_ref, cw_ref,
                     xs_ref, wr_ref):
    i = pl.program_id(0)

    @pl.when(first_ref[i] == 1)
    def _():
        xs_ref[...] = jnp.zeros_like(xs_ref)
        wr_ref[...] = jnp.zeros_like(wr_ref)

    @pl.when(val_ref[i] == 1)
    def _():
        e = exp_ref[i]
        lo = lo_ref[i]
        tok = lax.broadcasted_iota(jnp.int32, (1, WTOK), 1) + tok_ref[i] * 128
        mine = jnp.logical_and(tok >= lo, tok < lo + WSTEP)
        drow = jnp.where(mine, dest_ref[pl.ds(e, 1), :], -1)
        crow = cw_ref[pl.ds(e, 1), :]
        hit = drow == lax.broadcasted_iota(jnp.int32, (DT, WTOK), 0) + tile_ref[i] * DT
        onehot = jnp.where(hit, 1.0, 0.0).astype(BF)
        g = jnp.dot(onehot, h_ref[...], preferred_element_type=F32)
        xs_ref[...] = (xs_ref[...].astype(F32) + g).astype(BF)
        wr_ref[...] += jnp.sum(jnp.where(hit, crow, 0.0), axis=1, keepdims=True)


def _dispatch(h_lat, dest, comb_w, tb):
    def omap(i, tile, *_):
        return (tile[i], 0)

    def rows(i, tile, exp, lo, tok, *_):
        return (tok[i] * 128, 0)

    def lanes(i, tile, exp, lo, tok, *_):
        return (0, tok[i] * 128)

    return pl.pallas_call(
        _dispatch_kernel,
        grid_spec=pltpu.PrefetchScalarGridSpec(
            num_scalar_prefetch=6,
            grid=(NITEM,),
            in_specs=[
                pl.BlockSpec((pl.Element(WTOK), pl.Element(D)), rows),
                pl.BlockSpec((pl.Element(NE), pl.Element(WTOK)), lanes),
                pl.BlockSpec((pl.Element(NE), pl.Element(WTOK)), lanes),
            ],
            out_specs=[pl.BlockSpec((DT, D), omap), pl.BlockSpec((DT, 1), omap)],
        ),
        out_shape=[jax.ShapeDtypeStruct((NP * TM, D), BF), jax.ShapeDtypeStruct((NP * TM, 1), F32)],
        compiler_params=_cparams(("arbitrary",)),
        name="moe_dispatch",
    )(tb["tile"], tb["expert"], tb["lo"], tb["tok"], tb["valid"], tb["first"], h_lat, dest, comb_w)


def _combine_kernel(a0_ref, r0_ref, cnt_ref, y_hbm, dhi_ref, dlo_ref, x_ref, gt_ref, gp_ref, o_ref,
                    ybuf, acc_ref, sem):
    i = pl.program_id(0)
    slot = lax.rem(i, 2)

    def window_copies(tile, buf):
        out = []
        for e in range(NE):
            a0 = pl.multiple_of(a0_ref[tile * NE + e], ROW_ALIGN)
            out.append(pltpu.make_async_copy(y_hbm.at[pl.ds(a0, WIN), :], ybuf.at[buf, e, pl.ds(0, WIN), :],
                                             sem.at[buf, e]))
        return out

    @pl.when(i == 0)
    def _():
        ybuf[:, :, WIN:, :] = jnp.zeros((2, NE, WBUF - WIN, D), BF)
        for cp in window_copies(0, 0):
            cp.start()

    @pl.when(i + 1 < pl.num_programs(0))
    def _():
        for cp in window_copies(i + 1, 1 - slot):
            cp.start()

    for cp in window_copies(i, slot):
        cp.wait()

    dhi = dhi_ref[...]
    dlo = dlo_ref[...]

    def onehot(e, c0, width):
        idx = i * NE + e
        r0 = r0_ref[idx]
        ids = lax.broadcasted_iota(jnp.int32, (1, width), 1) + (a0_ref[idx] + c0)
        ids = jnp.where(jnp.logical_and(ids >= r0, ids < r0 + cnt_ref[idx]), ids, -1)
        return jnp.where(dhi == ids, 1.0, jnp.where(dlo == ids, 1.0, 0.0)).astype(BF)

    acc = jnp.zeros((CT, D), F32)
    for e in range(NE):
        acc = acc + jnp.dot(onehot(e, 0, CT), ybuf[slot, e, 0:CT, :], preferred_element_type=F32)
    acc_ref[...] = acc
    for e in range(NE):
        idx = i * NE + e

        @pl.when(r0_ref[idx] + cnt_ref[idx] > a0_ref[idx] + CT)
        def _():
            acc_ref[...] += jnp.dot(onehot(e, CT, WBUF - CT), ybuf[slot, e, CT:, :], preferred_element_type=F32)

    o_ref[...] = _residual_norm(acc_ref[...], x_ref[...], gt_ref[...], gp_ref[...])


def _combine(y, d_hi, d_lo, x_lat, modl, k_gate, g_post, tb):
    return pl.pallas_call(
        _combine_kernel,
        grid_spec=pltpu.PrefetchScalarGridSpec(
            num_scalar_prefetch=3,
            grid=(NCT,),
            in_specs=[
                pl.BlockSpec(memory_space=pl.ANY),
                pl.BlockSpec((CT, 1), lambda t, *_: (t, 0)),
                pl.BlockSpec((CT, 1), lambda t, *_: (t, 0)),
                pl.BlockSpec((CT, D), lambda t, *_: (t, 0)),
                pl.BlockSpec((None, 1, D), lambda t, *_: (_mrow(t // (TM // CT) + 1) * 6 + k_gate, 0, 0)),
                pl.BlockSpec((1, D), lambda t, *_: (0, 0)),
            ],
            out_specs=pl.BlockSpec((CT, D), lambda t, *_: (t, 0)),
            scratch_shapes=[pltpu.VMEM((2, NE, WBUF, D), BF), pltpu.VMEM((CT, D), F32),
                            pltpu.SemaphoreType.DMA((2, NE))],
        ),
        out_shape=jax.ShapeDtypeStruct((T, D), F32),
        compiler_params=_cparams(("arbitrary",)),
        name="moe_combine",
    )(tb["a0"], tb["r0"], tb["cnt"], y, d_hi, d_lo, x_lat, modl, g_post.reshape(1, D))


def kernel(x, c, ctx, c_ctx, w_mod, b_mod, g_pre_mix, g_post_mix, g_pre_ffn, g_post_ffn,
           w_in, w_lr_f, b_lr_f, w_lr_b, b_lr_b, g_gla, att_sink, w_br_gla, w_br_att, w_out,
           w_ffn_gate, w_ffn_up, w_ffn_down, w_router, b_router, w_exp_gate, w_exp_up, w_exp_down):
    cvec = jnp.zeros((8, D), F32).at[0:B].set(c).at[B].set(c_ctx)
    mod = _modulation(cvec, w_mod, b_mod).reshape(DEPTH, 8 * 6, 1, D)
    cos, sin_a, sin_b = _rope_tables()
    wt = jnp.swapaxes(w_in, 1, 2)

    xa, h = _prenorm(ctx.reshape(TC, D), x.reshape(T, D), g_pre_mix[0], mod[0], 1, 0)
    for l in range(DEPTH):
        last = l == DEPTH - 1
        t0 = 1 if last else 0
        modl = mod[l]
        pa = _proj(h, wt, l, 0, C_LF, 1024, BF, "proj_gla")
        pb, pkv = _proj_att(h, wt, l, (cos, sin_a, sin_b))
        lr = _proj(h, wt, l, C_LF, 2 * GRANK, 2 * GRANK, F32, "proj_decay")

        wlr = jnp.zeros((2, 2 * GRANK, GH * GDK), F32)
        wlr = wlr.at[0, 0:GRANK].set(w_lr_f[l]).at[1, GRANK:2 * GRANK].set(w_lr_b[l])
        blr = jnp.stack([b_lr_f[l], b_lr_b[l]]).reshape(2, 1, GH * GDK)
        o_f, o_b = _gla_scan(_gla_prep(pa, lr, wlr, blr), pa)
        gla = _gla_readout(o_f, o_b, pa, g_gla[l])
        att = _attention(pb, pkv, att_sink[l])
        mm = _merge(gla, att, pb, w_br_gla, w_br_att, l, t0)

        xa, h2 = _outproj(mm, w_out, l, D, TM, t0=t0, a_ctx=False, x=xa, modl=modl, k_gate=2, name="outproj",
                          g_post=g_post_mix[l], g_next=g_pre_ffn[l], modn=modl, k_sc=4, k_sh=3)
        if not last:
            ones = jnp.ones((NT_ALL,), jnp.int32)
            u = _ffn_up(h2, w_ffn_gate[:, None], w_ffn_up[:, None], l // 2, 0 * ones, ones)
            xa, h = _outproj(u, w_ffn_down, l // 2, DFF, TM // 2, t0=0, a_ctx=True, x=xa, modl=modl, k_gate=5,
                             name="ffn_down", g_post=g_post_ffn[l], g_next=g_pre_mix[l + 1], modn=mod[l + 1],
                             k_sc=1, k_sh=0)
        else:
            comb_w, pos, sel = _router(xa, g_pre_ffn[l], modl, 4, 3, w_router[l // 2], b_router[l // 2])
            rt = _route_tables(pos, sel)
            xs, wrow = _dispatch(h2, rt["dest"], comb_w, rt["disp"])
            u = _ffn_up(xs, w_exp_gate, w_exp_up, l // 2, rt["te"], rt["act"])
            y = _moe_down(u, w_exp_down, l // 2, wrow, rt["te"], rt["act"])
            xa = _combine(y, rt["d_hi"], rt["d_lo"], xa, modl, 5, g_post_ffn[l], rt["comb"])
    return xa.reshape(B, L, D)
```

```python
import functools

import jax
import jax.numpy as jnp
import numpy as np
from jax import lax
from jax.experimental import pallas as pl
from jax.experimental.pallas import tpu as pltpu

BF = jnp.bfloat16
F32 = jnp.float32

D = 2048
B = 2
L = 4096
LC = 256
T = B * L
TC = B * LC
R = TC + T
DEPTH = 2
GRID_W = 64
EPS = 1e-6

GH = 4
GDK = 256
GDV = 512
GRANK = 16
GTAU = 16.0
GC = 64

HD = 64
HQ = 32
HKV = 4
ROPE_BASE = 10000.0

DFF = 5632
NE = 8

TM = 512
NT_ALL = R // TM
TPB = L // TM
VMEM_LIMIT = 56 * 1024 * 1024

C_GQ, C_GK, C_GV, C_GR, C_LF, C_LB, C_AQ, C_AK, C_AV, C_G1, C_G2, C_END = (
    0, 1024, 2048, 4096, 6144, 6160, 6176, 8224, 8480, 8736, 10784, 12832)


def _cparams(sem):
    return pltpu.CompilerParams(dimension_semantics=sem, vmem_limit_bytes=VMEM_LIMIT)


def _mrow(gi):
    return jnp.where(gi == 0, 2, (gi - 1) // TPB)


def _sigmoid(z):
    return 1.0 / (1.0 + jnp.exp(-z))


def _silu(z):
    return z * _sigmoid(z)


def _split(a):
    hi = a.astype(BF)
    lo = (a - hi.astype(F32)).astype(BF)
    return hi, lo


def _mod_kernel(c_ref, w_ref, b_ref, o_ref):
    a = _silu(c_ref[...]).astype(BF)
    o_ref[...] = jnp.dot(a, w_ref[...].astype(BF), preferred_element_type=F32) + b_ref[...]


def _modulation(cvec, w_mod, b_mod):
    tn = 1024
    return pl.pallas_call(
        _mod_kernel,
        grid=(DEPTH, 6 * D // tn),
        in_specs=[
            pl.BlockSpec((8, D), lambda l, j: (0, 0)),
            pl.BlockSpec((None, D, tn), lambda l, j: (l, 0, j)),
            pl.BlockSpec((None, 1, tn), lambda l, j: (l, 0, j)),
        ],
        out_specs=pl.BlockSpec((None, 8, tn), lambda l, j: (l, 0, j)),
        out_shape=jax.ShapeDtypeStruct((DEPTH, 8, 6 * D), F32),
        compiler_params=_cparams(("arbitrary", "arbitrary")),
        name="modulation",
    )(cvec, w_mod, b_mod.reshape(DEPTH, 1, 6 * D))


def _norm_mod(x, g, sc, sh):
    ms = jnp.mean(x * x, axis=-1, keepdims=True)
    return (x * lax.rsqrt(ms + EPS) * g) * (1.0 + sc) + sh


def _prenorm_kernel(ctx_ref, x_ref, g_ref, sc_ref, sh_ref, xa_ref, o_ref):
    def emit(src_ref):
        v = src_ref[...]
        xa_ref[...] = v
        o_ref[...] = _norm_mod(v, g_ref[...], sc_ref[...], sh_ref[...]).astype(BF)

    @pl.when(pl.program_id(0) == 0)
    def _():
        emit(ctx_ref)

    @pl.when(pl.program_id(0) > 0)
    def _():
        emit(x_ref)


def _prenorm(ctx2d, x2d, g, modl, k_sc, k_sh):
    row = pl.BlockSpec((TM, D), lambda i: (i, 0))
    return pl.pallas_call(
        _prenorm_kernel,
        grid=(NT_ALL,),
        in_specs=[
            pl.BlockSpec((TM, D), lambda i: (0, 0)),
            pl.BlockSpec((TM, D), lambda i: (jnp.maximum(i - 1, 0), 0)),
            pl.BlockSpec((1, D), lambda i: (0, 0)),
            pl.BlockSpec((None, 1, D), lambda i: (_mrow(i) * 6 + k_sc, 0, 0)),
            pl.BlockSpec((None, 1, D), lambda i: (_mrow(i) * 6 + k_sh, 0, 0)),
        ],
        out_specs=[row, row],
        out_shape=[jax.ShapeDtypeStruct((R, D), F32), jax.ShapeDtypeStruct((R, D), BF)],
        compiler_params=_cparams(("arbitrary",)),
        name="prenorm",
    )(ctx2d, x2d, g.reshape(1, D), modl, modl)


NT_DIMS = (((1,), (1,)), ((), ()))


def _proj_kernel(a_ref, wt_ref, o_ref, wbf_ref):
    @pl.when(pl.program_id(1) == 0)
    def _():
        wbf_ref[...] = wt_ref[...].astype(BF)

    acc = lax.dot_general(a_ref[...], wbf_ref[...], NT_DIMS, preferred_element_type=F32)
    o_ref[...] = acc.astype(o_ref.dtype)


def _proj(h, wt, l, row0, n_cols, tn, out_dtype, name):
    return pl.pallas_call(
        _proj_kernel,
        grid=(n_cols // tn, NT_ALL),
        in_specs=[
            pl.BlockSpec((TM, D), lambda j, i: (i, 0)),
            pl.BlockSpec((None, tn, D), lambda j, i: (l, row0 // tn + j, 0)),
        ],
        out_specs=pl.BlockSpec((TM, tn), lambda j, i: (i, j)),
        out_shape=jax.ShapeDtypeStruct((R, n_cols), out_dtype),
        scratch_shapes=[pltpu.VMEM((tn, D), BF)],
        compiler_params=_cparams(("arbitrary", "arbitrary")),
        name=name,
    )(h, wt)


PB_TN = 1024
PB_COLS = 2048 + 4096
PB_QT = 2048 // PB_TN
KV_COLS = 2 * HKV * HD
LOG2E = 1.4426950408889634
Q_SCALE = HD ** -0.5 * LOG2E


def _rope(x, cos, sin_a, sin_b):
    return x * cos + pltpu.roll(x, 112, axis=1) * sin_a + pltpu.roll(x, 16, axis=1) * sin_b


def _proj_att_kernel(a_ref, wt_ref, cos_ref, sa_ref, sb_ref, o_ref, wbf_ref, *, n_q, rope_cols):
    j = pl.program_id(0)
    i = pl.program_id(1)

    @pl.when(i == 0)
    def _():
        wbf_ref[...] = wt_ref[0].astype(BF)

    acc = lax.dot_general(a_ref[...], wbf_ref[...], NT_DIMS, preferred_element_type=F32)
    tn = acc.shape[1]
    latent = i > 0
    is_q = j < n_q

    def rotated(cols, scale):
        for s in range(cols // 128):
            y = _rope(acc[:, s * 128:(s + 1) * 128], cos_ref[...], sa_ref[...], sb_ref[...])
            o_ref[:, s * 128:(s + 1) * 128] = (y * scale).astype(BF)

    @pl.when(jnp.logical_and(is_q, latent))
    def _():
        rotated(tn, Q_SCALE)

    @pl.when(jnp.logical_and(is_q, jnp.logical_not(latent)))
    def _():
        o_ref[...] = (acc * Q_SCALE).astype(BF)

    plain = jnp.logical_not(is_q)
    if rope_cols:
        plain = jnp.logical_and(plain, jnp.logical_not(latent))

        @pl.when(jnp.logical_and(jnp.logical_not(is_q), latent))
        def _():
            rotated(rope_cols, 1.0)
            o_ref[:, rope_cols:] = acc[:, rope_cols:].astype(BF)

    @pl.when(plain)
    def _():
        o_ref[...] = acc.astype(BF)


def _proj_att_call(h, wt, l, tabs, *, tn, n_tiles, n_q, rope_cols, wrow_units, name):
    def tab(j, i):
        return (jnp.where(i == 0, 0, (i - 1) % TPB), 0)

    return pl.pallas_call(
        functools.partial(_proj_att_kernel, n_q=n_q, rope_cols=rope_cols),
        grid=(n_tiles, NT_ALL),
        in_specs=[
            pl.BlockSpec((TM, D), lambda j, i: (i, 0)),
            pl.BlockSpec((pl.Element(1), pl.Element(tn), pl.Element(D)),
                         lambda j, i: (l, wrow_units(j) * (2 * GRANK), 0)),
            pl.BlockSpec((TM, 128), tab),
            pl.BlockSpec((TM, 128), tab),
            pl.BlockSpec((TM, 128), tab),
        ],
        out_specs=pl.BlockSpec((TM, tn), lambda j, i: (i, j)),
        out_shape=jax.ShapeDtypeStruct((R, n_tiles * tn), BF),
        scratch_shapes=[pltpu.VMEM((tn, D), BF)],
        compiler_params=_cparams(("arbitrary", "arbitrary")),
        name=name,
    )(h, wt, *tabs)


def _proj_att(h, wt, l, tabs):
    u = 2 * GRANK
    pb = _proj_att_call(
        h, wt, l, tabs, tn=PB_TN, n_tiles=PB_COLS // PB_TN, n_q=PB_QT, rope_cols=0, name="proj_att",
        wrow_units=lambda j: jnp.where(j < PB_QT, C_AQ // u + j * (PB_TN // u),
                                       C_G1 // u + (j - PB_QT) * (PB_TN // u)))
    pkv = _proj_att_call(h, wt, l, tabs, tn=KV_COLS, n_tiles=1, n_q=0, rope_cols=HKV * HD, name="proj_kv",
                         wrow_units=lambda j: C_AK // u + j)
    return pb, pkv


def _rope_tables():
    rows = L // GRID_W
    row = np.repeat(np.arange(rows), GRID_W)
    col = np.tile(np.arange(GRID_W), rows)
    half = HD // 2
    inv = (ROPE_BASE ** (-np.arange(0, half, 2, dtype=np.float32) / half)).astype(np.float32)

    def angles(p):
        a = p.astype(np.float32)[:, None] * inv[None, :]
        return np.concatenate([a, a], axis=-1)

    ang = np.concatenate([angles(row), angles(col)], axis=-1)
    ang = np.concatenate([ang, ang], axis=-1)
    cos, sin = np.cos(ang).astype(np.float32), np.sin(ang).astype(np.float32)
    first = (np.arange(128) % 32) < 16
    zero = np.float32(0.0)
    return jnp.asarray(cos), jnp.asarray(np.where(first, -sin, zero)), jnp.asarray(np.where(first, zero, sin))


GG = 256
NG = 1 + L // GG


MASK_ROWS = 256


def _chunk_masks():
    r = np.arange(MASK_ROWS)[:, None]
    c = np.arange(MASK_ROWS)[None, :]
    same = (r // GC) == (c // GC)
    return jnp.asarray(np.stack([same & (r >= c), same & (r <= c), same]).astype(np.float32), dtype=BF)


def _chunk_sums(mask, a):
    hi, lo = _split(a)
    out = []
    for s in range(0, a.shape[0], MASK_ROWS):
        rows = slice(s, s + MASK_ROWS)
        out.append(jnp.dot(mask, hi[rows], preferred_element_type=F32)
                   + jnp.dot(mask, lo[rows], preferred_element_type=F32))
    return jnp.concatenate(out, axis=0)


def _gla_prep_kernel(lr_ref, wlr_ref, blr_ref, q_ref, k_ref, mask_ref, *out_refs):
    lh, ll = _split(lr_ref[...])
    qf = q_ref[...].astype(F32) * (GDK ** -0.5)
    kf = k_ref[...].astype(F32)
    for d in range(2):
        wh, wl = _split(wlr_ref[d])
        z = (jnp.dot(lh, wh, preferred_element_type=F32) + jnp.dot(ll, wh, preferred_element_type=F32)
             + jnp.dot(lh, wl, preferred_element_type=F32) + blr_ref[d])
        la = (jnp.minimum(z, 0.0) - jnp.log1p(jnp.exp(-jnp.abs(z)))) * (1.0 / GTAU)
        bcum = _chunk_sums(mask_ref[d], la)
        tot = _chunk_sums(mask_ref[2], la)
        qi_ref, ki_ref, ko_ref, dec_ref = out_refs[4 * d:4 * d + 4]
        qi_ref[...] = (qf * jnp.exp(bcum)).astype(BF)
        ki_ref[...] = (kf * jnp.exp(-bcum)).astype(BF)
        ko_ref[...] = (kf * jnp.exp(tot - bcum)).astype(BF)
        dec_ref[...] = jnp.exp(tot)


PREP_HEADS = 4


def _gla_prep(pa, lr, wlr, blr):
    hk = GH * GDK
    pw = PREP_HEADS * GDK
    ospec = pl.BlockSpec((TM, pw), lambda i, h: (i, h))
    return pl.pallas_call(
        _gla_prep_kernel,
        grid=(NT_ALL, GH // PREP_HEADS),
        in_specs=[
            pl.BlockSpec((TM, 2 * GRANK), lambda i, h: (i, 0)),
            pl.BlockSpec((2, 2 * GRANK, pw), lambda i, h: (0, 0, h)),
            pl.BlockSpec((2, 1, pw), lambda i, h: (0, 0, h)),
            pl.BlockSpec((TM, pw), lambda i, h: (i, C_GQ // pw + h)),
            pl.BlockSpec((TM, pw), lambda i, h: (i, C_GK // pw + h)),
            pl.BlockSpec((3, MASK_ROWS, MASK_ROWS), lambda i, h: (0, 0, 0)),
        ],
        out_specs=[ospec] * 8,
        out_shape=[jax.ShapeDtypeStruct((R, hk), BF)] * 3 + [jax.ShapeDtypeStruct((R, hk), F32)]
                  + [jax.ShapeDtypeStruct((R, hk), BF)] * 3 + [jax.ShapeDtypeStruct((R, hk), F32)],
        compiler_params=_cparams(("arbitrary", "arbitrary")),
        name="gla_prep",
    )(lr, wlr, blr, pa, pa, _chunk_masks())


HP = 4


def _gla_scan_kernel(qf_ref, kif_ref, kof_ref, df_ref, vf_ref, qb_ref, kib_ref, kob_ref, db_ref, vb_ref,
                     of_ref, ob_ref, s_ref):
    @pl.when(pl.program_id(2) == 0)
    def _():
        s_ref[...] = jnp.zeros_like(s_ref)

    ri = lax.broadcasted_iota(jnp.int32, (GC, GC), 0)
    ci = lax.broadcasted_iota(jnp.int32, (GC, GC), 1)
    tn = (((0,), (0,)), ((), ()))
    dirs = ((qf_ref, kif_ref, kof_ref, df_ref, vf_ref, of_ref, ri >= ci, range(GG // GC)),
            (qb_ref, kib_ref, kob_ref, db_ref, vb_ref, ob_ref, ri <= ci, reversed(range(GG // GC))))
    for d, (q_ref, ki_ref, ko_ref, dec_ref, v_ref, o_ref, keep, order) in enumerate(dirs):
        for c in order:
            rows = slice(c * GC, (c + 1) * GC)
            for hh in range(HP):
                kc = slice(hh * GDK, (hh + 1) * GDK)
                vc = slice(hh * GDV, (hh + 1) * GDV)
                q_in = q_ref[rows, kc]
                v = v_ref[rows, vc]
                a = lax.dot_general(q_in, ki_ref[rows, kc], NT_DIMS, preferred_element_type=F32)
                a = jnp.where(keep, a, 0.0).astype(BF)
                st = s_ref[d, hh]
                o = jnp.dot(a, v, preferred_element_type=F32)
                o = o + lax.dot_general(q_in, st.astype(BF), NT_DIMS, preferred_element_type=F32)
                o_ref[rows, vc] = o
                upd = lax.dot_general(v, ko_ref[rows, kc], tn, preferred_element_type=F32)
                s_ref[d, hh] = st * dec_ref[c * GC:c * GC + 1, kc] + upd


def _gla_scan(prep, pa):
    lat0 = TC // GG

    def spec(width, d, col0):
        def index(b, p, g):
            lat = (g - 1) if d == 0 else (NG - 1 - g)
            return (jnp.where(g == 0, b, lat0 + b * (L // GG) + lat), col0 + p)
        return pl.BlockSpec((GG, width), index)

    in_specs = []
    for d in range(2):
        in_specs += [spec(HP * GDK, d, 0)] * 4 + [spec(HP * GDV, d, C_GV // (HP * GDV))]
    out_specs = [spec(HP * GDV, d, 0) for d in range(2)]
    return pl.pallas_call(
        _gla_scan_kernel,
        grid=(B, GH // HP, NG),
        in_specs=in_specs,
        out_specs=out_specs,
        out_shape=[jax.ShapeDtypeStruct((R, GH * GDV), F32)] * 2,
        scratch_shapes=[pltpu.VMEM((2, HP, GDV, GDK), F32)],
        compiler_params=_cparams(("arbitrary", "arbitrary", "arbitrary")),
        name="gla_scan",
    )(*prep[0:4], pa, *prep[4:8], pa)


def _gla_readout_kernel(of_ref, ob_ref, r_ref, gg_ref, o_ref):
    for h in range(GH):
        cols = slice(h * GDV, (h + 1) * GDV)
        o = of_ref[:, cols] + ob_ref[:, cols]
        on = o * lax.rsqrt(jnp.mean(o * o, axis=-1, keepdims=True) + EPS) * gg_ref[...]
        o_ref[:, cols] = (on * _silu(r_ref[:, cols].astype(F32))).astype(BF)


def _gla_readout(o_f, o_b, pa, g_gla):
    spec = pl.BlockSpec((TM, GH * GDV), lambda i: (i, 0))
    return pl.pallas_call(
        _gla_readout_kernel,
        grid=(NT_ALL,),
        in_specs=[spec, spec, pl.BlockSpec((TM, GH * GDV), lambda i: (i, C_GR // (GH * GDV))),
                  pl.BlockSpec((1, GDV), lambda i: (0, 0))],
        out_specs=spec,
        out_shape=jax.ShapeDtypeStruct((R, GH * GDV), BF),
        compiler_params=_cparams(("arbitrary",)),
        name="gla_readout",
    )(o_f, o_b, pa, g_gla.reshape(1, GDV))


AB = 128
NB = L // AB
NCB = LC // AB
NEG = float("-inf")
QSTACK = 8


def _attn_kernel(q_ref, kvp_ref, kvc_ref, kvn_ref, kvx_ref, sink_ref, o_ref):
    j = pl.program_id(1)
    latent = j >= NCB
    ri = lax.broadcasted_iota(jnp.int32, (AB, AB), 0)
    ci = lax.broadcasted_iota(jnp.int32, (AB, AB), 1)
    bias_p = jnp.where(jnp.logical_and(ci >= ri, j > NCB), 0.0, NEG)
    bias_c = jnp.where(latent, 0.0, NEG)
    bias_n = jnp.where(jnp.logical_and(ci <= ri, jnp.logical_and(latent, j < NCB + NB - 1)), 0.0, NEG)
    gsz = HQ // HKV
    for hk in range(HKV):
        ks = slice(hk * HD, (hk + 1) * HD)
        vs = slice(HKV * HD + hk * HD, HKV * HD + (hk + 1) * HD)
        k_all = jnp.concatenate([kvp_ref[:, ks], kvc_ref[:, ks], kvn_ref[:, ks], kvx_ref[:, ks]], axis=0)
        v_all = jnp.concatenate([kvp_ref[:, vs], kvc_ref[:, vs], kvn_ref[:, vs], kvx_ref[:, vs]], axis=0)
        for h0 in range(hk * gsz, (hk + 1) * gsz, QSTACK):
            qs = jnp.concatenate([q_ref[:, (h0 + g) * HD:(h0 + g + 1) * HD] for g in range(QSTACK)], axis=0)
            s_all = lax.dot_general(qs, k_all, NT_DIMS, preferred_element_type=F32)
            ps, dens = [], []
            for g in range(QSTACK):
                sg = s_all[g * AB:(g + 1) * AB]
                sg = jnp.concatenate([sg[:, 0:AB] + bias_p, sg[:, AB:2 * AB] + bias_c,
                                      sg[:, 2 * AB:3 * AB] + bias_n, sg[:, 3 * AB:]], axis=1)
                snk = sink_ref[h0 + g] * LOG2E
                m = jnp.maximum(jnp.max(sg, axis=-1, keepdims=True), snk)
                e = jnp.exp2(sg - m)
                dens.append(jnp.sum(e, axis=-1, keepdims=True) + jnp.exp2(snk - m))
                ps.append(e.astype(BF))
            o_all = jnp.dot(jnp.concatenate(ps, axis=0), v_all, preferred_element_type=F32)
            for g in range(0, QSTACK, 2):
                o2 = jnp.concatenate([o_all[g * AB:(g + 1) * AB] / dens[g],
                                      o_all[(g + 1) * AB:(g + 2) * AB] / dens[g + 1]], axis=1)
                o_ref[:, (h0 + g) * HD:(h0 + g + 2) * HD] = o2.astype(BF)


def _attention(pb, pkv, sink):
    lat0 = TC // AB

    def qrow(b, j):
        return jnp.where(j < NCB, b * NCB + j, lat0 + b * NB + j - NCB)

    def krow(off):
        def f(b, j):
            i = jnp.clip(j - NCB + off, 0, NB - 1)
            return (lat0 + b * NB + i, 0)
        return f

    return pl.pallas_call(
        _attn_kernel,
        grid=(B, NCB + NB),
        in_specs=[
            pl.BlockSpec((AB, HQ * HD), lambda b, j: (qrow(b, j), 0)),
            pl.BlockSpec((AB, KV_COLS), krow(-1)),
            pl.BlockSpec((AB, KV_COLS), krow(0)),
            pl.BlockSpec((AB, KV_COLS), krow(1)),
            pl.BlockSpec((LC, KV_COLS), lambda b, j: (b, 0)),
            pl.BlockSpec(memory_space=pltpu.SMEM),
        ],
        out_specs=pl.BlockSpec((AB, HQ * HD), lambda b, j: (qrow(b, j), 0)),
        out_shape=jax.ShapeDtypeStruct((R, HQ * HD), BF),
        compiler_params=_cparams(("arbitrary", "arbitrary")),
        name="attention",
    )(pb, pkv, pkv, pkv, pkv, sink)


MG_TN = 512


def _merge_kernel(gla_ref, att_ref, g1_ref, g2_ref, w1_ref, w2_ref, o_ref, w1b_ref, w2b_ref):
    @pl.when(pl.program_id(1) == 0)
    def _():
        w1b_ref[...] = w1_ref[...].astype(BF)
        w2b_ref[...] = w2_ref[...].astype(BF)

    y1 = jnp.dot(gla_ref[...], w1b_ref[...], preferred_element_type=F32)
    y2 = jnp.dot(att_ref[...], w2b_ref[...], preferred_element_type=F32)
    y = _sigmoid(g1_ref[...].astype(F32)) * y1 + _sigmoid(g2_ref[...].astype(F32)) * y2
    o_ref[...] = y.astype(BF)


def _merge(gla, att, pb, w1, w2, l, t0):
    nt = NT_ALL - t0
    g1c = 2048 // MG_TN
    g2c = 4096 // MG_TN
    return pl.pallas_call(
        _merge_kernel,
        grid=(D // MG_TN, nt),
        in_specs=[
            pl.BlockSpec((TM, GH * GDV), lambda j, i: (i + t0, 0)),
            pl.BlockSpec((TM, HQ * HD), lambda j, i: (i + t0, 0)),
            pl.BlockSpec((TM, MG_TN), lambda j, i: (i + t0, g1c + j)),
            pl.BlockSpec((TM, MG_TN), lambda j, i: (i + t0, g2c + j)),
            pl.BlockSpec((None, GH * GDV, MG_TN), lambda j, i: (l, 0, j)),
            pl.BlockSpec((None, HQ * HD, MG_TN), lambda j, i: (l, 0, j)),
        ],
        out_specs=pl.BlockSpec((TM, MG_TN), lambda j, i: (i, j)),
        out_shape=jax.ShapeDtypeStruct((nt * TM, D), BF),
        scratch_shapes=[pltpu.VMEM((GH * GDV, MG_TN), BF), pltpu.VMEM((HQ * HD, MG_TN), BF)],
        compiler_params=_cparams(("arbitrary", "arbitrary")),
        name="merge",
    )(gla, att, pb, pb, w1, w2)


def _residual_norm(y, x, gate, g_post):
    ms = jnp.mean(y * y, axis=-1, keepdims=True)
    return x + gate * (y * lax.rsqrt(ms + EPS) * g_post)


WO_CK = 512


def _outproj_kernel(a_ref, w_ref, x_ref, gt_ref, gp_ref, gn_ref, sc_ref, sh_ref, o_ref, h_ref, wbf_ref, *, nch):
    s = pl.program_id(0)

    @pl.when(s < nch)
    def _():
        wbf_ref[pl.ds(pl.multiple_of(s * WO_CK, WO_CK), WO_CK), :] = w_ref[...].astype(BF)

    @pl.when(s >= nch - 1)
    def _():
        y = jnp.dot(a_ref[...], wbf_ref[...], preferred_element_type=F32)
        xn = _residual_norm(y, x_ref[...], gt_ref[...], gp_ref[...])
        o_ref[...] = xn
        h_ref[...] = _norm_mod(xn, gn_ref[...], sc_ref[...], sh_ref[...]).astype(BF)


def _outproj(a, w_full, l, kdim, tmr, *, t0, a_ctx, x, modl, k_gate, g_post, g_next, modn, k_sc, k_sh, name):
    nch = kdim // WO_CK
    skip = t0 * TM // tmr
    nt = R // tmr - skip
    a_off = skip if a_ctx else 0

    def tile(s):
        return jnp.maximum(s - (nch - 1), 0)

    def mrow(s):
        first = (tile(s) + skip) * tmr
        return jnp.where(first < TC, 2, (first - TC) // L)

    def mspec(k_chunk):
        return pl.BlockSpec((None, 1, D), lambda s: (mrow(s) * 6 + k_chunk, 0, 0))

    vec = pl.BlockSpec((1, D), lambda s: (0, 0))
    row = pl.BlockSpec((tmr, D), lambda s: (tile(s), 0))
    return pl.pallas_call(
        functools.partial(_outproj_kernel, nch=nch),
        grid=(nt + nch - 1,),
        in_specs=[
            pl.BlockSpec((tmr, kdim), lambda s: (tile(s) + a_off, 0)),
            pl.BlockSpec((None, WO_CK, D), lambda s: (l, jnp.minimum(s, nch - 1), 0)),
            pl.BlockSpec((tmr, D), lambda s: (tile(s) + skip, 0)),
            mspec(k_gate), vec, vec, mspec(k_sc), mspec(k_sh),
        ],
        out_specs=[row, row],
        out_shape=[jax.ShapeDtypeStruct((nt * tmr, D), F32), jax.ShapeDtypeStruct((nt * tmr, D), BF)],
        scratch_shapes=[pltpu.VMEM((kdim, D), BF)],
        compiler_params=_cparams(("arbitrary",)),
        name=name,
    )(a, w_full, x, modl, g_post.reshape(1, D), g_next.reshape(1, D), modn, modn)


FF_TN = 1024


def _new_expert(te_ref, i):
    return jnp.logical_or(i == 0, te_ref[i] != te_ref[jnp.maximum(i - 1, 0)])


def _ffn_up_kernel(te_ref, act_ref, a_ref, wg_ref, wu_ref, o_ref, wgb_ref, wub_ref):
    i = pl.program_id(1)

    @pl.when(_new_expert(te_ref, i))
    def _():
        wgb_ref[...] = wg_ref[...].astype(BF)
        wub_ref[...] = wu_ref[...].astype(BF)

    @pl.when(act_ref[i] == 1)
    def _():
        a = a_ref[...]
        yg = jnp.dot(a, wgb_ref[...], preferred_element_type=F32)
        yu = jnp.dot(a, wub_ref[...], preferred_element_type=F32)
        o_ref[...] = (_silu(yg) * yu).astype(BF)

    @pl.when(act_ref[i] == 0)
    def _():
        o_ref[...] = jnp.zeros_like(o_ref)


def _ffn_up(a, wg, wu, lead, te, act):
    nt = a.shape[0] // TM
    wspec = pl.BlockSpec((None, None, D, FF_TN), lambda j, i, te, act: (lead, te[i], 0, j))
    return pl.pallas_call(
        _ffn_up_kernel,
        grid_spec=pltpu.PrefetchScalarGridSpec(
            num_scalar_prefetch=2,
            grid=(pl.cdiv(DFF, FF_TN), nt),
            in_specs=[pl.BlockSpec((TM, D), lambda j, i, te, act: (i, 0)), wspec, wspec],
            out_specs=pl.BlockSpec((TM, FF_TN), lambda j, i, te, act: (i, j)),
            scratch_shapes=[pltpu.VMEM((D, FF_TN), BF), pltpu.VMEM((D, FF_TN), BF)],
        ),
        out_shape=jax.ShapeDtypeStruct((nt * TM, DFF), BF),
        compiler_params=_cparams(("arbitrary", "arbitrary")),
        name="ffn_up",
    )(te, act, a, wg, wu)


DN_TN = 512


def _moe_down_kernel(te_ref, act_ref, a_ref, w_ref, s_ref, o_ref, wb_ref):
    i = pl.program_id(1)

    @pl.when(_new_expert(te_ref, i))
    def _():
        wb_ref[...] = w_ref[...].astype(BF)

    @pl.when(act_ref[i] == 1)
    def _():
        y = jnp.dot(a_ref[...], wb_ref[...], preferred_element_type=F32)
        o_ref[...] = (y * s_ref[...]).astype(BF)

    @pl.when(act_ref[i] == 0)
    def _():
        o_ref[...] = jnp.zeros_like(o_ref)


def _moe_down(u, wd, lead, wrow, te, act):
    nt = u.shape[0] // TM
    return pl.pallas_call(
        _moe_down_kernel,
        grid_spec=pltpu.PrefetchScalarGridSpec(
            num_scalar_prefetch=2,
            grid=(D // DN_TN, nt),
            in_specs=[
                pl.BlockSpec((TM, DFF), lambda j, i, te, act: (i, 0)),
                pl.BlockSpec((None, None, DFF, DN_TN), lambda j, i, te, act: (lead, te[i], 0, j)),
                pl.BlockSpec((TM, 1), lambda j, i, te, act: (i, 0)),
            ],
            out_specs=pl.BlockSpec((TM, DN_TN), lambda j, i, te, act: (i, j)),
            scratch_shapes=[pltpu.VMEM((DFF, DN_TN), BF)],
        ),
        out_shape=jax.ShapeDtypeStruct((nt * TM, D), BF),
        compiler_params=_cparams(("arbitrary", "arbitrary")),
        name="moe_down",
    )(te, act, u, wd, wrow)


def _router_kernel(x_ref, g_ref, sc_ref, sh_ref, wr_ref, br_ref, o_ref, pos_ref, sel_ref, carry_ref):
    @pl.when(pl.program_id(0) == 0)
    def _():
        carry_ref[...] = jnp.zeros_like(carry_ref)

    h = _norm_mod(x_ref[...], g_ref[...], sc_ref[...], sh_ref[...])
    hh, hl = _split(h)
    wh, wl = _split(wr_ref[...])
    nt = (((1,), (1,)), ((), ()))
    lg = (lax.dot_general(wh, hh, nt, preferred_element_type=F32)
          + lax.dot_general(wh, hl, nt, preferred_element_type=F32)
          + lax.dot_general(wl, hh, nt, preferred_element_type=F32) + br_ref[...])
    idx = lax.broadcasted_iota(jnp.int32, lg.shape, 0)
    m1 = jnp.max(lg, axis=0, keepdims=True)
    i1 = jnp.min(jnp.where(lg == m1, idx, NE), axis=0, keepdims=True)
    l2 = jnp.where(idx == i1, NEG, lg)
    m2 = jnp.max(l2, axis=0, keepdims=True)
    i2 = jnp.min(jnp.where(l2 == m2, idx, NE), axis=0, keepdims=True)
    e2 = jnp.exp(m2 - m1)
    w1 = 1.0 / (1.0 + e2)
    w2 = e2 / (1.0 + e2)
    o_ref[...] = jnp.where(idx == i1, w1, 0.0) + jnp.where(idx == i2, w2, 0.0)
    sel = jnp.where(idx == i1, 1.0, jnp.where(idx == i2, 1.0, 0.0))
    si = lax.broadcasted_iota(jnp.int32, (TM, TM), 0)
    ti = lax.broadcasted_iota(jnp.int32, (TM, TM), 1)
    before = jnp.where(si < ti, 1.0, 0.0).astype(BF)
    excl = jnp.dot(sel.astype(BF), before, preferred_element_type=F32)
    pos_ref[...] = (excl + carry_ref[...]).astype(jnp.int32)
    sel_ref[...] = sel.astype(jnp.int32)
    carry_ref[...] += jnp.sum(sel, axis=1, keepdims=True)


def _router(x_lat, g, modl, k_sc, k_sh, w_router, b_router):
    ospec = pl.BlockSpec((NE, TM), lambda i: (0, i))
    return pl.pallas_call(
        _router_kernel,
        grid=(T // TM,),
        in_specs=[
            pl.BlockSpec((TM, D), lambda i: (i, 0)),
            pl.BlockSpec((1, D), lambda i: (0, 0)),
            pl.BlockSpec((None, 1, D), lambda i: (_mrow(i + 1) * 6 + k_sc, 0, 0)),
            pl.BlockSpec((None, 1, D), lambda i: (_mrow(i + 1) * 6 + k_sh, 0, 0)),
            pl.BlockSpec((NE, D), lambda i: (0, 0)),
            pl.BlockSpec((NE, 1), lambda i: (0, 0)),
        ],
        out_specs=[ospec, ospec, ospec],
        out_shape=[jax.ShapeDtypeStruct((NE, T), F32), jax.ShapeDtypeStruct((NE, T), jnp.int32),
                   jax.ShapeDtypeStruct((NE, T), jnp.int32)],
        scratch_shapes=[pltpu.VMEM((NE, 1), F32)],
        compiler_params=_cparams(("arbitrary",)),
        name="router",
    )(x_lat, g.reshape(1, D), modl, modl, w_router.T, b_router.reshape(NE, 1))


NP = 2 * T // TM + NE
DT = 256
NDT = NP * TM // DT
WSTEP = 1408
WTOK = WSTEP + 128
NITEM = 136
CT = 256
NCT = T // CT
ROW_ALIGN = 16
WIN = CT + ROW_ALIGN
WBUF = CT + 128


def _route_tables(pos, sel):
    i32 = jnp.int32
    counts = pos[:, -1] + sel[:, -1]
    ntile = (counts + TM - 1) // TM
    tend = jnp.cumsum(ntile)
    seg = (tend - ntile) * TM
    dest = jnp.where(sel > 0, seg[:, None] + pos, -1)
    d_hi = jnp.max(dest, axis=0)
    d_lo = jnp.sum(dest, axis=0) + (NE - 2) - d_hi
    tiles = jnp.arange(NP, dtype=i32)
    total = tend[-1]
    te = jnp.sum((tiles[:, None] >= tend[None, :]).astype(i32), axis=1)
    te_last = jnp.sum(((total - 1) >= tend).astype(i32))
    act = (tiles < total).astype(i32)
    te = jnp.where(act > 0, te, te_last)
    dtile = jnp.arange(NDT, dtype=i32)
    de = te[dtile // (TM // DT)]
    p_lo = dtile * DT - seg[de]
    has = jnp.logical_and(dtile * DT < total * TM, p_lo < counts[de])
    p_end = jnp.minimum(p_lo + DT, counts[de])
    incl = jnp.take(pos + sel, de, axis=0)
    t_first = jnp.sum((incl <= p_lo[:, None]).astype(i32), axis=1)
    t_last = jnp.sum((incl <= (p_end - 1)[:, None]).astype(i32), axis=1)
    w0 = (t_first // 128) * 128
    nw = jnp.where(has, (t_last + 1 - w0 + WSTEP - 1) // WSTEP, 1)
    cum = jnp.cumsum(nw)
    item = jnp.minimum(jnp.arange(NITEM, dtype=i32), cum[-1] - 1)
    real = jnp.arange(NITEM, dtype=i32) < cum[-1]
    it_tile = jnp.sum((cum[None, :] <= item[:, None]).astype(i32), axis=1)
    k = item - (cum - nw)[it_tile]
    lo = w0[it_tile] + k * WSTEP
    disp = dict(tile=it_tile, expert=de[it_tile], lo=lo,
                tok=jnp.minimum(lo, T - WTOK) // 128,
                valid=jnp.logical_and(real, has[it_tile]).astype(i32),
                first=jnp.logical_and(real, k == 0).astype(i32))
    pbc = pos[:, ::CT]
    r0c = (seg[:, None] + pbc).T.reshape(NCT * NE)
    cntc = (jnp.concatenate([pbc[:, 1:], counts[:, None]], axis=1) - pbc).T.reshape(NCT * NE)
    a0c = jnp.minimum((r0c // ROW_ALIGN) * ROW_ALIGN, NP * TM - WIN)
    comb = dict(a0=a0c, r0=r0c, cnt=cntc)
    return dict(dest=dest, d_hi=d_hi.reshape(T, 1), d_lo=d_lo.reshape(T, 1), te=te, act=act,
                disp=disp, comb=comb)


def _dispatch_kernel(tile_ref, exp_ref, lo_ref, tok_ref, val_ref, first_ref, h_ref, dest_ref, cw_ref,
                     xs_ref, wr_ref):
    i = pl.program_id(0)

    @pl.when(first_ref[i] == 1)
    def _():
        xs_ref[...] = jnp.zeros_like(xs_ref)
        wr_ref[...] = jnp.zeros_like(wr_ref)

    @pl.when(val_ref[i] == 1)
    def _():
        e = exp_ref[i]
        lo = lo_ref[i]
        tok = lax.broadcasted_iota(jnp.int32, (1, WTOK), 1) + tok_ref[i] * 128
        mine = jnp.logical_and(tok >= lo, tok < lo + WSTEP)
        drow = jnp.where(mine, dest_ref[pl.ds(e, 1), :], -1)
        crow = cw_ref[pl.ds(e, 1), :]
        hit = drow == lax.broadcasted_iota(jnp.int32, (DT, WTOK), 0) + tile_ref[i] * DT
        onehot = jnp.where(hit, 1.0, 0.0).astype(BF)
        g = jnp.dot(onehot, h_ref[...], preferred_element_type=F32)
        xs_ref[...] = (xs_ref[...].astype(F32) + g).astype(BF)
        wr_ref[...] += jnp.sum(jnp.where(hit, crow, 0.0), axis=1, keepdims=True)


def _dispatch(h_lat, dest, comb_w, tb):
    def omap(i, tile, *_):
        return (tile[i], 0)

    def rows(i, tile, exp, lo, tok, *_):
        return (tok[i] * 128, 0)

    def lanes(i, tile, exp, lo, tok, *_):
        return (0, tok[i] * 128)

    return pl.pallas_call(
        _dispatch_kernel,
        grid_spec=pltpu.PrefetchScalarGridSpec(
            num_scalar_prefetch=6,
            grid=(NITEM,),
            in_specs=[
                pl.BlockSpec((pl.Element(WTOK), pl.Element(D)), rows),
                pl.BlockSpec((pl.Element(NE), pl.Element(WTOK)), lanes),
                pl.BlockSpec((pl.Element(NE), pl.Element(WTOK)), lanes),
            ],
            out_specs=[pl.BlockSpec((DT, D), omap), pl.BlockSpec((DT, 1), omap)],
        ),
        out_shape=[jax.ShapeDtypeStruct((NP * TM, D), BF), jax.ShapeDtypeStruct((NP * TM, 1), F32)],
        compiler_params=_cparams(("arbitrary",)),
        name="moe_dispatch",
    )(tb["tile"], tb["expert"], tb["lo"], tb["tok"], tb["valid"], tb["first"], h_lat, dest, comb_w)


def _combine_kernel(a0_ref, r0_ref, cnt_ref, y_hbm, dhi_ref, dlo_ref, x_ref, gt_ref, gp_ref, o_ref,
                    ybuf, acc_ref, sem):
    i = pl.program_id(0)
    slot = lax.rem(i, 2)

    def window_copies(tile, buf):
        out = []
        for e in range(NE):
            a0 = pl.multiple_of(a0_ref[tile * NE + e], ROW_ALIGN)
            out.append(pltpu.make_async_copy(y_hbm.at[pl.ds(a0, WIN), :], ybuf.at[buf, e, pl.ds(0, WIN), :],
                                             sem.at[buf, e]))
        return out

    @pl.when(i == 0)
    def _():
        ybuf[:, :, WIN:, :] = jnp.zeros((2, NE, WBUF - WIN, D), BF)
        for cp in window_copies(0, 0):
            cp.start()

    @pl.when(i + 1 < pl.num_programs(0))
    def _():
        for cp in window_copies(i + 1, 1 - slot):
            cp.start()

    for cp in window_copies(i, slot):
        cp.wait()

    dhi = dhi_ref[...]
    dlo = dlo_ref[...]

    def onehot(e, c0, width):
        idx = i * NE + e
        r0 = r0_ref[idx]
        ids = lax.broadcasted_iota(jnp.int32, (1, width), 1) + (a0_ref[idx] + c0)
        ids = jnp.where(jnp.logical_and(ids >= r0, ids < r0 + cnt_ref[idx]), ids, -1)
        return jnp.where(dhi == ids, 1.0, jnp.where(dlo == ids, 1.0, 0.0)).astype(BF)

    acc = jnp.zeros((CT, D), F32)
    for e in range(NE):
        acc = acc + jnp.dot(onehot(e, 0, CT), ybuf[slot, e, 0:CT, :], preferred_element_type=F32)
    acc_ref[...] = acc
    for e in range(NE):
        idx = i * NE + e

        @pl.when(r0_ref[idx] + cnt_ref[idx] > a0_ref[idx] + CT)
        def _():
            acc_ref[...] += jnp.dot(onehot(e, CT, WBUF - CT), ybuf[slot, e, CT:, :], preferred_element_type=F32)

    o_ref[...] = _residual_norm(acc_ref[...], x_ref[...], gt_ref[...], gp_ref[...])


def _combine(y, d_hi, d_lo, x_lat, modl, k_gate, g_post, tb):
    return pl.pallas_call(
        _combine_kernel,
        grid_spec=pltpu.PrefetchScalarGridSpec(
            num_scalar_prefetch=3,
            grid=(NCT,),
            in_specs=[
                pl.BlockSpec(memory_space=pl.ANY),
                pl.BlockSpec((CT, 1), lambda t, *_: (t, 0)),
                pl.BlockSpec((CT, 1), lambda t, *_: (t, 0)),
                pl.BlockSpec((CT, D), lambda t, *_: (t, 0)),
                pl.BlockSpec((None, 1, D), lambda t, *_: (_mrow(t // (TM // CT) + 1) * 6 + k_gate, 0, 0)),
                pl.BlockSpec((1, D), lambda t, *_: (0, 0)),
            ],
            out_specs=pl.BlockSpec((CT, D), lambda t, *_: (t, 0)),
            scratch_shapes=[pltpu.VMEM((2, NE, WBUF, D), BF), pltpu.VMEM((CT, D), F32),
                            pltpu.SemaphoreType.DMA((2, NE))],
        ),
        out_shape=jax.ShapeDtypeStruct((T, D), F32),
        compiler_params=_cparams(("arbitrary",)),
        name="moe_combine",
    )(tb["a0"], tb["r0"], tb["cnt"], y, d_hi, d_lo, x_lat, modl, g_post.reshape(1, D))


def kernel(x, c, ctx, c_ctx, w_mod, b_mod, g_pre_mix, g_post_mix, g_pre_ffn, g_post_ffn,
           w_in, w_lr_f, b_lr_f, w_lr_b, b_lr_b, g_gla, att_sink, w_br_gla, w_br_att, w_out,
           w_ffn_gate, w_ffn_up, w_ffn_down, w_router, b_router, w_exp_gate, w_exp_up, w_exp_down):
    cvec = jnp.zeros((8, D), F32).at[0:B].set(c).at[B].set(c_ctx)
    mod = _modulation(cvec, w_mod, b_mod).reshape(DEPTH, 8 * 6, 1, D)
    cos, sin_a, sin_b = _rope_tables()
    wt = jnp.swapaxes(w_in, 1, 2)

    xa, h = _prenorm(ctx.reshape(TC, D), x.reshape(T, D), g_pre_mix[0], mod[0], 1, 0)
    for l in range(DEPTH):
        last = l == DEPTH - 1
        t0 = 1 if last else 0
        modl = mod[l]
        pa = _proj(h, wt, l, 0, C_LF, 1024, BF, "proj_gla")
        pb, pkv = _proj_att(h, wt, l, (cos, sin_a, sin_b))
        lr = _proj(h, wt, l, C_LF, 2 * GRANK, 2 * GRANK, F32, "proj_decay")

        wlr = jnp.zeros((2, 2 * GRANK, GH * GDK), F32)
        wlr = wlr.at[0, 0:GRANK].set(w_lr_f[l]).at[1, GRANK:2 * GRANK].set(w_lr_b[l])
        blr = jnp.stack([b_lr_f[l], b_lr_b[l]]).reshape(2, 1, GH * GDK)
        o_f, o_b = _gla_scan(_gla_prep(pa, lr, wlr, blr), pa)
        gla = _gla_readout(o_f, o_b, pa, g_gla[l])
        att = _attention(pb, pkv, att_sink[l])
        mm = _merge(gla, att, pb, w_br_gla, w_br_att, l, t0)

        xa, h2 = _outproj(mm, w_out, l, D, TM, t0=t0, a_ctx=False, x=xa, modl=modl, k_gate=2, name="outproj",
                          g_post=g_post_mix[l], g_next=g_pre_ffn[l], modn=modl, k_sc=4, k_sh=3)
        if not last:
            ones = jnp.ones((NT_ALL,), jnp.int32)
            u = _ffn_up(h2, w_ffn_gate[:, None], w_ffn_up[:, None], l // 2, 0 * ones, ones)
            xa, h = _outproj(u, w_ffn_down, l // 2, DFF, TM // 2, t0=0, a_ctx=True, x=xa, modl=modl, k_gate=5,
                             name="ffn_down", g_post=g_post_ffn[l], g_next=g_pre_mix[l + 1], modn=mod[l + 1],
                             k_sc=1, k_sh=0)
        else:
            comb_w, pos, sel = _router(xa, g_pre_ffn[l], modl, 4, 3, w_router[l // 2], b_router[l // 2])
            rt = _route_tables(pos, sel)
            xs, wrow = _dispatch(h2, rt["dest"], comb_w, rt["disp"])
            u = _ffn_up(xs, w_exp_gate, w_exp_up, l // 2, rt["te"], rt["act"])
            y = _moe_down(u, w_exp_down, l // 2, wrow, rt["te"], rt["act"])
            xa = _combine(y, rt["d_hi"], rt["d_lo"], xa, modl, 5, g_post_ffn[l], rt["comb"])
    return xa.reshape(B, L, D)
```

```python
import functools

import jax
import jax.numpy as jnp
import numpy as np
from jax import lax
from jax.experimental import pallas as pl
from jax.experimental.pallas import tpu as pltpu

BF = jnp.bfloat16
F32 = jnp.float32

D = 2048
B = 2
L = 4096
LC = 256
T = B * L
TC = B * LC
R = TC + T
DEPTH = 2
GRID_W = 64
EPS = 1e-6

GH = 4
GDK = 256
GDV = 512
GRANK = 16
GTAU = 16.0
GC = 64

HD = 64
HQ = 32
HKV = 4
ROPE_BASE = 10000.0

DFF = 5632
NE = 8

TM = 512
NT_ALL = R // TM
TPB = L // TM
VMEM_LIMIT = 56 * 1024 * 1024

C_GQ, C_GK, C_GV, C_GR, C_LF, C_LB, C_AQ, C_AK, C_AV, C_G1, C_G2, C_END = (
    0, 1024, 2048, 4096, 6144, 6160, 6176, 8224, 8480, 8736, 10784, 12832)


def _cparams(sem):
    return pltpu.CompilerParams(dimension_semantics=sem, vmem_limit_bytes=VMEM_LIMIT)


def _mrow(gi):
    return jnp.where(gi == 0, 2, (gi - 1) // TPB)


def _sigmoid(z):
    return 1.0 / (1.0 + jnp.exp(-z))


def _silu(z):
    return z * _sigmoid(z)


def _split(a):
    hi = a.astype(BF)
    lo = (a - hi.astype(F32)).astype(BF)
    return hi, lo


def _mod_kernel(c_ref, w_ref, b_ref, o_ref):
    a = _silu(c_ref[...]).astype(BF)
    o_ref[...] = jnp.dot(a, w_ref[...].astype(BF), preferred_element_type=F32) + b_ref[...]


def _modulation(cvec, w_mod, b_mod):
    tn = 1024
    return pl.pallas_call(
        _mod_kernel,
        grid=(DEPTH, 6 * D // tn),
        in_specs=[
            pl.BlockSpec((8, D), lambda l, j: (0, 0)),
            pl.BlockSpec((None, D, tn), lambda l, j: (l, 0, j)),
            pl.BlockSpec((None, 1, tn), lambda l, j: (l, 0, j)),
        ],
        out_specs=pl.BlockSpec((None, 8, tn), lambda l, j: (l, 0, j)),
        out_shape=jax.ShapeDtypeStruct((DEPTH, 8, 6 * D), F32),
        compiler_params=_cparams(("arbitrary", "arbitrary")),
        name="modulation",
    )(cvec, w_mod, b_mod.reshape(DEPTH, 1, 6 * D))


def _norm_mod(x, g, sc, sh):
    ms = jnp.mean(x * x, axis=-1, keepdims=True)
    return (x * lax.rsqrt(ms + EPS) * g) * (1.0 + sc) + sh


def _prenorm_kernel(ctx_ref, x_ref, g_ref, sc_ref, sh_ref, xa_ref, o_ref):
    def emit(src_ref):
        v = src_ref[...]
        xa_ref[...] = v
        o_ref[...] = _norm_mod(v, g_ref[...], sc_ref[...], sh_ref[...]).astype(BF)

    @pl.when(pl.program_id(0) == 0)
    def _():
        emit(ctx_ref)

    @pl.when(pl.program_id(0) > 0)
    def _():
        emit(x_ref)


def _prenorm(ctx2d, x2d, g, modl, k_sc, k_sh):
    row = pl.BlockSpec((TM, D), lambda i: (i, 0))
    return pl.pallas_call(
        _prenorm_kernel,
        grid=(NT_ALL,),
        in_specs=[
            pl.BlockSpec((TM, D), lambda i: (0, 0)),
            pl.BlockSpec((TM, D), lambda i: (jnp.maximum(i - 1, 0), 0)),
            pl.BlockSpec((1, D), lambda i: (0, 0)),
            pl.BlockSpec((None, 1, D), lambda i: (_mrow(i) * 6 + k_sc, 0, 0)),
            pl.BlockSpec((None, 1, D), lambda i: (_mrow(i) * 6 + k_sh, 0, 0)),
        ],
        out_specs=[row, row],
        out_shape=[jax.ShapeDtypeStruct((R, D), F32), jax.ShapeDtypeStruct((R, D), BF)],
        compiler_params=_cparams(("arbitrary",)),
        name="prenorm",
    )(ctx2d, x2d, g.reshape(1, D), modl, modl)


NT_DIMS = (((1,), (1,)), ((), ()))


def _proj_kernel(a_ref, wt_ref, o_ref, wbf_ref):
    @pl.when(pl.program_id(1) == 0)
    def _():
        wbf_ref[...] = wt_ref[...].astype(BF)

    acc = lax.dot_general(a_ref[...], wbf_ref[...], NT_DIMS, preferred_element_type=F32)
    o_ref[...] = acc.astype(o_ref.dtype)


def _proj(h, wt, l, row0, n_cols, tn, out_dtype, name):
    return pl.pallas_call(
        _proj_kernel,
        grid=(n_cols // tn, NT_ALL),
        in_specs=[
            pl.BlockSpec((TM, D), lambda j, i: (i, 0)),
            pl.BlockSpec((None, tn, D), lambda j, i: (l, row0 // tn + j, 0)),
        ],
        out_specs=pl.BlockSpec((TM, tn), lambda j, i: (i, j)),
        out_shape=jax.ShapeDtypeStruct((R, n_cols), out_dtype),
        scratch_shapes=[pltpu.VMEM((tn, D), BF)],
        compiler_params=_cparams(("arbitrary", "arbitrary")),
        name=name,
    )(h, wt)


PB_TN = 1024
PB_COLS = 2048 + 4096
PB_QT = 2048 // PB_TN
KV_COLS = 2 * HKV * HD
LOG2E = 1.4426950408889634
Q_SCALE = HD ** -0.5 * LOG2E


def _rope(x, cos, sin_a, sin_b):
    return x * cos + pltpu.roll(x, 112, axis=1) * sin_a + pltpu.roll(x, 16, axis=1) * sin_b


def _proj_att_kernel(a_ref, wt_ref, cos_ref, sa_ref, sb_ref, o_ref, wbf_ref, *, n_q, rope_cols):
    j = pl.program_id(0)
    i = pl.program_id(1)

    @pl.when(i == 0)
    def _():
        wbf_ref[...] = wt_ref[0].astype(BF)

    acc = lax.dot_general(a_ref[...], wbf_ref[...], NT_DIMS, preferred_element_type=F32)
    tn = acc.shape[1]
    latent = i > 0
    is_q = j < n_q

    def rotated(cols, scale):
        for s in range(cols // 128):
            y = _rope(acc[:, s * 128:(s + 1) * 128], cos_ref[...], sa_ref[...], sb_ref[...])
            o_ref[:, s * 128:(s + 1) * 128] = (y * scale).astype(BF)

    @pl.when(jnp.logical_and(is_q, latent))
    def _():
        rotated(tn, Q_SCALE)

    @pl.when(jnp.logical_and(is_q, jnp.logical_not(latent)))
    def _():
        o_ref[...] = (acc * Q_SCALE).astype(BF)

    plain = jnp.logical_not(is_q)
    if rope_cols:
        plain = jnp.logical_and(plain, jnp.logical_not(latent))

        @pl.when(jnp.logical_and(jnp.logical_not(is_q), latent))
        def _():
            rotated(rope_cols, 1.0)
            o_ref[:, rope_cols:] = acc[:, rope_cols:].astype(BF)

    @pl.when(plain)
    def _():
        o_ref[...] = acc.astype(BF)


def _proj_att_call(h, wt, l, tabs, *, tn, n_tiles, n_q, rope_cols, wrow_units, name):
    def tab(j, i):
        return (jnp.where(i == 0, 0, (i - 1) % TPB), 0)

    return pl.pallas_call(
        functools.partial(_proj_att_kernel, n_q=n_q, rope_cols=rope_cols),
        grid=(n_tiles, NT_ALL),
        in_specs=[
            pl.BlockSpec((TM, D), lambda j, i: (i, 0)),
            pl.BlockSpec((pl.Element(1), pl.Element(tn), pl.Element(D)),
                         lambda j, i: (l, wrow_units(j) * (2 * GRANK), 0)),
            pl.BlockSpec((TM, 128), tab),
            pl.BlockSpec((TM, 128), tab),
            pl.BlockSpec((TM, 128), tab),
        ],
        out_specs=pl.BlockSpec((TM, tn), lambda j, i: (i, j)),
        out_shape=jax.ShapeDtypeStruct((R, n_tiles * tn), BF),
        scratch_shapes=[pltpu.VMEM((tn, D), BF)],
        compiler_params=_cparams(("arbitrary", "arbitrary")),
        name=name,
    )(h, wt, *tabs)


def _proj_att(h, wt, l, tabs):
    u = 2 * GRANK
    pb = _proj_att_call(
        h, wt, l, tabs, tn=PB_TN, n_tiles=PB_COLS // PB_TN, n_q=PB_QT, rope_cols=0, name="proj_att",
        wrow_units=lambda j: jnp.where(j < PB_QT, C_AQ // u + j * (PB_TN // u),
                                       C_G1 // u + (j - PB_QT) * (PB_TN // u)))
    pkv = _proj_att_call(h, wt, l, tabs, tn=KV_COLS, n_tiles=1, n_q=0, rope_cols=HKV * HD, name="proj_kv",
                         wrow_units=lambda j: C_AK // u + j)
    return pb, pkv


def _rope_tables():
    rows = L // GRID_W
    row = np.repeat(np.arange(rows), GRID_W)
    col = np.tile(np.arange(GRID_W), rows)
    half = HD // 2
    inv = (ROPE_BASE ** (-np.arange(0, half, 2, dtype=np.float32) / half)).astype(np.float32)

    def angles(p):
        a = p.astype(np.float32)[:, None] * inv[None, :]
        return np.concatenate([a, a], axis=-1)

    ang = np.concatenate([angles(row), angles(col)], axis=-1)
    ang = np.concatenate([ang, ang], axis=-1)
    cos, sin = np.cos(ang).astype(np.float32), np.sin(ang).astype(np.float32)
    first = (np.arange(128) % 32) < 16
    zero = np.float32(0.0)
    return jnp.asarray(cos), jnp.asarray(np.where(first, -sin, zero)), jnp.asarray(np.where(first, zero, sin))


GG = 256
NG = 1 + L // GG


MASK_ROWS = 256


def _chunk_masks():
    r = np.arange(MASK_ROWS)[:, None]
    c = np.arange(MASK_ROWS)[None, :]
    same = (r // GC) == (c // GC)
    return jnp.asarray(np.stack([same & (r >= c), same & (r <= c), same]).astype(np.float32), dtype=BF)


def _chunk_sums(mask, a):
    hi, lo = _split(a)
    out = []
    for s in range(0, a.shape[0], MASK_ROWS):
        rows = slice(s, s + MASK_ROWS)
        out.append(jnp.dot(mask, hi[rows], preferred_element_type=F32)
                   + jnp.dot(mask, lo[rows], preferred_element_type=F32))
    return jnp.concatenate(out, axis=0)


def _gla_prep_kernel(lr_ref, wlr_ref, blr_ref, q_ref, k_ref, mask_ref, *out_refs):
    lh, ll = _split(lr_ref[...])
    qf = q_ref[...].astype(F32) * (GDK ** -0.5)
    kf = k_ref[...].astype(F32)
    for d in range(2):
        wh, wl = _split(wlr_ref[d])
        z = (jnp.dot(lh, wh, preferred_element_type=F32) + jnp.dot(ll, wh, preferred_element_type=F32)
             + jnp.dot(lh, wl, preferred_element_type=F32) + blr_ref[d])
        la = (jnp.minimum(z, 0.0) - jnp.log1p(jnp.exp(-jnp.abs(z)))) * (1.0 / GTAU)
        bcum = _chunk_sums(mask_ref[d], la)
        tot = _chunk_sums(mask_ref[2], la)
        qi_ref, ki_ref, ko_ref, dec_ref = out_refs[4 * d:4 * d + 4]
        qi_ref[...] = (qf * jnp.exp(bcum)).astype(BF)
        ki_ref[...] = (kf * jnp.exp(-bcum)).astype(BF)
        ko_ref[...] = (kf * jnp.exp(tot - bcum)).astype(BF)
        dec_ref[...] = jnp.exp(tot)


PREP_HEADS = 4


def _gla_prep(pa, lr, wlr, blr):
    hk = GH * GDK
    pw = PREP_HEADS * GDK
    ospec = pl.BlockSpec((TM, pw), lambda i, h: (i, h))
    return pl.pallas_call(
        _gla_prep_kernel,
        grid=(NT_ALL, GH // PREP_HEADS),
        in_specs=[
            pl.BlockSpec((TM, 2 * GRANK), lambda i, h: (i, 0)),
            pl.BlockSpec((2, 2 * GRANK, pw), lambda i, h: (0, 0, h)),
            pl.BlockSpec((2, 1, pw), lambda i, h: (0, 0, h)),
            pl.BlockSpec((TM, pw), lambda i, h: (i, C_GQ // pw + h)),
            pl.BlockSpec((TM, pw), lambda i, h: (i, C_GK // pw + h)),
            pl.BlockSpec((3, MASK_ROWS, MASK_ROWS), lambda i, h: (0, 0, 0)),
        ],
        out_specs=[ospec] * 8,
        out_shape=[jax.ShapeDtypeStruct((R, hk), BF)] * 3 + [jax.ShapeDtypeStruct((R, hk), F32)]
                  + [jax.ShapeDtypeStruct((R, hk), BF)] * 3 + [jax.ShapeDtypeStruct((R, hk), F32)],
        compiler_params=_cparams(("arbitrary", "arbitrary")),
        name="gla_prep",
    )(lr, wlr, blr, pa, pa, _chunk_masks())


HP = 4


def _gla_scan_kernel(qf_ref, kif_ref, kof_ref, df_ref, vf_ref, qb_ref, kib_ref, kob_ref, db_ref, vb_ref,
                     of_ref, ob_ref, s_ref):
    @pl.when(pl.program_id(2) == 0)
    def _():
        s_ref[...] = jnp.zeros_like(s_ref)

    ri = lax.broadcasted_iota(jnp.int32, (GC, GC), 0)
    ci = lax.broadcasted_iota(jnp.int32, (GC, GC), 1)
    tn = (((0,), (0,)), ((), ()))
    dirs = ((qf_ref, kif_ref, kof_ref, df_ref, vf_ref, of_ref, ri >= ci, range(GG // GC)),
            (qb_ref, kib_ref, kob_ref, db_ref, vb_ref, ob_ref, ri <= ci, reversed(range(GG // GC))))
    for d, (q_ref, ki_ref, ko_ref, dec_ref, v_ref, o_ref, keep, order) in enumerate(dirs):
        for c in order:
            rows = slice(c * GC, (c + 1) * GC)
            for hh in range(HP):
                kc = slice(hh * GDK, (hh + 1) * GDK)
                vc = slice(hh * GDV, (hh + 1) * GDV)
                q_in = q_ref[rows, kc]
                v = v_ref[rows, vc]
                a = lax.dot_general(q_in, ki_ref[rows, kc], NT_DIMS, preferred_element_type=F32)
                a = jnp.where(keep, a, 0.0).astype(BF)
                st = s_ref[d, hh]
                o = jnp.dot(a, v, preferred_element_type=F32)
                o = o + lax.dot_general(q_in, st.astype(BF), NT_DIMS, preferred_element_type=F32)
                o_ref[rows, vc] = o
                upd = lax.dot_general(v, ko_ref[rows, kc], tn, preferred_element_type=F32)
                s_ref[d, hh] = st * dec_ref[c * GC:c * GC + 1, kc] + upd


def _gla_scan(prep, pa):
    lat0 = TC // GG

    def spec(width, d, col0):
        def index(b, p, g):
            lat = (g - 1) if d == 0 else (NG - 1 - g)
            return (jnp.where(g == 0, b, lat0 + b * (L // GG) + lat), col0 + p)
        return pl.BlockSpec((GG, width), index)

    in_specs = []
    for d in range(2):
        in_specs += [spec(HP * GDK, d, 0)] * 4 + [spec(HP * GDV, d, C_GV // (HP * GDV))]
    out_specs = [spec(HP * GDV, d, 0) for d in range(2)]
    return pl.pallas_call(
        _gla_scan_kernel,
        grid=(B, GH // HP, NG),
        in_specs=in_specs,
        out_specs=out_specs,
        out_shape=[jax.ShapeDtypeStruct((R, GH * GDV), F32)] * 2,
        scratch_shapes=[pltpu.VMEM((2, HP, GDV, GDK), F32)],
        compiler_params=_cparams(("arbitrary", "arbitrary", "arbitrary")),
        name="gla_scan",
    )(*prep[0:4], pa, *prep[4:8], pa)


def _gla_readout_kernel(of_ref, ob_ref, r_ref, gg_ref, o_ref):
    for h in range(GH):
        cols = slice(h * GDV, (h + 1) * GDV)
        o = of_ref[:, cols] + ob_ref[:, cols]
        on = o * lax.rsqrt(jnp.mean(o * o, axis=-1, keepdims=True) + EPS) * gg_ref[...]
        o_ref[:, cols] = (on * _silu(r_ref[:, cols].astype(F32))).astype(BF)


def _gla_readout(o_f, o_b, pa, g_gla):
    spec = pl.BlockSpec((TM, GH * GDV), lambda i: (i, 0))
    return pl.pallas_call(
        _gla_readout_kernel,
        grid=(NT_ALL,),
        in_specs=[spec, spec, pl.BlockSpec((TM, GH * GDV), lambda i: (i, C_GR // (GH * GDV))),
                  pl.BlockSpec((1, GDV), lambda i: (0, 0))],
        out_specs=spec,
        out_shape=jax.ShapeDtypeStruct((R, GH * GDV), BF),
        compiler_params=_cparams(("arbitrary",)),
        name="gla_readout",
    )(o_f, o_b, pa, g_gla.reshape(1, GDV))


AB = 128
NB = L // AB
NCB = LC // AB
NEG = float("-inf")
QSTACK = 8


def _attn_kernel(q_ref, kvp_ref, kvc_ref, kvn_ref, kvx_ref, sink_ref, o_ref):
    j = pl.program_id(1)
    latent = j >= NCB
    ri = lax.broadcasted_iota(jnp.int32, (AB, AB), 0)
    ci = lax.broadcasted_iota(jnp.int32, (AB, AB), 1)
    bias_p = jnp.where(jnp.logical_and(ci >= ri, j > NCB), 0.0, NEG)
    bias_c = jnp.where(latent, 0.0, NEG)
    bias_n = jnp.where(jnp.logical_and(ci <= ri, jnp.logical_and(latent, j < NCB + NB - 1)), 0.0, NEG)
    gsz = HQ // HKV
    for hk in range(HKV):
        ks = slice(hk * HD, (hk + 1) * HD)
        vs = slice(HKV * HD + hk * HD, HKV * HD + (hk + 1) * HD)
        k_all = jnp.concatenate([kvp_ref[:, ks], kvc_ref[:, ks], kvn_ref[:, ks], kvx_ref[:, ks]], axis=0)
        v_all = jnp.concatenate([kvp_ref[:, vs], kvc_ref[:, vs], kvn_ref[:, vs], kvx_ref[:, vs]], axis=0)
        for h0 in range(hk * gsz, (hk + 1) * gsz, QSTACK):
            qs = jnp.concatenate([q_ref[:, (h0 + g) * HD:(h0 + g + 1) * HD] for g in range(QSTACK)], axis=0)
            s_all = lax.dot_general(qs, k_all, NT_DIMS, preferred_element_type=F32)
            ps, dens = [], []
            for g in range(QSTACK):
                sg = s_all[g * AB:(g + 1) * AB]
                sg = jnp.concatenate([sg[:, 0:AB] + bias_p, sg[:, AB:2 * AB] + bias_c,
                                      sg[:, 2 * AB:3 * AB] + bias_n, sg[:, 3 * AB:]], axis=1)
                snk = sink_ref[h0 + g] * LOG2E
                m = jnp.maximum(jnp.max(sg, axis=-1, keepdims=True), snk)
                e = jnp.exp2(sg - m)
                dens.append(jnp.sum(e, axis=-1, keepdims=True) + jnp.exp2(snk - m))
                ps.append(e.astype(BF))
            o_all = jnp.dot(jnp.concatenate(ps, axis=0), v_all, preferred_element_type=F32)
            for g in range(0, QSTACK, 2):
                o2 = jnp.concatenate([o_all[g * AB:(g + 1) * AB] / dens[g],
                                      o_all[(g + 1) * AB:(g + 2) * AB] / dens[g + 1]], axis=1)
                o_ref[:, (h0 + g) * HD:(h0 + g + 2) * HD] = o2.astype(BF)


def _attention(pb, pkv, sink):
    lat0 = TC // AB

    def qrow(b, j):
        return jnp.where(j < NCB, b * NCB + j, lat0 + b * NB + j - NCB)

    def krow(off):
        def f(b, j):
            i = jnp.clip(j - NCB + off, 0, NB - 1)
            return (lat0 + b * NB + i, 0)
        return f

    return pl.pallas_call(
        _attn_kernel,
        grid=(B, NCB + NB),
        in_specs=[
            pl.BlockSpec((AB, HQ * HD), lambda b, j: (qrow(b, j), 0)),
            pl.BlockSpec((AB, KV_COLS), krow(-1)),
            pl.BlockSpec((AB, KV_COLS), krow(0)),
            pl.BlockSpec((AB, KV_COLS), krow(1)),
            pl.BlockSpec((LC, KV_COLS), lambda b, j: (b, 0)),
            pl.BlockSpec(memory_space=pltpu.SMEM),
        ],
        out_specs=pl.BlockSpec((AB, HQ * HD), lambda b, j: (qrow(b, j), 0)),
        out_shape=jax.ShapeDtypeStruct((R, HQ * HD), BF),
        compiler_params=_cparams(("arbitrary", "arbitrary")),
        name="attention",
    )(pb, pkv, pkv, pkv, pkv, sink)


MG_TN = 512


def _merge_kernel(gla_ref, att_ref, g1_ref, g2_ref, w1_ref, w2_ref, o_ref, w1b_ref, w2b_ref):
    @pl.when(pl.program_id(1) == 0)
    def _():
        w1b_ref[...] = w1_ref[...].astype(BF)
        w2b_ref[...] = w2_ref[...].astype(BF)

    y1 = jnp.dot(gla_ref[...], w1b_ref[...], preferred_element_type=F32)
    y2 = jnp.dot(att_ref[...], w2b_ref[...], preferred_element_type=F32)
    y = _sigmoid(g1_ref[...].astype(F32)) * y1 + _sigmoid(g2_ref[...].astype(F32)) * y2
    o_ref[...] = y.astype(BF)


def _merge(gla, att, pb, w1, w2, l, t0):
    nt = NT_ALL - t0
    g1c = 2048 // MG_TN
    g2c = 4096 // MG_TN
    return pl.pallas_call(
        _merge_kernel,
        grid=(D // MG_TN, nt),
        in_specs=[
            pl.BlockSpec((TM, GH * GDV), lambda j, i: (i + t0, 0)),
            pl.BlockSpec((TM, HQ * HD), lambda j, i: (i + t0, 0)),
            pl.BlockSpec((TM, MG_TN), lambda j, i: (i + t0, g1c + j)),
            pl.BlockSpec((TM, MG_TN), lambda j, i: (i + t0, g2c + j)),
            pl.BlockSpec((None, GH * GDV, MG_TN), lambda j, i: (l, 0, j)),
            pl.BlockSpec((None, HQ * HD, MG_TN), lambda j, i: (l, 0, j)),
        ],
        out_specs=pl.BlockSpec((TM, MG_TN), lambda j, i: (i, j)),
        out_shape=jax.ShapeDtypeStruct((nt * TM, D), BF),
        scratch_shapes=[pltpu.VMEM((GH * GDV, MG_TN), BF), pltpu.VMEM((HQ * HD, MG_TN), BF)],
        compiler_params=_cparams(("arbitrary", "arbitrary")),
        name="merge",
    )(gla, att, pb, pb, w1, w2)


def _residual_norm(y, x, gate, g_post):
    ms = jnp.mean(y * y, axis=-1, keepdims=True)
    return x + gate * (y * lax.rsqrt(ms + EPS) * g_post)


WO_CK = 512


def _outproj_kernel(a_ref, w_ref, x_ref, gt_ref, gp_ref, gn_ref, sc_ref, sh_ref, o_ref, h_ref, wbf_ref, *, nch):
    s = pl.program_id(0)

    @pl.when(s < nch)
    def _():
        wbf_ref[pl.ds(pl.multiple_of(s * WO_CK, WO_CK), WO_CK), :] = w_ref[...].astype(BF)

    @pl.when(s >= nch - 1)
    def _():
        y = jnp.dot(a_ref[...], wbf_ref[...], preferred_element_type=F32)
        xn = _residual_norm(y, x_ref[...], gt_ref[...], gp_ref[...])
        o_ref[...] = xn
        h_ref[...] = _norm_mod(xn, gn_ref[...], sc_ref[...], sh_ref[...]).astype(BF)


def _outproj(a, w_full, l, kdim, tmr, *, t0, a_ctx, x, modl, k_gate, g_post, g_next, modn, k_sc, k_sh, name):
    nch = kdim // WO_CK
    skip = t0 * TM // tmr
    nt = R // tmr - skip
    a_off = skip if a_ctx else 0

    def tile(s):
        return jnp.maximum(s - (nch - 1), 0)

    def mrow(s):
        first = (tile(s) + skip) * tmr
        return jnp.where(first < TC, 2, (first - TC) // L)

    def mspec(k_chunk):
        return pl.BlockSpec((None, 1, D), lambda s: (mrow(s) * 6 + k_chunk, 0, 0))

    vec = pl.BlockSpec((1, D), lambda s: (0, 0))
    row = pl.BlockSpec((tmr, D), lambda s: (tile(s), 0))
    return pl.pallas_call(
        functools.partial(_outproj_kernel, nch=nch),
        grid=(nt + nch - 1,),
        in_specs=[
            pl.BlockSpec((tmr, kdim), lambda s: (tile(s) + a_off, 0)),
            pl.BlockSpec((None, WO_CK, D), lambda s: (l, jnp.minimum(s, nch - 1), 0)),
            pl.BlockSpec((tmr, D), lambda s: (tile(s) + skip, 0)),
            mspec(k_gate), vec, vec, mspec(k_sc), mspec(k_sh),
        ],
        out_specs=[row, row],
        out_shape=[jax.ShapeDtypeStruct((nt * tmr, D), F32), jax.ShapeDtypeStruct((nt * tmr, D), BF)],
        scratch_shapes=[pltpu.VMEM((kdim, D), BF)],
        compiler_params=_cparams(("arbitrary",)),
        name=name,
    )(a, w_full, x, modl, g_post.reshape(1, D), g_next.reshape(1, D), modn, modn)


FF_TN = 1024


def _new_expert(te_ref, i):
    return jnp.logical_or(i == 0, te_ref[i] != te_ref[jnp.maximum(i - 1, 0)])


def _ffn_up_kernel(te_ref, act_ref, a_ref, wg_ref, wu_ref, o_ref, wgb_ref, wub_ref):
    i = pl.program_id(1)

    @pl.when(_new_expert(te_ref, i))
    def _():
        wgb_ref[...] = wg_ref[...].astype(BF)
        wub_ref[...] = wu_ref[...].astype(BF)

    @pl.when(act_ref[i] == 1)
    def _():
        a = a_ref[...]
        yg = jnp.dot(a, wgb_ref[...], preferred_element_type=F32)
        yu = jnp.dot(a, wub_ref[...], preferred_element_type=F32)
        o_ref[...] = (_silu(yg) * yu).astype(BF)

    @pl.when(act_ref[i] == 0)
    def _():
        o_ref[...] = jnp.zeros_like(o_ref)


def _ffn_up(a, wg, wu, lead, te, act):
    nt = a.shape[0] // TM
    wspec = pl.BlockSpec((None, None, D, FF_TN), lambda j, i, te, act: (lead, te[i], 0, j))
    return pl.pallas_call(
        _ffn_up_kernel,
        grid_spec=pltpu.PrefetchScalarGridSpec(
            num_scalar_prefetch=2,
            grid=(pl.cdiv(DFF, FF_TN), nt),
            in_specs=[pl.BlockSpec((TM, D), lambda j, i, te, act: (i, 0)), wspec, wspec],
            out_specs=pl.BlockSpec((TM, FF_TN), lambda j, i, te, act: (i, j)),
            scratch_shapes=[pltpu.VMEM((D, FF_TN), BF), pltpu.VMEM((D, FF_TN), BF)],
        ),
        out_shape=jax.ShapeDtypeStruct((nt * TM, DFF), BF),
        compiler_params=_cparams(("arbitrary", "arbitrary")),
        name="ffn_up",
    )(te, act, a, wg, wu)


DN_TN = 512


def _moe_down_kernel(te_ref, act_ref, a_ref, w_ref, s_ref, o_ref, wb_ref):
    i = pl.program_id(1)

    @pl.when(_new_expert(te_ref, i))
    def _():
        wb_ref[...] = w_ref[...].astype(BF)

    @pl.when(act_ref[i] == 1)
    def _():
        y = jnp.dot(a_ref[...], wb_ref[...], preferred_element_type=F32)
        o_ref[...] = (y * s_ref[...]).astype(BF)

    @pl.when(act_ref[i] == 0)
    def _():
        o_ref[...] = jnp.zeros_like(o_ref)


def _moe_down(u, wd, lead, wrow, te, act):
    nt = u.shape[0] // TM
    return pl.pallas_call(
        _moe_down_kernel,
        grid_spec=pltpu.PrefetchScalarGridSpec(
            num_scalar_prefetch=2,
            grid=(D // DN_TN, nt),
            in_specs=[
                pl.BlockSpec((TM, DFF), lambda j, i, te, act: (i, 0)),
                pl.BlockSpec((None, None, DFF, DN_TN), lambda j, i, te, act: (lead, te[i], 0, j)),
                pl.BlockSpec((TM, 1), lambda j, i, te, act: (i, 0)),
            ],
            out_specs=pl.BlockSpec((TM, DN_TN), lambda j, i, te, act: (i, j)),
            scratch_shapes=[pltpu.VMEM((DFF, DN_TN), BF)],
        ),
        out_shape=jax.ShapeDtypeStruct((nt * TM, D), BF),
        compiler_params=_cparams(("arbitrary", "arbitrary")),
        name="moe_down",
    )(te, act, u, wd, wrow)


def _router_kernel(x_ref, g_ref, sc_ref, sh_ref, wr_ref, br_ref, o_ref, pos_ref, sel_ref, carry_ref):
    @pl.when(pl.program_id(0) == 0)
    def _():
        carry_ref[...] = jnp.zeros_like(carry_ref)

    h = _norm_mod(x_ref[...], g_ref[...], sc_ref[...], sh_ref[...])
    hh, hl = _split(h)
    wh, wl = _split(wr_ref[...])
    nt = (((1,), (1,)), ((), ()))
    lg = (lax.dot_general(wh, hh, nt, preferred_element_type=F32)
          + lax.dot_general(wh, hl, nt, preferred_element_type=F32)
          + lax.dot_general(wl, hh, nt, preferred_element_type=F32) + br_ref[...])
    idx = lax.broadcasted_iota(jnp.int32, lg.shape, 0)
    m1 = jnp.max(lg, axis=0, keepdims=True)
    i1 = jnp.min(jnp.where(lg == m1, idx, NE), axis=0, keepdims=True)
    l2 = jnp.where(idx == i1, NEG, lg)
    m2 = jnp.max(l2, axis=0, keepdims=True)
    i2 = jnp.min(jnp.where(l2 == m2, idx, NE), axis=0, keepdims=True)
    e2 = jnp.exp(m2 - m1)
    w1 = 1.0 / (1.0 + e2)
    w2 = e2 / (1.0 + e2)
    o_ref[...] = jnp.where(idx == i1, w1, 0.0) + jnp.where(idx == i2, w2, 0.0)
    sel = jnp.where(idx == i1, 1.0, jnp.where(idx == i2, 1.0, 0.0))
    si = lax.broadcasted_iota(jnp.int32, (TM, TM), 0)
    ti = lax.broadcasted_iota(jnp.int32, (TM, TM), 1)
    before = jnp.where(si < ti, 1.0, 0.0).astype(BF)
    excl = jnp.dot(sel.astype(BF), before, preferred_element_type=F32)
    pos_ref[...] = (excl + carry_ref[...]).astype(jnp.int32)
    sel_ref[...] = sel.astype(jnp.int32)
    carry_ref[...] += jnp.sum(sel, axis=1, keepdims=True)


def _router(x_lat, g, modl, k_sc, k_sh, w_router, b_router):
    ospec = pl.BlockSpec((NE, TM), lambda i: (0, i))
    return pl.pallas_call(
        _router_kernel,
        grid=(T // TM,),
        in_specs=[
            pl.BlockSpec((TM, D), lambda i: (i, 0)),
            pl.BlockSpec((1, D), lambda i: (0, 0)),
            pl.BlockSpec((None, 1, D), lambda i: (_mrow(i + 1) * 6 + k_sc, 0, 0)),
            pl.BlockSpec((None, 1, D), lambda i: (_mrow(i + 1) * 6 + k_sh, 0, 0)),
            pl.BlockSpec((NE, D), lambda i: (0, 0)),
            pl.BlockSpec((NE, 1), lambda i: (0, 0)),
        ],
        out_specs=[ospec, ospec, ospec],
        out_shape=[jax.ShapeDtypeStruct((NE, T), F32), jax.ShapeDtypeStruct((NE, T), jnp.int32),
                   jax.ShapeDtypeStruct((NE, T), jnp.int32)],
        scratch_shapes=[pltpu.VMEM((NE, 1), F32)],
        compiler_params=_cparams(("arbitrary",)),
        name="router",
    )(x_lat, g.reshape(1, D), modl, modl, w_router.T, b_router.reshape(NE, 1))


NP = 2 * T // TM + NE
DT = 256
NDT = NP * TM // DT
WSTEP = 1408
WTOK = WSTEP + 128
NITEM = 136
CT = 256
NCT = T // CT
ROW_ALIGN = 16
WIN = CT + ROW_ALIGN
WBUF = CT + 128


def _route_tables(pos, sel):
    i32 = jnp.int32
    counts = pos[:, -1] + sel[:, -1]
    ntile = (counts + TM - 1) // TM
    tend = jnp.cumsum(ntile)
    seg = (tend - ntile) * TM
    dest = jnp.where(sel > 0, seg[:, None] + pos, -1)
    d_hi = jnp.max(dest, axis=0)
    d_lo = jnp.sum(dest, axis=0) + (NE - 2) - d_hi
    tiles = jnp.arange(NP, dtype=i32)
    total = tend[-1]
    te = jnp.sum((tiles[:, None] >= tend[None, :]).astype(i32), axis=1)
    te_last = jnp.sum(((total - 1) >= tend).astype(i32))
    act = (tiles < total).astype(i32)
    te = jnp.where(act > 0, te, te_last)
    dtile = jnp.arange(NDT, dtype=i32)
    de = te[dtile // (TM // DT)]
    p_lo = dtile * DT - seg[de]
    has = jnp.logical_and(dtile * DT < total * TM, p_lo < counts[de])
    p_end = jnp.minimum(p_lo + DT, counts[de])
    incl = jnp.take(pos + sel, de, axis=0)
    t_first = jnp.sum((incl <= p_lo[:, None]).astype(i32), axis=1)
    t_last = jnp.sum((incl <= (p_end - 1)[:, None]).astype(i32), axis=1)
    w0 = (t_first // 128) * 128
    nw = jnp.where(has, (t_last + 1 - w0 + WSTEP - 1) // WSTEP, 1)
    cum = jnp.cumsum(nw)
    item = jnp.minimum(jnp.arange(NITEM, dtype=i32), cum[-1] - 1)
    real = jnp.arange(NITEM, dtype=i32) < cum[-1]
    it_tile = jnp.sum((cum[None, :] <= item[:, None]).astype(i32), axis=1)
    k = item - (cum - nw)[it_tile]
    lo = w0[it_tile] + k * WSTEP
    disp = dict(tile=it_tile, expert=de[it_tile], lo=lo,
                tok=jnp.minimum(lo, T - WTOK) // 128,
                valid=jnp.logical_and(real, has[it_tile]).astype(i32),
                first=jnp.logical_and(real, k == 0).astype(i32))
    pbc = pos[:, ::CT]
    r0c = (seg[:, None] + pbc).T.reshape(NCT * NE)
    cntc = (jnp.concatenate([pbc[:, 1:], counts[:, None]], axis=1) - pbc).T.reshape(NCT * NE)
    a0c = jnp.minimum((r0c // ROW_ALIGN) * ROW_ALIGN, NP * TM - WIN)
    comb = dict(a0=a0c, r0=r0c, cnt=cntc)
    return dict(dest=dest, d_hi=d_hi.reshape(T, 1), d_lo=d_lo.reshape(T, 1), te=te, act=act,
                disp=disp, comb=comb)


def _dispatch_kernel(tile_ref, exp_ref, lo_ref, tok_ref, val_ref, first_ref, h_ref, dest_ref, cw_ref,
                     xs_ref, wr_ref):
    i = pl.program_id(0)

    @pl.when(first_ref[i] == 1)
    def _():
        xs_ref[...] = jnp.zeros_like(xs_ref)
        wr_ref[...] = jnp.zeros_like(wr_ref)

    @pl.when(val_ref[i] == 1)
    def _():
        e = exp_ref[i]
        lo = lo_ref[i]
        tok = lax.broadcasted_iota(jnp.int32, (1, WTOK), 1) + tok_ref[i] * 128
        mine = jnp.logical_and(tok >= lo, tok < lo + WSTEP)
        drow = jnp.where(mine, dest_ref[pl.ds(e, 1), :], -1)
        crow = cw_ref[pl.ds(e, 1), :]
        hit = drow == lax.broadcasted_iota(jnp.int32, (DT, WTOK), 0) + tile_ref[i] * DT
        onehot = jnp.where(hit, 1.0, 0.0).astype(BF)
        g = jnp.dot(onehot, h_ref[...], preferred_element_type=F32)
        xs_ref[...] = (xs_ref[...].astype(F32) + g).astype(BF)
        wr_ref[...] += jnp.sum(jnp.where(hit, crow, 0.0), axis=1, keepdims=True)


def _dispatch(h_lat, dest, comb_w, tb):
    def omap(i, tile, *_):
        return (tile[i], 0)

    def rows(i, tile, exp, lo, tok, *_):
        return (tok[i] * 128, 0)

    def lanes(i, tile, exp, lo, tok, *_):
        return (0, tok[i] * 128)

    return pl.pallas_call(
        _dispatch_kernel,
        grid_spec=pltpu.PrefetchScalarGridSpec(
            num_scalar_prefetch=6,
            grid=(NITEM,),
            in_specs=[
                pl.BlockSpec((pl.Element(WTOK), pl.Element(D)), rows),
                pl.BlockSpec((pl.Element(NE), pl.Element(WTOK)), lanes),
                pl.BlockSpec((pl.Element(NE), pl.Element(WTOK)), lanes),
            ],
            out_specs=[pl.BlockSpec((DT, D), omap), pl.BlockSpec((DT, 1), omap)],
        ),
        out_shape=[jax.ShapeDtypeStruct((NP * TM, D), BF), jax.ShapeDtypeStruct((NP * TM, 1), F32)],
        compiler_params=_cparams(("arbitrary",)),
        name="moe_dispatch",
    )(tb["tile"], tb["expert"], tb["lo"], tb["tok"], tb["valid"], tb["first"], h_lat, dest, comb_w)


def _combine_kernel(a0_ref, r0_ref, cnt_ref, y_hbm, dhi_ref, dlo_ref, x_ref, gt_ref, gp_ref, o_ref,
                    ybuf, acc_ref, sem):
    i = pl.program_id(0)
    slot = lax.rem(i, 2)

    def window_copies(tile, buf):
        out = []
        for e in range(NE):
            a0 = pl.multiple_of(a0_ref[tile * NE + e], ROW_ALIGN)
            out.append(pltpu.make_async_copy(y_hbm.at[pl.ds(a0, WIN), :], ybuf.at[buf, e, pl.ds(0, WIN), :],
                                             sem.at[buf, e]))
        return out

    @pl.when(i == 0)
    def _():
        ybuf[:, :, WIN:, :] = jnp.zeros((2, NE, WBUF - WIN, D), BF)
        for cp in window_copies(0, 0):
            cp.start()

    @pl.when(i + 1 < pl.num_programs(0))
    def _():
        for cp in window_copies(i + 1, 1 - slot):
            cp.start()

    for cp in window_copies(i, slot):
        cp.wait()

    dhi = dhi_ref[...]
    dlo = dlo_ref[...]

    def onehot(e, c0, width):
        idx = i * NE + e
        r0 = r0_ref[idx]
        ids = lax.broadcasted_iota(jnp.int32, (1, width), 1) + (a0_ref[idx] + c0)
        ids = jnp.where(jnp.logical_and(ids >= r0, ids < r0 + cnt_ref[idx]), ids, -1)
        return jnp.where(dhi == ids, 1.0, jnp.where(dlo == ids, 1.0, 0.0)).astype(BF)

    acc = jnp.zeros((CT, D), F32)
    for e in range(NE):
        acc = acc + jnp.dot(onehot(e, 0, CT), ybuf[slot, e, 0:CT, :], preferred_element_type=F32)
    acc_ref[...] = acc
    for e in range(NE):
        idx = i * NE + e

        @pl.when(r0_ref[idx] + cnt_ref[idx] > a0_ref[idx] + CT)
        def _():
            acc_ref[...] += jnp.dot(onehot(e, CT, WBUF - CT), ybuf[slot, e, CT:, :], preferred_element_type=F32)

    o_ref[...] = _residual_norm(acc_ref[...], x_ref[...], gt_ref[...], gp_ref[...])


def _combine(y, d_hi, d_lo, x_lat, modl, k_gate, g_post, tb):
    return pl.pallas_call(
        _combine_kernel,
        grid_spec=pltpu.PrefetchScalarGridSpec(
            num_scalar_prefetch=3,
            grid=(NCT,),
            in_specs=[
                pl.BlockSpec(memory_space=pl.ANY),
                pl.BlockSpec((CT, 1), lambda t, *_: (t, 0)),
                pl.BlockSpec((CT, 1), lambda t, *_: (t, 0)),
                pl.BlockSpec((CT, D), lambda t, *_: (t, 0)),
                pl.BlockSpec((None, 1, D), lambda t, *_: (_mrow(t // (TM // CT) + 1) * 6 + k_gate, 0, 0)),
                pl.BlockSpec((1, D), lambda t, *_: (0, 0)),
            ],
            out_specs=pl.BlockSpec((CT, D), lambda t, *_: (t, 0)),
            scratch_shapes=[pltpu.VMEM((2, NE, WBUF, D), BF), pltpu.VMEM((CT, D), F32),
                            pltpu.SemaphoreType.DMA((2, NE))],
        ),
        out_shape=jax.ShapeDtypeStruct((T, D), F32),
        compiler_params=_cparams(("arbitrary",)),
        name="moe_combine",
    )(tb["a0"], tb["r0"], tb["cnt"], y, d_hi, d_lo, x_lat, modl, g_post.reshape(1, D))


def kernel(x, c, ctx, c_ctx, w_mod, b_mod, g_pre_mix, g_post_mix, g_pre_ffn, g_post_ffn,
           w_in, w_lr_f, b_lr_f, w_lr_b, b_lr_b, g_gla, att_sink, w_br_gla, w_br_att, w_out,
           w_ffn_gate, w_ffn_up, w_ffn_down, w_router, b_router, w_exp_gate, w_exp_up, w_exp_down):
    cvec = jnp.zeros((8, D), F32).at[0:B].set(c).at[B].set(c_ctx)
    mod = _modulation(cvec, w_mod, b_mod).reshape(DEPTH, 8 * 6, 1, D)
    cos, sin_a, sin_b = _rope_tables()
    wt = jnp.swapaxes(w_in, 1, 2)

    xa, h = _prenorm(ctx.reshape(TC, D), x.reshape(T, D), g_pre_mix[0], mod[0], 1, 0)
    for l in range(DEPTH):
        last = l == DEPTH - 1
        t0 = 1 if last else 0
        modl = mod[l]
        pa = _proj(h, wt, l, 0, C_LF, 2048, BF, "proj_gla")
        pb, pkv = _proj_att(h, wt, l, (cos, sin_a, sin_b))
        lr = _proj(h, wt, l, C_LF, 2 * GRANK, 2 * GRANK, F32, "proj_decay")

        wlr = jnp.zeros((2, 2 * GRANK, GH * GDK), F32)
        wlr = wlr.at[0, 0:GRANK].set(w_lr_f[l]).at[1, GRANK:2 * GRANK].set(w_lr_b[l])
        blr = jnp.stack([b_lr_f[l], b_lr_b[l]]).reshape(2, 1, GH * GDK)
        o_f, o_b = _gla_scan(_gla_prep(pa, lr, wlr, blr), pa)
        gla = _gla_readout(o_f, o_b, pa, g_gla[l])
        att = _attention(pb, pkv, att_sink[l])
        mm = _merge(gla, att, pb, w_br_gla, w_br_att, l, t0)

        xa, h2 = _outproj(mm, w_out, l, D, TM, t0=t0, a_ctx=False, x=xa, modl=modl, k_gate=2, name="outproj",
                          g_post=g_post_mix[l], g_next=g_pre_ffn[l], modn=modl, k_sc=4, k_sh=3)
        if not last:
            ones = jnp.ones((NT_ALL,), jnp.int32)
            u = _ffn_up(h2, w_ffn_gate[:, None], w_ffn_up[:, None], l // 2, 0 * ones, ones)
            xa, h = _outproj(u, w_ffn_down, l // 2, DFF, TM // 2, t0=0, a_ctx=True, x=xa, modl=modl, k_gate=5,
                             name="ffn_down", g_post=g_post_ffn[l], g_next=g_pre_mix[l + 1], modn=mod[l + 1],
                             k_sc=1, k_sh=0)
        else:
            comb_w, pos, sel = _router(xa, g_pre_ffn[l], modl, 4, 3, w_router[l // 2], b_router[l // 2])
            rt = _route_tables(pos, sel)
            xs, wrow = _dispatch(h2, rt["dest"], comb_w, rt["disp"])
            u = _ffn_up(xs, w_exp_gate, w_exp_up, l // 2, rt["te"], rt["act"])
            y = _moe_down(u, w_exp_down, l // 2, wrow, rt["te"], rt["act"])
            xa = _combine(y, rt["d_hi"], rt["d_lo"], xa, modl, 5, g_post_ffn[l], rt["comb"])
    return xa.reshape(B, L, D)
```
